```python
import math
import jax, jax.numpy as jnp
from jax import lax
import numpy as np

D_MODEL = 1024
BATCH = 8
SEQ = 2048
DEPTH = 1

ATTN_HEADS = 8
ATTN_KV_HEADS = 2
HEAD_DIM = 64
ATTN_WIDTH = ATTN_HEADS * HEAD_DIM
KV_WIDTH = ATTN_KV_HEADS * HEAD_DIM
IDX_HEADS = 8
IDX_DIM = 64
TOPK_MAX = 256
Q_BLOCK = 128
RWKV_HEADS = 8
RWKV_HEAD_DIM = 64
RWKV_WIDTH = RWKV_HEADS * RWKV_HEAD_DIM
LORA_DECAY = 32
LORA_AAA = 32
LORA_GATE = 64
RWKV_IN = 3 * RWKV_WIDTH + LORA_DECAY + LORA_AAA + LORA_GATE
D_FF = 2816
CONV_WIDTH = 3
LN_EPS = 1e-5
GN_EPS = 64e-5
DEEPNORM_ALPHA = (2.0 * DEPTH) ** 0.25
DEEPNORM_BETA = (8.0 * DEPTH) ** -0.25
IN_SPLITS = (ATTN_WIDTH, KV_WIDTH, KV_WIDTH, IDX_HEADS * IDX_DIM, IDX_DIM, IDX_HEADS,
             RWKV_IN, D_MODEL, D_MODEL)
IN_WIDTH = sum(IN_SPLITS)

kernel_name = "dsa_rwkv7_gated_hybrid_deepnorm"


def _split(z, sizes):
    return jnp.split(z, np.cumsum(sizes)[:-1].tolist(), axis=-1)


def _layer_norm(x, w, b, eps=LN_EPS):
    xf = x.astype(jnp.float32)
    mu = jnp.mean(xf, axis=-1, keepdims=True)
    var = jnp.mean(jnp.square(xf - mu), axis=-1, keepdims=True)
    return ((xf - mu) * lax.rsqrt(var + eps)).astype(x.dtype) * w + b


def _shift(z, n):
    return jnp.pad(z[:, : z.shape[1] - n], ((0, 0), (n, 0), (0, 0)))


def _dsa_attention(q, qi, wi, k, v, ki):
    B, L = k.shape[0], k.shape[1]
    n_sel = min(TOPK_MAX, L // 4)
    nb = L // Q_BLOCK
    rep = ATTN_HEADS // ATTN_KV_HEADS
    slopes = jnp.exp2(-8.0 * jnp.arange(1, ATTN_HEADS + 1, dtype=jnp.float32) / ATTN_HEADS)
    slopes = slopes.reshape(ATTN_KV_HEADS, rep)
    s_pos = jnp.arange(L)

    def block(args):
        q_b, qi_b, w_b, blk_id = args
        t_pos = blk_id * Q_BLOCK + jnp.arange(Q_BLOCK)
        rel = jax.nn.relu(jnp.einsum('bthd,bsd->bths', qi_b, ki).astype(jnp.float32) * IDX_DIM ** -0.5)
        score = jnp.einsum('bths,bth->bts', rel, w_b.astype(jnp.float32) * IDX_HEADS ** -0.5)
        causal = s_pos[None, None, :] <= t_pos[None, :, None]
        score = jnp.where(causal, score, -jnp.inf)
        _, idx = lax.top_k(score, n_sel)
        valid = idx <= t_pos[None, :, None]
        k_sel = jax.vmap(lambda kb, ib: kb[ib])(k, idx)
        v_sel = jax.vmap(lambda vb, ib: vb[ib])(v, idx)
        qg = q_b.reshape(B, Q_BLOCK, ATTN_KV_HEADS, rep, HEAD_DIM)
        logits = jnp.einsum('btgrd,btkgd->btgrk', qg, k_sel).astype(jnp.float32) * HEAD_DIM ** -0.5
        dist = (t_pos[None, :, None] - idx).astype(jnp.float32)
        logits = logits - slopes[None, None, :, :, None] * dist[:, :, None, None, :]
        logits = jnp.where(valid[:, :, None, None, :], logits, -jnp.inf)
        p = jax.nn.softmax(logits, axis=-1).astype(v.dtype)
        out = jnp.einsum('btgrk,btkgd->btgrd', p, v_sel)
        return out.reshape(B, Q_BLOCK, ATTN_WIDTH)

    to_blocks = lambda z: jnp.swapaxes(z.reshape((B, nb, Q_BLOCK) + z.shape[2:]), 0, 1)
    out = lax.map(block, (to_blocks(q), to_blocks(qi), to_blocks(wi), jnp.arange(nb)))
    return jnp.swapaxes(out, 0, 1).reshape(B, L, ATTN_WIDTH)


def _rwkv_step(state, inp):
    r_t, w_t, k_t, v_t, a_t, b_t = inp
    sa = jnp.einsum('bhvk,bhk->bhv', state, a_t)
    state = (state * w_t[:, :, None, :] + sa[..., None] * b_t[:, :, None, :]
             + v_t[..., None] * k_t[:, :, None, :])
    y = jnp.einsum('bhvk,bhk->bhv', state, r_t)
    return state, y


def _rwkv7(p, mu, w0, w_up, a0, a_up, g_up, k_k, k_a, r_k, ln_w, ln_b):
    B, L, _ = p.shape
    f32 = jnp.float32
    p = p + (_shift(p, 1) - p) * mu
    r, k, v, wd, ad, gd = _split(p, (RWKV_WIDTH, RWKV_WIDTH, RWKV_WIDTH, LORA_DECAY, LORA_AAA, LORA_GATE))
    log_w = -jax.nn.softplus(-(w0 + jnp.tanh(wd) @ w_up)) - 0.5
    decay = jnp.exp(-jnp.exp(log_w.astype(f32)))
    a = jax.nn.sigmoid(a0 + ad @ a_up)
    g = jax.nn.sigmoid(gd) @ g_up
    heads = lambda z: z.reshape(B, L, RWKV_HEADS, RWKV_HEAD_DIM).astype(f32)
    head_param = lambda z: z.reshape(RWKV_HEADS, RWKV_HEAD_DIM).astype(f32)
    r_h, k_h, v_h, a_h, w_h = heads(r), heads(k), heads(v), heads(a), heads(decay)
    kk = k_h * head_param(k_k)
    kk = kk * lax.rsqrt(jnp.maximum(jnp.sum(kk * kk, axis=-1, keepdims=True), 1e-24))
    k_h = k_h * (1.0 + (a_h - 1.0) * head_param(k_a))
    seq_first = lambda z: jnp.swapaxes(z, 0, 1)
    state0 = jnp.zeros((B, RWKV_HEADS, RWKV_HEAD_DIM, RWKV_HEAD_DIM), f32)
    xs = (seq_first(r_h), seq_first(w_h), seq_first(k_h), seq_first(v_h),
          seq_first(-kk), seq_first(kk * a_h))
    _, y = lax.scan(_rwkv_step, state0, xs)
    y = jnp.swapaxes(y, 0, 1)
    mean = jnp.mean(y, axis=-1, keepdims=True)
    var = jnp.mean(jnp.square(y - mean), axis=-1, keepdims=True)
    yn = ((y - mean) * lax.rsqrt(var + GN_EPS)).reshape(B, L, RWKV_WIDTH)
    yn = yn * ln_w.astype(f32) + ln_b.astype(f32)
    bonus = jnp.sum(r_h * k_h * r_k.astype(f32), axis=-1, keepdims=True) * v_h
    out = (yn + bonus.reshape(B, L, RWKV_WIDTH)) * g.astype(f32)
    return out.astype(p.dtype)


def _conv_ffn(x, w_up, conv_w, conv_b, w_down):
    h = x @ w_up
    h = conv_w[0] * _shift(h, 2) + conv_w[1] * _shift(h, 1) + conv_w[2] * h + conv_b
    gate, up = jnp.split(h, 2, axis=-1)
    return (jax.nn.silu(gate) * up) @ w_down


def setup_inputs(seed: int = 0) -> dict:
    key = jax.random.key(seed)
    ks = jax.random.split(key, 32)
    f32 = jnp.float32
    nrm = lambda k, shape, scale: jax.random.normal(k, shape, f32) * scale
    L = DEPTH
    return {
        "x": nrm(ks[0], (BATCH, SEQ, D_MODEL), 1.0),
        "w_in": nrm(ks[1], (L, D_MODEL, IN_WIDTH), D_MODEL ** -0.5),
        "idx_k_norm_w": 1.0 + nrm(ks[2], (L, IDX_DIM), 0.02),
        "idx_k_norm_b": nrm(ks[3], (L, IDX_DIM), 0.02),
        "rwkv_mu": jax.random.uniform(ks[4], (L, RWKV_IN), f32),
        "rwkv_w0": jax.random.uniform(ks[5], (L, RWKV_WIDTH), f32, -6.0, -1.0),
        "rwkv_w_up": nrm(ks[6], (L, LORA_DECAY, RWKV_WIDTH), 0.1 * LORA_DECAY ** -0.5),
        "rwkv_a0": nrm(ks[7], (L, RWKV_WIDTH), 0.1),
        "rwkv_a_up": nrm(ks[8], (L, LORA_AAA, RWKV_WIDTH), LORA_AAA ** -0.5),
        "rwkv_g_up": nrm(ks[9], (L, LORA_GATE, RWKV_WIDTH), LORA_GATE ** -0.5),
        "rwkv_k_k": 0.85 + nrm(ks[10], (L, RWKV_WIDTH), 0.05),
        "rwkv_k_a": 1.0 + nrm(ks[11], (L, RWKV_WIDTH), 0.05),
        "rwkv_r_k": nrm(ks[12], (L, RWKV_HEADS, RWKV_HEAD_DIM), 0.1),
        "rwkv_ln_w": 1.0 + nrm(ks[13], (L, RWKV_WIDTH), 0.02),
        "rwkv_ln_b": nrm(ks[14], (L, RWKV_WIDTH), 0.02),
        "w_branch_attn": nrm(ks[15], (L, ATTN_WIDTH, D_MODEL), ATTN_WIDTH ** -0.5 * DEEPNORM_BETA),
        "w_branch_rwkv": nrm(ks[16], (L, RWKV_WIDTH, D_MODEL), RWKV_WIDTH ** -0.5 * DEEPNORM_BETA),
        "w_out": nrm(ks[17], (L, D_MODEL, D_MODEL), D_MODEL ** -0.5 * DEEPNORM_BETA),
        "ln1_w": 1.0 + nrm(ks[18], (L, D_MODEL), 0.02),
        "ln1_b": nrm(ks[19], (L, D_MODEL), 0.02),
        "w_up": nrm(ks[20], (L, D_MODEL, 2 * D_FF), D_MODEL ** -0.5),
        "conv_w": nrm(ks[21], (L, CONV_WIDTH, 2 * D_FF), CONV_WIDTH ** -0.5),
        "conv_b": nrm(ks[22], (L, 2 * D_FF), 0.02),
        "w_down": nrm(ks[23], (L, D_FF, D_MODEL), D_FF ** -0.5 * DEEPNORM_BETA),
        "ln2_w": 1.0 + nrm(ks[24], (L, D_MODEL), 0.02),
        "ln2_b": nrm(ks[25], (L, D_MODEL), 0.02),
    }


def reference(x, w_in, idx_k_norm_w, idx_k_norm_b, rwkv_mu, rwkv_w0, rwkv_w_up, rwkv_a0,
              rwkv_a_up, rwkv_g_up, rwkv_k_k, rwkv_k_a, rwkv_r_k, rwkv_ln_w, rwkv_ln_b,
              w_branch_attn, w_branch_rwkv, w_out, ln1_w, ln1_b, w_up, conv_w, conv_b,
              w_down, ln2_w, ln2_b):
    B, L, _ = x.shape
    h = x
    for l in range(DEPTH):
        P = h @ w_in[l]
        q, k, v, qi, ki, wi, p_rwkv, gate_a, gate_r = _split(P, IN_SPLITS)
        ki = _layer_norm(ki, idx_k_norm_w[l], idx_k_norm_b[l])
        y_attn = _dsa_attention(
            q.reshape(B, L, ATTN_HEADS, HEAD_DIM),
            qi.reshape(B, L, IDX_HEADS, IDX_DIM), wi,
            k.reshape(B, L, ATTN_KV_HEADS, HEAD_DIM),
            v.reshape(B, L, ATTN_KV_HEADS, HEAD_DIM), ki)
        y_rwkv = _rwkv7(p_rwkv, rwkv_mu[l], rwkv_w0[l], rwkv_w_up[l], rwkv_a0[l], rwkv_a_up[l],
                        rwkv_g_up[l], rwkv_k_k[l], rwkv_k_a[l], rwkv_r_k[l], rwkv_ln_w[l], rwkv_ln_b[l])
        mix = (jax.nn.sigmoid(gate_a) * (y_attn @ w_branch_attn[l])
               + jax.nn.sigmoid(gate_r) * (y_rwkv @ w_branch_rwkv[l]))
        h = _layer_norm(DEEPNORM_ALPHA * h + mix @ w_out[l], ln1_w[l], ln1_b[l])
        f = _conv_ffn(h, w_up[l], conv_w[l], conv_b[l], w_down[l])
        h = _layer_norm(DEEPNORM_ALPHA * h + f, ln2_w[l], ln2_b[l])
    return h
```

```python
import functools

import jax
import jax.numpy as jnp
import numpy as np
from jax import lax
from jax.experimental import pallas as pl
from jax.experimental.pallas import tpu as pltpu

F32 = jnp.float32
BF16 = jnp.bfloat16
I32 = jnp.int32

LANES = 128
HEAD_DIM = 64
HEADS = 8
KV_HEADS = 2
TOPK_MAX = 256
LORA_W = 128
LN_EPS = 1e-5
GN_EPS = 64e-5
INT_MIN = -(2 ** 31)
VMEM_LIMIT = 48 * 1024 * 1024


def _nt(a, b):
    return lax.dot_general(a, b, (((1,), (1,)), ((), ())), preferred_element_type=F32)


def _nn(a, b):
    return lax.dot_general(a, b, (((1,), (0,)), ((), ())), preferred_element_type=F32)


def _split2(x):
    hi = x.astype(BF16)
    lo = (x - hi.astype(F32)).astype(BF16)
    return hi, lo


def _mm3(a, b, dot=_nn):
    ah, al = _split2(a)
    bh, bl = _split2(b)
    return dot(ah, bh) + (dot(ah, bl) + dot(al, bh))


def _mm_exact_lhs(a_bf16, b):
    b0 = b.astype(BF16)
    r1 = b - b0.astype(F32)
    b1 = r1.astype(BF16)
    b2 = (r1 - b1.astype(F32)).astype(BF16)
    return _nn(a_bf16, b0) + (_nn(a_bf16, b1) + _nn(a_bf16, b2))


def _inproj_kernel(x_ref, wa_ref, wr_ref, wg_ref, lnw_ref, lnb_ref,
                   q_ref, k2_ref, v2_ref, qi_ref, ki2_ref, wi_ref, pr_ref, ga_ref, gr_ref):
    xb = x_ref[...].astype(BF16)
    pa = _nn(xb, wa_ref[...])
    q_ref[...] = (pa[:, 0:512] * (HEAD_DIM ** -0.5)).astype(BF16)
    k2_ref[...] = pa[:, 512:768].astype(BF16)
    v2_ref[...] = pa[:, 768:1024].astype(BF16)
    qi_ref[...] = pa[:, 1024:1536].astype(BF16)
    ki = pa[:, 1536:1664]
    mu = jnp.mean(ki, axis=-1, keepdims=True)
    var = jnp.mean(jnp.square(ki - mu), axis=-1, keepdims=True)
    ki2_ref[...] = ((ki - mu) * lax.rsqrt(var + LN_EPS) * lnw_ref[...] + lnb_ref[...]).astype(BF16)
    wi_ref[...] = pa[:, 1664:1792]
    pr_ref[...] = _nn(xb, wr_ref[...])
    pg = _nn(xb, wg_ref[...])
    d = ga_ref.shape[-1]
    ga_ref[...] = pg[:, :d]
    gr_ref[...] = pg[:, d:]


def _inproj(x2, wa, wr, wg, lnw2, lnb2, tm):
    m, d = x2.shape
    full = lambda a: pl.BlockSpec(a.shape, lambda i: (0,) * a.ndim)
    row = lambda n: pl.BlockSpec((tm, n), lambda i: (i, 0))
    outs = [(512, BF16), (256, BF16), (256, BF16), (512, BF16), (128, BF16), (128, F32),
            (wr.shape[1], F32), (d, F32), (d, F32)]
    return pl.pallas_call(
        _inproj_kernel,
        grid=(m // tm,),
        in_specs=[row(d), full(wa), full(wr), full(wg), full(lnw2), full(lnb2)],
        out_specs=[row(n) for n, _ in outs],
        out_shape=[jax.ShapeDtypeStruct((m, n), dt) for n, dt in outs],
        compiler_params=pltpu.CompilerParams(
            dimension_semantics=("parallel",), vmem_limit_bytes=VMEM_LIMIT),
        name="inproj",
    )(x2, wa, wr, wg, lnw2, lnb2)


def _attn_kernel(q_ref, qi_ref, wi_ref, k2_ref, v2_ref, ki2_ref, o_ref,
                 key_ref, tpos_ref, *, tq, seq, n_sel, cw):
    t0 = pl.program_id(1) * tq
    lane = lax.broadcasted_iota(I32, (tq, LANES), 1)
    lo = lane < HEAD_DIM
    n_cc = seq // cw
    n_lc = seq // LANES

    wi = wi_ref[0] * (HEAD_DIM ** -0.5 * HEADS ** -0.5)
    ki2 = ki2_ref[0]
    row_c = t0 + lax.broadcasted_iota(I32, (tq, cw), 0)
    col_c = lax.broadcasted_iota(I32, (tq, cw), 1)
    for c in range(n_cc):
        kc = ki2[c * cw:(c + 1) * cw, :]
        s = jnp.zeros((tq, cw), F32)
        for p in range(HEADS // 2):
            qip = qi_ref[0, :, p * LANES:(p + 1) * LANES]
            for j in range(2):
                h = 2 * p + j
                lhs = jnp.where(lo if j == 0 else jnp.logical_not(lo), qip, jnp.zeros_like(qip))
                x = _nt(lhs, kc)
                s = s + jnp.maximum(x, 0.0) * wi[:, h:h + 1]
        bits = lax.bitcast_convert_type(s, I32)
        key = jnp.where(bits < 0, bits ^ jnp.int32(0x7FFFFFFF), bits)
        key = jnp.where(s == 0.0, jnp.int32(0), key)
        key = jnp.where(col_c + c * cw <= row_c, key, jnp.int32(INT_MIN))
        key_ref[:, c * cw:(c + 1) * cw] = key

    kf = jnp.float32(n_sel)

    def count(pred_fn, ref):
        acc = jnp.zeros((tq, LANES), F32)
        for c in range(n_lc):
            acc = acc + jnp.where(pred_fn(ref[:, c * LANES:(c + 1) * LANES]), 1.0, 0.0)
        return jnp.sum(acc, axis=1, keepdims=True)

    cnt0 = count(lambda kc: kc >= 0, key_ref)
    thr = jnp.where(cnt0 >= kf, jnp.int32(0), jnp.int32(INT_MIN))

    def bis_body(i, t):
        cand = t + lax.shift_left(jnp.int32(1), jnp.int32(30) - i)
        cnt = count(lambda kc: kc >= cand, key_ref)
        return jnp.where(cnt >= kf, cand, t)

    thr = lax.fori_loop(0, 31, bis_body, thr)
    thr = jnp.maximum(thr, jnp.int32(INT_MIN + 1))

    need = kf - count(lambda kc: kc > thr, key_ref)
    big = jnp.int32(seq)
    for c in range(n_lc):
        kc = key_ref[:, c * LANES:(c + 1) * LANES]
        tpos_ref[:, c * LANES:(c + 1) * LANES] = jnp.where(kc == thr, lane + c * LANES, big)

    nbits = int(np.log2(seq))

    def tie_body(i, j):
        cand = j + lax.shift_left(jnp.int32(1), jnp.int32(nbits - 1) - i)
        cnt = count(lambda pc: pc < cand, tpos_ref)
        return jnp.where(cnt < need, cand, j)

    jmax = lax.fori_loop(0, nbits, tie_body, jnp.zeros((tq, 1), I32))

    for c in range(n_lc):
        sl = slice(c * LANES, (c + 1) * LANES)
        keep = jnp.where(key_ref[:, sl] > thr, 0.0,
                         jnp.where(tpos_ref[:, sl] <= jmax, 0.0, -jnp.inf))
        tpos_ref[:, sl] = lax.bitcast_convert_type(keep.astype(F32), I32)

    row = t0 + lax.broadcasted_iota(I32, (tq, seq), 0)
    col = lax.broadcasted_iota(I32, (tq, seq), 1)
    dist = (row - col).astype(F32)
    bias = lax.bitcast_convert_type(tpos_ref[...], F32)
    for p in range(HEADS // 2):
        g = (2 * p) // (HEADS // KV_HEADS)
        qp = q_ref[0, :, p * LANES:(p + 1) * LANES]
        k2g = k2_ref[0, :, g * LANES:(g + 1) * LANES]
        v2g = v2_ref[0, :, g * LANES:(g + 1) * LANES]
        outs = []
        for j in range(2):
            h = 2 * p + j
            lhs = jnp.where(lo if j == 0 else jnp.logical_not(lo), qp, jnp.zeros_like(qp))
            logit = _nt(lhs, k2g) - (2.0 ** -(h + 1)) * dist + bias
            m = jnp.max(logit, axis=1, keepdims=True)
            e = jnp.exp(logit - m)
            ssum = jnp.sum(e, axis=1, keepdims=True)
            outs.append(_nn(e.astype(BF16), v2g) / ssum)
        o_ref[0, :, p * LANES:(p + 1) * LANES] = jnp.where(lo, outs[0], outs[1]).astype(o_ref.dtype)


def _attention(q, qi, wi, k2, v2, ki2, tq):
    b, seq, _ = q.shape
    n_sel = min(TOPK_MAX, seq // 4)
    cw = min(512, seq)
    qblk = lambda n: pl.BlockSpec((1, tq, n), lambda bi, i: (bi, i, 0))
    kblk = lambda n: pl.BlockSpec((1, seq, n), lambda bi, i: (bi, 0, 0))
    return pl.pallas_call(
        functools.partial(_attn_kernel, tq=tq, seq=seq, n_sel=n_sel, cw=cw),
        grid=(b, seq // tq),
        in_specs=[qblk(512), qblk(512), qblk(128), kblk(256), kblk(256), kblk(128)],
        out_specs=qblk(512),
        out_shape=jax.ShapeDtypeStruct((b, seq, 512), BF16),
        scratch_shapes=[pltpu.VMEM((tq, seq), I32), pltpu.VMEM((tq, seq), I32)],
        compiler_params=pltpu.CompilerParams(
            dimension_semantics=("parallel", "parallel"), vmem_limit_bytes=VMEM_LIMIT),
        name="dsa_attention",
    )(q, qi, wi, k2, v2, ki2)


def _rwkv_kernel(pr_ref, mu_ref, w0_ref, wup_ref, a0_ref, aup_ref, gup_ref, kk_ref, ka_ref,
                 rk_ref, lnw_ref, lnb_ref, o_ref, state_ref, prev_ref, *, c):
    width = o_ref.shape[-1]
    n_pairs = width // LANES

    @pl.when(pl.program_id(1) == 0)
    def _():
        state_ref[...] = jnp.zeros_like(state_ref)
        prev_ref[...] = jnp.zeros_like(prev_ref)

    p = pr_ref[0]
    rows = lax.broadcasted_iota(I32, p.shape, 0)
    shifted = jnp.where(rows == 0, prev_ref[7:8, :], pltpu.roll(p, 1, 0))
    prev_ref[...] = p[c - 8:c, :]
    ps = p + (shifted - p) * mu_ref[...]

    r = ps[:, 0:width]
    k = ps[:, width:2 * width]
    v = ps[:, 2 * width:3 * width]
    lora = ps[:, 3 * width:3 * width + LORA_W]
    zw = w0_ref[...] + _mm3(jnp.tanh(lora), wup_ref[...])
    nz = -zw
    log_w = -(jnp.maximum(nz, 0.0) + jnp.log(1.0 + jnp.exp(-jnp.abs(nz)))) - 0.5
    lw = -jnp.exp(log_w)
    a_sig = jax.nn.sigmoid(a0_ref[...] + _mm3(lora, aup_ref[...]))
    gate = _mm3(jax.nn.sigmoid(lora), gup_ref[...])

    li = lax.broadcasted_iota(I32, (LANES, LANES), 0)
    lj = lax.broadcasted_iota(I32, (LANES, LANES), 1)
    same_head = (li // HEAD_DIM) == (lj // HEAD_DIM)
    ones_bd = jnp.where(same_head, 1.0, 0.0).astype(BF16)
    tri_c = jnp.where(lax.broadcasted_iota(I32, (c, c), 0) >= lax.broadcasted_iota(I32, (c, c), 1),
                      1.0, 0.0).astype(BF16)
    cum = _mm_exact_lhs(tri_c, lw)
    tot = cum[c - 1:c, :]

    lane = lax.broadcasted_iota(I32, (c, LANES), 1)
    lo = lane < HEAD_DIM
    strict = li > lj
    incl = li >= lj
    eye = jnp.where(li == lj, 1.0, 0.0)

    def stack(x):
        return jnp.concatenate([jnp.where(lo, x, 0.0), jnp.where(lo, 0.0, x)], axis=0)

    def head_sum(x):
        return _mm_exact_rhs(x, ones_bd)

    for pi in range(n_pairs):
        sl = slice(pi * LANES, (pi + 1) * LANES)
        r_p, k_p, v_p = r[:, sl], k[:, sl], v[:, sl]
        a_p = a_sig[:, sl]
        kk = k_p * kk_ref[:, sl]
        kk = kk * lax.rsqrt(jnp.maximum(head_sum(kk * kk), 1e-24))
        kmod = k_p * (1.0 + (a_p - 1.0) * ka_ref[:, sl])
        avec = -kk
        bvec = kk * a_p

        cum_p = cum[:, sl]
        lw_p = lw[:, sl]
        tot_p = tot[:, sl]
        p_incl = jnp.exp(cum_p)
        p_excl = jnp.exp(cum_p - lw_p)
        p_inv = jnp.exp(-cum_p)
        p_end = jnp.exp(tot_p - cum_p)

        at = stack(avec * p_excl)
        rt = stack(r_p * p_incl)
        bt = stack(bvec * p_inv)
        kt = stack(kmod * p_inv)
        bh = stack(bvec * p_end)
        kh = stack(kmod * p_end)
        vs = stack(v_p)

        n_ab = jnp.where(strict, _mm3(at, bt, _nt), 0.0)
        m_ak = jnp.where(strict, _mm3(at, kt, _nt), 0.0)
        m_rb = jnp.where(incl, _mm3(rt, bt, _nt), 0.0)
        m_rk = jnp.where(incl, _mm3(rt, kt, _nt), 0.0)

        tinv = eye + n_ab
        npow = n_ab
        for _ in range(int(np.log2(c)) - 1):
            npow = _mm3(npow, npow)
            tinv = tinv + _mm3(tinv, npow)

        state = state_ref[pi]
        x0 = _mm3(at, state, _nt)
        us = _mm3(tinv, x0 + _mm3(m_ak, vs))
        ys = _mm3(rt, state, _nt) + _mm3(m_rb, us) + _mm3(m_rk, vs)
        y = ys[0:c, :] + ys[c:2 * c, :]
        state_ref[pi] = (state * jnp.exp(tot_p)
                         + _mm3(us.T, bh) + _mm3(vs.T, kh))

        mean = head_sum(y) * (1.0 / HEAD_DIM)
        yc = y - mean
        var = head_sum(yc * yc) * (1.0 / HEAD_DIM)
        yn = yc * lax.rsqrt(var + GN_EPS) * lnw_ref[:, sl] + lnb_ref[:, sl]
        bonus = head_sum(r_p * kmod * rk_ref[:, sl]) * v_p
        o_ref[0, :, sl] = ((yn + bonus) * gate[:, sl]).astype(o_ref.dtype)


def _mm_exact_rhs(a, b_bf16):
    a0 = a.astype(BF16)
    r1 = a - a0.astype(F32)
    a1 = r1.astype(BF16)
    a2 = (r1 - a1.astype(F32)).astype(BF16)
    return _nn(a0, b_bf16) + (_nn(a1, b_bf16) + _nn(a2, b_bf16))


def _rwkv(pr, vecs, mats, c):
    b, seq, pw = pr.shape
    width = (pw - LORA_W) // 3
    full = lambda a: pl.BlockSpec(a.shape, lambda bi, i: (0,) * a.ndim)
    mu, w0, a0, kk, ka, rk, lnw, lnb = vecs
    wup, aup, gup = mats
    return pl.pallas_call(
        functools.partial(_rwkv_kernel, c=c),
        grid=(b, seq // c),
        in_specs=[pl.BlockSpec((1, c, pw), lambda bi, i: (bi, i, 0)),
                  full(mu), full(w0), full(wup), full(a0), full(aup), full(gup),
                  full(kk), full(ka), full(rk), full(lnw), full(lnb)],
        out_specs=pl.BlockSpec((1, c, width), lambda bi, i: (bi, i, 0)),
        out_shape=jax.ShapeDtypeStruct((b, seq, width), BF16),
        scratch_shapes=[pltpu.VMEM((width // LANES, LANES, LANES), F32),
                        pltpu.VMEM((8, pw), F32)],
        compiler_params=pltpu.CompilerParams(
            dimension_semantics=("parallel", "arbitrary"), vmem_limit_bytes=VMEM_LIMIT),
        name="rwkv7",
    )(pr, mu, w0, wup, a0, aup, gup, kk, ka, rk, lnw, lnb)


def _layer_norm(z, w, b):
    mu = jnp.mean(z, axis=-1, keepdims=True)
    zc = z - mu
    var = jnp.mean(zc * zc, axis=-1, keepdims=True)
    return zc * lax.rsqrt(var + LN_EPS) * w + b


def _merge_kernel(x_ref, ya_ref, yr_ref, ga_ref, gr_ref, wa_ref, wb_ref, wo_ref, lnw_ref, lnb_ref,
                  o_ref, *, alpha):
    mix = (jax.nn.sigmoid(ga_ref[...]) * _nn(ya_ref[...], wa_ref[...])
           + jax.nn.sigmoid(gr_ref[...]) * _nn(yr_ref[...], wb_ref[...]))
    z = alpha * x_ref[...] + _nn(mix.astype(BF16), wo_ref[...])
    o_ref[...] = _layer_norm(z, lnw_ref[...], lnb_ref[...])


def _merge(x2, ya, yr, ga, gr, wa, wb, wo, lnw, lnb, tm, alpha):
    m, d = x2.shape
    full = lambda a: pl.BlockSpec(a.shape, lambda i: (0,) * a.ndim)
    row = lambda n: pl.BlockSpec((tm, n), lambda i: (i, 0))
    return pl.pallas_call(
        functools.partial(_merge_kernel, alpha=alpha),
        grid=(m // tm,),
        in_specs=[row(d), row(ya.shape[1]), row(yr.shape[1]), row(d), row(d),
                  full(wa), full(wb), full(wo), full(lnw), full(lnb)],
        out_specs=row(d),
        out_shape=jax.ShapeDtypeStruct((m, d), F32),
        compiler_params=pltpu.CompilerParams(
            dimension_semantics=("parallel",), vmem_limit_bytes=VMEM_LIMIT),
        name="merge_out_ln",
    )(x2, ya, yr, ga, gr, wa, wb, wo, lnw, lnb)


HALO = 8


def _ffn_kernel(h_ref, halo_ref, wug_ref, wuu_ref, cwg_ref, cwu_ref, cbg_ref, cbu_ref, wd_ref,
                lnw_ref, lnb_ref, o_ref, acc_ref, *, alpha, tiles_per_seq):
    i = pl.program_id(0)
    j = pl.program_id(1)
    tm = h_ref.shape[0]
    h = h_ref[...]
    first = (i % tiles_per_seq) == 0
    halo = jnp.where(first, 0.0, halo_ref[...])
    hb = jnp.concatenate([halo, h], axis=0).astype(BF16)

    def conv(u, cw_ref, cb_ref):
        cw = cw_ref[...]
        return (cw[0:1, :] * u[HALO - 2:HALO - 2 + tm, :] + cw[1:2, :] * u[HALO - 1:HALO - 1 + tm, :]
                + cw[2:3, :] * u[HALO:HALO + tm, :] + cb_ref[...])

    gate = conv(_nn(hb, wug_ref[...]), cwg_ref, cbg_ref)
    up = conv(_nn(hb, wuu_ref[...]), cwu_ref, cbu_ref)
    act = (gate * jax.nn.sigmoid(gate) * up).astype(BF16)
    part = _nn(act, wd_ref[...])

    @pl.when(j == 0)
    def _():
        acc_ref[...] = part

    @pl.when(j > 0)
    def _():
        acc_ref[...] += part

    @pl.when(j == pl.num_programs(1) - 1)
    def _():
        o_ref[...] = _layer_norm(alpha * h + acc_ref[...], lnw_ref[...], lnb_ref[...])


def _ffn(h1, wug, wuu, cwg, cwu, cbg, cbu, wd, lnw, lnb, tm, tf, seq, alpha):
    m, d = h1.shape
    dff = wug.shape[1]
    full = lambda a: pl.BlockSpec(a.shape, lambda i, j: (0,) * a.ndim)
    hb = tm // HALO
    return pl.pallas_call(
        functools.partial(_ffn_kernel, alpha=alpha, tiles_per_seq=seq // tm),
        grid=(m // tm, dff // tf),
        in_specs=[pl.BlockSpec((tm, d), lambda i, j: (i, 0)),
                  pl.BlockSpec((HALO, d), lambda i, j: (jnp.maximum(i * hb - 1, 0), 0)),
                  pl.BlockSpec((d, tf), lambda i, j: (0, j)),
                  pl.BlockSpec((d, tf), lambda i, j: (0, j)),
                  pl.BlockSpec((3, tf), lambda i, j: (0, j)),
                  pl.BlockSpec((3, tf), lambda i, j: (0, j)),
                  pl.BlockSpec((1, tf), lambda i, j: (0, j)),
                  pl.BlockSpec((1, tf), lambda i, j: (0, j)),
                  pl.BlockSpec((tf, d), lambda i, j: (j, 0)),
                  full(lnw), full(lnb)],
        out_specs=pl.BlockSpec((tm, d), lambda i, j: (i, 0)),
        out_shape=jax.ShapeDtypeStruct((m, d), F32),
        scratch_shapes=[pltpu.VMEM((tm, d), F32)],
        compiler_params=pltpu.CompilerParams(
            dimension_semantics=("parallel", "arbitrary"), vmem_limit_bytes=VMEM_LIMIT),
        name="conv_ffn_ln",
    )(h1, h1, wug, wuu, cwg, cwu, cbg, cbu, wd, lnw, lnb)


def _tile_sizes(seq):
    return dict(tm_proj=min(256, seq), tq=min(128, seq), chunk=64,
                tm_merge=min(256, seq), tm_ffn=min(512, seq))


def _layer(h, w_in, idx_w, idx_b, mu, w0, w_up, a0, a_up, g_up, k_k, k_a, r_k, gn_w, gn_b,
           w_ba, w_br, w_out, ln1_w, ln1_b, w_up_ffn, conv_w, conv_b, w_down, ln2_w, ln2_b, alpha):
    b, seq, d = h.shape
    ts = _tile_sizes(seq)
    aw = HEADS * HEAD_DIM
    kvw = KV_HEADS * HEAD_DIM
    o = np.cumsum([0, aw, kvw, kvw, aw, HEAD_DIM, HEADS, 3 * aw + LORA_W, d, d])
    col = lambda i: w_in[:, o[i]:o[i + 1]]
    dup = lambda w: jnp.concatenate([w[:, hh * HEAD_DIM:(hh + 1) * HEAD_DIM]
                                     for hh in range(KV_HEADS) for _ in range(2)], axis=1)
    wi_pad = jnp.pad(col(5), ((0, 0), (0, LANES - HEADS)))
    wa = jnp.concatenate([col(0), dup(col(1)), dup(col(2)), col(3), col(4), col(4), wi_pad],
                         axis=1).astype(BF16)
    wr = col(6).astype(BF16)
    wg = jnp.concatenate([col(7), col(8)], axis=1).astype(BF16)
    row2 = lambda v: v.reshape(1, -1)
    lnw2 = row2(jnp.concatenate([idx_w, idx_w]))
    lnb2 = row2(jnp.concatenate([idx_b, idx_b]))

    x2 = h.reshape(b * seq, d)
    q, k2, v2, qi, ki2, wi, pr, ga, gr = _inproj(x2, wa, wr, wg, lnw2, lnb2, ts["tm_proj"])
    r3 = lambda a: a.reshape(b, seq, a.shape[-1])

    y_attn = _attention(r3(q), r3(qi), r3(wi), r3(k2), r3(v2), r3(ki2), ts["tq"])

    ld, la = w_up.shape[0], a_up.shape[0]
    pad_rows = lambda w, start: jnp.pad(w, ((start, LORA_W - start - w.shape[0]), (0, 0)))
    vecs = [row2(v) for v in (mu, w0, a0, k_k, k_a, r_k.reshape(-1), gn_w, gn_b)]
    mats = [pad_rows(w_up, 0), pad_rows(a_up, ld), pad_rows(g_up, ld + la)]
    y_rwkv = _rwkv(r3(pr), vecs, mats, ts["chunk"])

    h1 = _merge(x2, y_attn.reshape(b * seq, aw), y_rwkv.reshape(b * seq, aw), ga, gr,
                w_ba.astype(BF16), w_br.astype(BF16), w_out.astype(BF16),
                row2(ln1_w), row2(ln1_b), ts["tm_merge"], alpha)

    dff = w_down.shape[0]
    tf = dff // 2 if (dff // 2) % LANES == 0 else dff
    out = _ffn(h1, w_up_ffn[:, :dff].astype(BF16), w_up_ffn[:, dff:].astype(BF16),
               conv_w[:, :dff], conv_w[:, dff:], row2(conv_b[:dff]), row2(conv_b[dff:]),
               w_down.astype(BF16), row2(ln2_w), row2(ln2_b), ts["tm_ffn"], tf, seq, alpha)
    return out.reshape(b, seq, d)


def kernel(x, w_in, idx_k_norm_w, idx_k_norm_b, rwkv_mu, rwkv_w0, rwkv_w_up, rwkv_a0, rwkv_a_up,
           rwkv_g_up, rwkv_k_k, rwkv_k_a, rwkv_r_k, rwkv_ln_w, rwkv_ln_b, w_branch_attn,
           w_branch_rwkv, w_out, ln1_w, ln1_b, w_up, conv_w, conv_b, w_down, ln2_w, ln2_b):
    depth = w_in.shape[0]
    alpha = (2.0 * depth) ** 0.25
    h = x
    for l in range(depth):
        h = _layer(h, w_in[l], idx_k_norm_w[l], idx_k_norm_b[l], rwkv_mu[l], rwkv_w0[l],
                   rwkv_w_up[l], rwkv_a0[l], rwkv_a_up[l], rwkv_g_up[l], rwkv_k_k[l], rwkv_k_a[l],
                   rwkv_r_k[l], rwkv_ln_w[l], rwkv_ln_b[l], w_branch_attn[l], w_branch_rwkv[l],
                   w_out[l], ln1_w[l], ln1_b[l], w_up[l], conv_w[l], conv_b[l], w_down[l],
                   ln2_w[l], ln2_b[l], alpha)
    return h
```

```python
import functools

import jax
import jax.numpy as jnp
import numpy as np
from jax import lax
from jax.experimental import pallas as pl
from jax.experimental.pallas import tpu as pltpu

F32 = jnp.float32
BF16 = jnp.bfloat16
I32 = jnp.int32

LANES = 128
HEAD_DIM = 64
HEADS = 8
KV_HEADS = 2
TOPK_MAX = 256
LORA_W = 128
LN_EPS = 1e-5
GN_EPS = 64e-5
INT_MIN = -(2 ** 31)
BISECT_ROWS = 64
BISECT_UNROLL = 4
VMEM_LIMIT = 48 * 1024 * 1024


def _nt(a, b):
    return lax.dot_general(a, b, (((1,), (1,)), ((), ())), preferred_element_type=F32)


def _nn(a, b):
    return lax.dot_general(a, b, (((1,), (0,)), ((), ())), preferred_element_type=F32)


def _bnt(a, b):
    return lax.dot_general(a, b, (((2,), (2,)), ((0,), (0,))), preferred_element_type=F32)


def _bnn(a, b):
    return lax.dot_general(a, b, (((2,), (1,)), ((0,), (0,))), preferred_element_type=F32)


def _split2(x):
    hi = x.astype(BF16)
    lo = (x - hi.astype(F32)).astype(BF16)
    return hi, lo


def _mm3(a, b, dot=_nn):
    ah, al = _split2(a)
    bh, bl = _split2(b)
    return dot(ah, bh) + (dot(ah, bl) + dot(al, bh))


def _mm_exact_lhs(a_bf16, b):
    b0 = b.astype(BF16)
    r1 = b - b0.astype(F32)
    b1 = r1.astype(BF16)
    b2 = (r1 - b1.astype(F32)).astype(BF16)
    return _nn(a_bf16, b0) + (_nn(a_bf16, b1) + _nn(a_bf16, b2))


def _inproj_kernel(x_ref, wa_ref, wr_ref, wg_ref, lnw_ref, lnb_ref,
                   q_ref, k2_ref, v2_ref, qi_ref, ki2_ref, wi_ref, pr_ref, ga_ref, gr_ref):
    xb = x_ref[...].astype(BF16)
    pa = _nn(xb, wa_ref[...])
    q_ref[...] = (pa[:, 0:512] * (HEAD_DIM ** -0.5)).astype(BF16)
    k2_ref[...] = pa[:, 512:768].astype(BF16)
    v2_ref[...] = pa[:, 768:1024].astype(BF16)
    qi_ref[...] = pa[:, 1024:1536].astype(BF16)
    ki = pa[:, 1536:1664]
    mu = jnp.mean(ki, axis=-1, keepdims=True)
    var = jnp.mean(jnp.square(ki - mu), axis=-1, keepdims=True)
    ki2_ref[...] = ((ki - mu) * lax.rsqrt(var + LN_EPS) * lnw_ref[...] + lnb_ref[...]).astype(BF16)
    wi_ref[...] = pa[:, 1664:1792]
    pr_ref[...] = _nn(xb, wr_ref[...])
    pg = _nn(xb, wg_ref[...])
    d = ga_ref.shape[-1]
    ga_ref[...] = pg[:, :d]
    gr_ref[...] = pg[:, d:]


def _inproj(x2, wa, wr, wg, lnw2, lnb2, tm):
    m, d = x2.shape
    full = lambda a: pl.BlockSpec(a.shape, lambda i: (0,) * a.ndim)
    row = lambda n: pl.BlockSpec((tm, n), lambda i: (i, 0))
    outs = [(512, BF16), (256, BF16), (256, BF16), (512, BF16), (128, BF16), (128, F32),
            (wr.shape[1], F32), (d, F32), (d, F32)]
    return pl.pallas_call(
        _inproj_kernel,
        grid=(m // tm,),
        in_specs=[row(d), full(wa), full(wr), full(wg), full(lnw2), full(lnb2)],
        out_specs=[row(n) for n, _ in outs],
        out_shape=[jax.ShapeDtypeStruct((m, n), dt) for n, dt in outs],
        compiler_params=pltpu.CompilerParams(
            dimension_semantics=("parallel",), vmem_limit_bytes=VMEM_LIMIT),
        name="inproj",
    )(x2, wa, wr, wg, lnw2, lnb2)


def _attn_body(q_ref, qi_ref, wi_ref, k2_ref, v2_ref, ki2_ref, o_ref, key_ref, bias_ref,
               *, t0, tq, width, n_sel, cw):
    lane = lax.broadcasted_iota(I32, (tq, LANES), 1)
    lo = lane < HEAD_DIM
    n_cc = width // cw
    n_lc = width // LANES

    def head_lhs(ref, h):
        pair = ref[0, :, (h // 2) * LANES:(h // 2 + 1) * LANES]
        return jnp.where(lo if h % 2 == 0 else jnp.logical_not(lo), pair, jnp.zeros_like(pair))

    wi = wi_ref[0] * (HEAD_DIM ** -0.5 * HEADS ** -0.5)
    row_c = t0 + lax.broadcasted_iota(I32, (tq, cw), 0)
    col_c = lax.broadcasted_iota(I32, (tq, cw), 1)
    for c in range(n_cc):
        kc = ki2_ref[0, c * cw:(c + 1) * cw, :]
        s = jnp.zeros((tq, cw), F32)
        for h in range(HEADS):
            s = s + jnp.maximum(_nt(head_lhs(qi_ref, h), kc), 0.0) * wi[:, h:h + 1]
        bits = lax.bitcast_convert_type(s, I32)
        key = jnp.where(bits < 0, bits ^ jnp.int32(0x7FFFFFFF), bits)
        key = jnp.where(s == 0.0, jnp.int32(0), key)
        key = jnp.where(col_c + c * cw <= row_c, key, jnp.int32(INT_MIN))
        key_ref[:, c * cw:(c + 1) * cw] = key

    kf = jnp.float32(n_sel)
    rb = BISECT_ROWS
    n_rb = tq // rb
    blocks = [slice(b * rb, (b + 1) * rb) for b in range(n_rb)]

    def count(pred_fn, rows, read=lambda ref, rows, sl: ref[rows, sl], ref=key_ref):
        acc = jnp.zeros((rb, LANES), F32)
        for c in range(n_lc):
            acc = acc + jnp.where(pred_fn(read(ref, rows, slice(c * LANES, (c + 1) * LANES))),
                                  1.0, 0.0)
        return jnp.broadcast_to(jnp.sum(acc, axis=1, keepdims=True), (rb, LANES))

    def bis_body(i, thrs):
        inc = lax.shift_left(jnp.int32(1), jnp.int32(31) - i)
        out = []
        for rows, t in zip(blocks, thrs):
            cand = t + inc
            out.append(jnp.where(count(lambda kc: kc >= cand, rows) >= kf, cand, t))
        return tuple(out)

    thrs = lax.fori_loop(0, 32, bis_body,
                         tuple(jnp.full((rb, LANES), INT_MIN, I32) for _ in blocks),
                         unroll=BISECT_UNROLL)
    thrs = [jnp.maximum(t, jnp.int32(INT_MIN + 1)) for t in thrs]

    excess = [count(lambda kc: kc >= t, rows) - kf for rows, t in zip(blocks, thrs)]
    has_ties = functools.reduce(jnp.maximum, [jnp.max(e) for e in excess]) > 0.0

    @pl.when(jnp.logical_not(has_ties))
    def _():
        for rows, t in zip(blocks, thrs):
            for c in range(n_lc):
                sl = slice(c * LANES, (c + 1) * LANES)
                bias_ref[rows, sl] = jnp.where(key_ref[rows, sl] >= t, 0.0, -jnp.inf)

    @pl.when(has_ties)
    def _():
        lane_rb = lax.broadcasted_iota(I32, (rb, LANES), 1)
        read_pos = lambda ref, rows, sl: lax.bitcast_convert_type(ref[rows, sl], I32)
        nbits = int(np.log2(width - 1)) + 1
        for rows, t in zip(blocks, thrs):
            need = kf - count(lambda kc: kc > t, rows)
            for c in range(n_lc):
                sl = slice(c * LANES, (c + 1) * LANES)
                tpos = jnp.where(key_ref[rows, sl] == t, lane_rb + c * LANES, jnp.int32(width))
                bias_ref[rows, sl] = lax.bitcast_convert_type(tpos, F32)

            def tie_body(i, j, rows=rows, need=need):
                cand = j + lax.shift_left(jnp.int32(1), jnp.int32(nbits - 1) - i)
                below = count(lambda pc: pc < cand, rows, read_pos, bias_ref)
                return jnp.where(below < need, cand, j)

            jmax = lax.fori_loop(0, nbits, tie_body, jnp.zeros((rb, LANES), I32))
            for c in range(n_lc):
                sl = slice(c * LANES, (c + 1) * LANES)
                keep = jnp.where(read_pos(bias_ref, rows, sl) <= jmax, 0.0, -jnp.inf)
                bias_ref[rows, sl] = jnp.where(key_ref[rows, sl] > t, 0.0, keep)

    row = t0 + lax.broadcasted_iota(I32, (tq, width), 0)
    col = lax.broadcasted_iota(I32, (tq, width), 1)
    dist = (row - col).astype(F32)
    bias = bias_ref[:, 0:width]
    for p in range(HEADS // 2):
        g = (2 * p) // (HEADS // KV_HEADS)
        k2g = k2_ref[0, 0:width, g * LANES:(g + 1) * LANES]
        v2g = v2_ref[0, 0:width, g * LANES:(g + 1) * LANES]
        outs = []
        for h in (2 * p, 2 * p + 1):
            logit = _nt(head_lhs(q_ref, h), k2g) - (2.0 ** -(h + 1)) * dist + bias
            m = jnp.max(logit, axis=1, keepdims=True)
            e = jnp.exp(logit - m)
            ssum = jnp.sum(e, axis=1, keepdims=True)
            outs.append(_nn(e.astype(BF16), v2g) / ssum)
        o_ref[0, :, p * LANES:(p + 1) * LANES] = jnp.where(lo, outs[0], outs[1]).astype(o_ref.dtype)


def _attn_kernel(q_ref, qi_ref, wi_ref, k2_ref, v2_ref, ki2_ref, o_ref, key_ref, bias_ref,
                 *, tq, seq, n_sel, n_classes):
    i = pl.program_id(1)
    tiles_per_class = (seq // tq) // n_classes
    for cls in range(n_classes):
        width = (cls + 1) * tiles_per_class * tq

        @pl.when((i >= cls * tiles_per_class) & (i < (cls + 1) * tiles_per_class))
        def _():
            _attn_body(q_ref, qi_ref, wi_ref, k2_ref, v2_ref, ki2_ref, o_ref, key_ref, bias_ref,
                       t0=i * tq, tq=tq, width=width, n_sel=n_sel, cw=min(512, width))


def _attention(q, qi, wi, k2, v2, ki2, tq):
    b, seq, _ = q.shape
    n_sel = min(TOPK_MAX, seq // 4)
    n_classes = min(4, seq // tq)
    qblk = lambda n: pl.BlockSpec((1, tq, n), lambda bi, i: (bi, i, 0))
    kblk = lambda n: pl.BlockSpec((1, seq, n), lambda bi, i: (bi, 0, 0))
    return pl.pallas_call(
        functools.partial(_attn_kernel, tq=tq, seq=seq, n_sel=n_sel, n_classes=n_classes),
        grid=(b, seq // tq),
        in_specs=[qblk(512), qblk(512), qblk(128), kblk(256), kblk(256), kblk(128)],
        out_specs=qblk(512),
        out_shape=jax.ShapeDtypeStruct((b, seq, 512), BF16),
        scratch_shapes=[pltpu.VMEM((tq, seq), I32), pltpu.VMEM((tq, seq), F32)],
        compiler_params=pltpu.CompilerParams(
            dimension_semantics=("parallel", "parallel"), vmem_limit_bytes=VMEM_LIMIT),
        name="dsa_attention",
    )(q, qi, wi, k2, v2, ki2)


def _bf(x):
    return x.astype(BF16)


def _rwkv_kernel(pr_ref, mu_ref, w0_ref, wup_ref, a0_ref, aup_ref, gup_ref, kk_ref, ka_ref,
                 rk_ref, lnw_ref, lnb_ref, o_ref,
                 state_ref, prev_ref, tinv_ref, w_ref, atrt_ref, bhm_ref, y_ref, *, c):
    tl = o_ref.shape[1]
    width = o_ref.shape[-1]
    n_pairs = width // LANES
    n_chunks = tl // c
    rows2 = 2 * c

    @pl.when(pl.program_id(1) == 0)
    def _():
        state_ref[...] = jnp.zeros_like(state_ref)
        prev_ref[...] = jnp.zeros_like(prev_ref)

    p = pr_ref[0]
    rows = lax.broadcasted_iota(I32, p.shape, 0)
    shifted = jnp.where(rows == 0, prev_ref[7:8, :], pltpu.roll(p, 1, 0))
    prev_ref[...] = p[tl - 8:tl, :]
    ps = p + (shifted - p) * mu_ref[...]

    r = ps[:, 0:width]
    k = ps[:, width:2 * width]
    v = ps[:, 2 * width:3 * width]
    lora = ps[:, 3 * width:3 * width + LORA_W]
    nz = -(w0_ref[...] + _nn(_bf(jnp.tanh(lora)), wup_ref[...]))
    log_w = -(jnp.maximum(nz, 0.0) + jnp.log(1.0 + jnp.exp(-jnp.abs(nz)))) - 0.5
    lw = -jnp.exp(log_w)
    a_sig = jax.nn.sigmoid(a0_ref[...] + _nn(_bf(lora), aup_ref[...]))
    gate = _nn(_bf(jax.nn.sigmoid(lora)), gup_ref[...])

    ti = lax.broadcasted_iota(I32, (tl, tl), 0)
    tj = lax.broadcasted_iota(I32, (tl, tl), 1)
    same_chunk = (ti // c) == (tj // c)
    tri = jnp.where(same_chunk & (ti >= tj), 1.0, 0.0).astype(BF16)
    blk = jnp.where(same_chunk, 1.0, 0.0).astype(BF16)
    lw_hi = _bf(lw)
    lw_lo = _bf(lw - lw_hi.astype(F32))
    cum = _nn(tri, lw_hi) + _nn(tri, lw_lo)
    tot = _nn(blk, lw_hi) + _nn(blk, lw_lo)

    li = lax.broadcasted_iota(I32, (LANES, LANES), 0)
    lj = lax.broadcasted_iota(I32, (LANES, LANES), 1)
    ones_bd = jnp.where((li // HEAD_DIM) == (lj // HEAD_DIM), 1.0, 0.0).astype(BF16)
    upper_strict = li < lj
    upper_incl = li <= lj
    eye = jnp.where(li == lj, 1.0, 0.0)
    lo = lax.broadcasted_iota(I32, (c, LANES), 1) < HEAD_DIM

    def head_sum(x):
        return _nn(_bf(x), ones_bd)

    def stack(x):
        return _bf(jnp.concatenate([jnp.where(lo, x, 0.0), jnp.where(lo, 0.0, x)], axis=0))

    n_pc = n_chunks * n_pairs
    atrt_l, btkt_l, vst_l, kh_l, bh_l = ([None] * n_pc for _ in range(5))
    for pi in range(n_pairs):
        sl = slice(pi * LANES, (pi + 1) * LANES)
        r_p, k_p, v_p, a_p = r[:, sl], k[:, sl], v[:, sl], a_sig[:, sl]
        kk = k_p * kk_ref[:, sl]
        kk = kk * lax.rsqrt(jnp.maximum(head_sum(kk * kk), 1e-24))
        kmod = k_p * (1.0 + (a_p - 1.0) * ka_ref[:, sl])
        bvec = kk * a_p
        cum_p, tot_p = cum[:, sl], tot[:, sl]
        p_inv = jnp.exp(-cum_p)
        p_end = jnp.exp(tot_p - cum_p)
        at_all = -kk * jnp.exp(cum_p - lw[:, sl])
        rt_all = r_p * jnp.exp(cum_p)
        bt_all, kt_all = bvec * p_inv, kmod * p_inv
        bh_all, kh_all = bvec * p_end, kmod * p_end
        for ci in range(n_chunks):
            rs = slice(ci * c, (ci + 1) * c)
            idx = ci * n_pairs + pi
            atrt_l[idx] = jnp.concatenate([stack(at_all[rs]), stack(rt_all[rs])], axis=0)
            btkt_l[idx] = jnp.concatenate([stack(bt_all[rs]), stack(kt_all[rs])], axis=0)
            vst_l[idx] = _bf(jnp.concatenate([jnp.where(lo, v_p[rs], 0.0),
                                              jnp.where(lo, 0.0, v_p[rs])], axis=0).T)
            kh_l[idx] = stack(kh_all[rs])
            bh_l[idx] = stack(bh_all[rs])

    atrt = jnp.stack(atrt_l)
    gt = _bnt(jnp.stack(btkt_l), atrt)
    n_t = jnp.where(upper_strict, gt[:, 0:rows2, 0:rows2], 0.0)
    m_rb_t = jnp.where(upper_incl, gt[:, 0:rows2, rows2:], 0.0)
    m_ak_t = jnp.where(upper_strict, gt[:, rows2:, 0:rows2], 0.0)
    m_rk_t = jnp.where(upper_incl, gt[:, rows2:, rows2:], 0.0)
    tinv = eye + n_t
    npow = _bf(n_t)
    for _ in range(int(np.log2(c)) - 1):
        npow = _bf(_bnn(npow, npow))
        tinv = tinv + _bnn(_bf(tinv), npow)
    rhs = jnp.concatenate([_bf(m_ak_t), jnp.stack(kh_l), _bf(m_rk_t)], axis=2)
    w_ref[...] = _bnn(jnp.stack(vst_l), rhs)
    tinv_ref[...] = _bf(tinv)
    atrt_ref[...] = atrt
    bhm_ref[...] = jnp.concatenate([jnp.stack(bh_l), _bf(m_rb_t)], axis=2)

    for ci in range(n_chunks):
        rs = slice(ci * c, (ci + 1) * c)
        ids = slice(ci * n_pairs, (ci + 1) * n_pairs)
        state = state_ref[...]
        x = _bnt(_bf(state), atrt_ref[ids])
        w = w_ref[ids]
        us_t = _bnn(_bf(x[:, :, 0:rows2] + w[:, :, 0:rows2]), tinv_ref[ids])
        z = _bnn(_bf(us_t), bhm_ref[ids])
        decay_c = jnp.exp(jnp.stack([tot[ci * c:ci * c + 1, pi * LANES:(pi + 1) * LANES]
                                     for pi in range(n_pairs)]))
        state_ref[...] = state * decay_c + z[:, :, 0:rows2] + w[:, :, rows2:2 * rows2]
        ys_t = x[:, :, rows2:] + z[:, :, rows2:] + w[:, :, 2 * rows2:]
        for pi in range(n_pairs):
            ys = ys_t[pi].T
            y_ref[rs, pi * LANES:(pi + 1) * LANES] = ys[0:c, :] + ys[c:rows2, :]

    for pi in range(n_pairs):
        sl = slice(pi * LANES, (pi + 1) * LANES)
        y = y_ref[:, sl]
        mean = head_sum(y) * (1.0 / HEAD_DIM)
        yc = y - mean
        var = head_sum(yc * yc) * (1.0 / HEAD_DIM)
        yn = yc * lax.rsqrt(var + GN_EPS) * lnw_ref[:, sl] + lnb_ref[:, sl]
        kmod = k[:, sl] * (1.0 + (a_sig[:, sl] - 1.0) * ka_ref[:, sl])
        bonus = head_sum(r[:, sl] * kmod * rk_ref[:, sl]) * v[:, sl]
        o_ref[0, :, sl] = ((yn + bonus) * gate[:, sl]).astype(o_ref.dtype)


def _rwkv(pr, vecs, mats, c, tl):
    b, seq, pw = pr.shape
    width = (pw - LORA_W) // 3
    n_pairs = width // LANES
    n_pc = (tl // c) * n_pairs
    full = lambda a: pl.BlockSpec(a.shape, lambda bi, i: (0,) * a.ndim)
    mu, w0, a0, kk, ka, rk, lnw, lnb = vecs
    wup, aup, gup = [m.astype(BF16) for m in mats]
    return pl.pallas_call(
        functools.partial(_rwkv_kernel, c=c),
        grid=(b, seq // tl),
        in_specs=[pl.BlockSpec((1, tl, pw), lambda bi, i: (bi, i, 0)),
                  full(mu), full(w0), full(wup), full(a0), full(aup), full(gup),
                  full(kk), full(ka), full(rk), full(lnw), full(lnb)],
        out_specs=pl.BlockSpec((1, tl, width), lambda bi, i: (bi, i, 0)),
        out_shape=jax.ShapeDtypeStruct((b, seq, width), BF16),
        scratch_shapes=[pltpu.VMEM((n_pairs, LANES, LANES), F32),
                        pltpu.VMEM((8, pw), F32),
                        pltpu.VMEM((n_pc, 2 * c, 2 * c), BF16),
                        pltpu.VMEM((n_pc, 2 * c, 6 * c), F32),
                        pltpu.VMEM((n_pc, 4 * c, LANES), BF16),
                        pltpu.VMEM((n_pc, 2 * c, 4 * c), BF16),
                        pltpu.VMEM((tl, width), F32)],
        compiler_params=pltpu.CompilerParams(
            dimension_semantics=("parallel", "arbitrary"), vmem_limit_bytes=VMEM_LIMIT),
        name="rwkv7",
    )(pr, mu, w0, wup, a0, aup, gup, kk, ka, rk, lnw, lnb)


def _layer_norm(z, w, b):
    mu = jnp.mean(z, axis=-1, keepdims=True)
    zc = z - mu
    var = jnp.mean(zc * zc, axis=-1, keepdims=True)
    return zc * lax.rsqrt(var + LN_EPS) * w + b


def _merge_kernel(x_ref, ya_ref, yr_ref, ga_ref, gr_ref, wa_ref, wb_ref, wo_ref, lnw_ref, lnb_ref,
                  o_ref, *, alpha):
    mix = (jax.nn.sigmoid(ga_ref[...]) * _nn(ya_ref[...], wa_ref[...])
           + jax.nn.sigmoid(gr_ref[...]) * _nn(yr_ref[...], wb_ref[...]))
    z = alpha * x_ref[...] + _nn(mix.astype(BF16), wo_ref[...])
    o_ref[...] = _layer_norm(z, lnw_ref[...], lnb_ref[...])


def _merge(x2, ya, yr, ga, gr, wa, wb, wo, lnw, lnb, tm, alpha):
    m, d = x2.shape
    full = lambda a: pl.BlockSpec(a.shape, lambda i: (0,) * a.ndim)
    row = lambda n: pl.BlockSpec((tm, n), lambda i: (i, 0))
    return pl.pallas_call(
        functools.partial(_merge_kernel, alpha=alpha),
        grid=(m // tm,),
        in_specs=[row(d), row(ya.shape[1]), row(yr.shape[1]), row(d), row(d),
                  full(wa), full(wb), full(wo), full(lnw), full(lnb)],
        out_specs=row(d),
        out_shape=jax.ShapeDtypeStruct((m, d), F32),
        compiler_params=pltpu.CompilerParams(
            dimension_semantics=("parallel",), vmem_limit_bytes=VMEM_LIMIT),
        name="merge_out_ln",
    )(x2, ya, yr, ga, gr, wa, wb, wo, lnw, lnb)


HALO = 8


def _ffn_kernel(h_ref, halo_ref, wug_ref, wuu_ref, cwg_ref, cwu_ref, cbg_ref, cbu_ref, wd_ref,
                lnw_ref, lnb_ref, o_ref, acc_ref, *, alpha, tiles_per_seq):
    i = pl.program_id(0)
    j = pl.program_id(1)
    tm = h_ref.shape[0]
    h = h_ref[...]
    first = (i % tiles_per_seq) == 0
    halo = jnp.where(first, 0.0, halo_ref[...])
    hb = jnp.concatenate([halo, h], axis=0).astype(BF16)

    def conv(u, cw_ref, cb_ref):
        cw = cw_ref[...]
        return (cw[0:1, :] * u[HALO - 2:HALO - 2 + tm, :] + cw[1:2, :] * u[HALO - 1:HALO - 1 + tm, :]
                + cw[2:3, :] * u[HALO:HALO + tm, :] + cb_ref[...])

    gate = conv(_nn(hb, wug_ref[...]), cwg_ref, cbg_ref)
    up = conv(_nn(hb, wuu_ref[...]), cwu_ref, cbu_ref)
    act = (gate * jax.nn.sigmoid(gate) * up).astype(BF16)
    part = _nn(act, wd_ref[...])

    @pl.when(j == 0)
    def _():
        acc_ref[...] = part

    @pl.when(j > 0)
    def _():
        acc_ref[...] += part

    @pl.when(j == pl.num_programs(1) - 1)
    def _():
        o_ref[...] = _layer_norm(alpha * h + acc_ref[...], lnw_ref[...], lnb_ref[...])


def _ffn(h1, wug, wuu, cwg, cwu, cbg, cbu, wd, lnw, lnb, tm, tf, seq, alpha):
    m, d = h1.shape
    dff = wug.shape[1]
    full = lambda a: pl.BlockSpec(a.shape, lambda i, j: (0,) * a.ndim)
    hb = tm // HALO
    return pl.pallas_call(
        functools.partial(_ffn_kernel, alpha=alpha, tiles_per_seq=seq // tm),
        grid=(m // tm, dff // tf),
        in_specs=[pl.BlockSpec((tm, d), lambda i, j: (i, 0)),
                  pl.BlockSpec((HALO, d), lambda i, j: (jnp.maximum(i * hb - 1, 0), 0)),
                  pl.BlockSpec((d, tf), lambda i, j: (0, j)),
                  pl.BlockSpec((d, tf), lambda i, j: (0, j)),
                  pl.BlockSpec((3, tf), lambda i, j: (0, j)),
                  pl.BlockSpec((3, tf), lambda i, j: (0, j)),
                  pl.BlockSpec((1, tf), lambda i, j: (0, j)),
                  pl.BlockSpec((1, tf), lambda i, j: (0, j)),
                  pl.BlockSpec((tf, d), lambda i, j: (j, 0)),
                  full(lnw), full(lnb)],
        out_specs=pl.BlockSpec((tm, d), lambda i, j: (i, 0)),
        out_shape=jax.ShapeDtypeStruct((m, d), F32),
        scratch_shapes=[pltpu.VMEM((tm, d), F32)],
        compiler_params=pltpu.CompilerParams(
            dimension_semantics=("parallel", "arbitrary"), vmem_limit_bytes=VMEM_LIMIT),
        name="conv_ffn_ln",
    )(h1, h1, wug, wuu, cwg, cwu, cbg, cbu, wd, lnw, lnb)


def _tile_sizes(seq):
    return dict(tm_proj=min(256, seq), tq=min(256, seq), chunk=64, tl_rwkv=min(256, seq),
                tm_merge=min(256, seq), tm_ffn=min(512, seq))


def _layer(h, w_in, idx_w, idx_b, mu, w0, w_up, a0, a_up, g_up, k_k, k_a, r_k, gn_w, gn_b,
           w_ba, w_br, w_out, ln1_w, ln1_b, w_up_ffn, conv_w, conv_b, w_down, ln2_w, ln2_b, alpha):
    b, seq, d = h.shape
    ts = _tile_sizes(seq)
    aw = HEADS * HEAD_DIM
    kvw = KV_HEADS * HEAD_DIM
    o = np.cumsum([0, aw, kvw, kvw, aw, HEAD_DIM, HEADS, 3 * aw + LORA_W, d, d])
    col = lambda i: w_in[:, o[i]:o[i + 1]]
    dup = lambda w: jnp.concatenate([w[:, hh * HEAD_DIM:(hh + 1) * HEAD_DIM]
                                     for hh in range(KV_HEADS) for _ in range(2)], axis=1)
    wi_pad = jnp.pad(col(5), ((0, 0), (0, LANES - HEADS)))
    wa = jnp.concatenate([col(0), dup(col(1)), dup(col(2)), col(3), col(4), col(4), wi_pad],
                         axis=1).astype(BF16)
    wr = col(6).astype(BF16)
    wg = jnp.concatenate([col(7), col(8)], axis=1).astype(BF16)
    row2 = lambda v: v.reshape(1, -1)
    lnw2 = row2(jnp.concatenate([idx_w, idx_w]))
    lnb2 = row2(jnp.concatenate([idx_b, idx_b]))

    x2 = h.reshape(b * seq, d)
    q, k2, v2, qi, ki2, wi, pr, ga, gr = _inproj(x2, wa, wr, wg, lnw2, lnb2, ts["tm_proj"])
    r3 = lambda a: a.reshape(b, seq, a.shape[-1])

    y_attn = _attention(r3(q), r3(qi), r3(wi), r3(k2), r3(v2), r3(ki2), ts["tq"])

    ld, la = w_up.shape[0], a_up.shape[0]
    pad_rows = lambda w, start: jnp.pad(w, ((start, LORA_W - start - w.shape[0]), (0, 0)))
    vecs = [row2(v) for v in (mu, w0, a0, k_k, k_a, r_k.reshape(-1), gn_w, gn_b)]
    mats = [pad_rows(w_up, 0), pad_rows(a_up, ld), pad_rows(g_up, ld + la)]
    y_rwkv = _rwkv(r3(pr), vecs, mats, ts["chunk"], ts["tl_rwkv"])

    h1 = _merge(x2, y_attn.reshape(b * seq, aw), y_rwkv.reshape(b * seq, aw), ga, gr,
                w_ba.astype(BF16), w_br.astype(BF16), w_out.astype(BF16),
                row2(ln1_w), row2(ln1_b), ts["tm_merge"], alpha)

    dff = w_down.shape[0]
    tf = dff // 2 if (dff // 2) % LANES == 0 else dff
    out = _ffn(h1, w_up_ffn[:, :dff].astype(BF16), w_up_ffn[:, dff:].astype(BF16),
               conv_w[:, :dff], conv_w[:, dff:], row2(conv_b[:dff]), row2(conv_b[dff:]),
               w_down.astype(BF16), row2(ln2_w), row2(ln2_b), ts["tm_ffn"], tf, seq, alpha)
    return out.reshape(b, seq, d)


def kernel(x, w_in, idx_k_norm_w, idx_k_norm_b, rwkv_mu, rwkv_w0, rwkv_w_up, rwkv_a0, rwkv_a_up,
           rwkv_g_up, rwkv_k_k, rwkv_k_a, rwkv_r_k, rwkv_ln_w, rwkv_ln_b, w_branch_attn,
           w_branch_rwkv, w_out, ln1_w, ln1_b, w_up, conv_w, conv_b, w_down, ln2_w, ln2_b):
    depth = w_in.shape[0]
    alpha = (2.0 * depth) ** 0.25
    h = x
    for l in range(depth):
        h = _layer(h, w_in[l], idx_k_norm_w[l], idx_k_norm_b[l], rwkv_mu[l], rwkv_w0[l],
                   rwkv_w_up[l], rwkv_a0[l], rwkv_a_up[l], rwkv_g_up[l], rwkv_k_k[l], rwkv_k_a[l],
                   rwkv_r_k[l], rwkv_ln_w[l], rwkv_ln_b[l], w_branch_attn[l], w_branch_rwkv[l],
                   w_out[l], ln1_w[l], ln1_b[l], w_up[l], conv_w[l], conv_b[l], w_down[l],
                   ln2_w[l], ln2_b[l], alpha)
    return h
```

```python
import functools

import jax
import jax.numpy as jnp
import ml_dtypes
import numpy as np
from jax import lax
from jax.experimental import pallas as pl
from jax.experimental.pallas import tpu as pltpu

F32 = jnp.float32
BF16 = jnp.bfloat16
I32 = jnp.int32

LANES = 128
HEAD_DIM = 64
HEADS = 8
KV_HEADS = 2
TOPK_MAX = 256
LORA_W = 128
LN_EPS = 1e-5
GN_EPS = 64e-5
INT_MIN = -(2 ** 31)
BISECT_ROWS = 64
BISECT_UNROLL = 4
VMEM_LIMIT = 48 * 1024 * 1024


def _nt(a, b):
    return lax.dot_general(a, b, (((1,), (1,)), ((), ())), preferred_element_type=F32)


def _nn(a, b):
    return lax.dot_general(a, b, (((1,), (0,)), ((), ())), preferred_element_type=F32)


def _bnt(a, b):
    return lax.dot_general(a, b, (((2,), (2,)), ((0,), (0,))), preferred_element_type=F32)


def _bnn(a, b):
    return lax.dot_general(a, b, (((2,), (1,)), ((0,), (0,))), preferred_element_type=F32)


def _split2(x):
    hi = x.astype(BF16)
    lo = (x - hi.astype(F32)).astype(BF16)
    return hi, lo


def _mm3(a, b, dot=_nn):
    ah, al = _split2(a)
    bh, bl = _split2(b)
    return dot(ah, bh) + (dot(ah, bl) + dot(al, bh))


def _mm_exact_lhs(a_bf16, b):
    b0 = b.astype(BF16)
    r1 = b - b0.astype(F32)
    b1 = r1.astype(BF16)
    b2 = (r1 - b1.astype(F32)).astype(BF16)
    return _nn(a_bf16, b0) + (_nn(a_bf16, b1) + _nn(a_bf16, b2))


LOG2E = 1.4426950408889634
POS_SPLIT = 16
SLOPE_TERMS = 3


def _inproj_kernel(x_ref, wa_ref, wr_ref, lnw_ref, lnb_ref,
                   q_ref, k4_ref, v2_ref, qi_ref, ki2_ref, wi_ref, pr_ref, *, seq):
    tm = x_ref.shape[0]
    xb = x_ref[...].astype(BF16)
    pa = _nn(xb, wa_ref[...])
    q_ref[...] = (pa[:, 0:512] * (HEAD_DIM ** -0.5 * LOG2E)).astype(BF16)
    pos = (pl.program_id(0) * tm) % seq + lax.broadcasted_iota(I32, (tm, 2 * LANES), 0)
    s_lo = pos & (POS_SPLIT - 1)
    ln = lax.broadcasted_iota(I32, (tm, 2 * LANES), 1)
    a = jnp.where(ln < LANES, ln - HEAD_DIM, ln - LANES)
    in_aug = (a >= 0) & (a < 2 * SLOPE_TERMS)
    aug = jnp.where(in_aug, jnp.where((a & 1) == 0, pos - s_lo, s_lo), 0).astype(F32)
    for g in range(KV_HEADS):
        sl = slice(512 + g * 2 * LANES, 512 + (g + 1) * 2 * LANES)
        k4_ref[:, g * 2 * LANES:(g + 1) * 2 * LANES] = (pa[:, sl] + aug).astype(BF16)
    v2_ref[...] = pa[:, 1024:1280].astype(BF16)
    qi_ref[...] = pa[:, 1280:1792].astype(BF16)
    ki = pa[:, 1792:1920]
    mu = jnp.mean(ki, axis=-1, keepdims=True)
    var = jnp.mean(jnp.square(ki - mu), axis=-1, keepdims=True)
    ki2_ref[...] = ((ki - mu) * lax.rsqrt(var + LN_EPS) * lnw_ref[...] + lnb_ref[...]).astype(BF16)
    wi_ref[...] = pa[:, 1920:2048]
    pr_ref[...] = _nn(xb, wr_ref[...])


def _inproj(x2, wa, wr, lnw2, lnb2, tm, seq):
    m, d = x2.shape
    assert seq <= 256 * POS_SPLIT
    full = lambda a: pl.BlockSpec(a.shape, lambda i: (0,) * a.ndim)
    row = lambda n: pl.BlockSpec((tm, n), lambda i: (i, 0))
    outs = [(512, BF16), (512, BF16), (256, BF16), (512, BF16), (128, BF16), (128, F32),
            (wr.shape[1], F32)]
    return pl.pallas_call(
        functools.partial(_inproj_kernel, seq=seq),
        grid=(m // tm,),
        in_specs=[row(d), full(wa), full(wr), full(lnw2), full(lnb2)],
        out_specs=[row(n) for n, _ in outs],
        out_shape=[jax.ShapeDtypeStruct((m, n), dt) for n, dt in outs],
        compiler_params=pltpu.CompilerParams(
            dimension_semantics=("parallel",), vmem_limit_bytes=VMEM_LIMIT),
        name="inproj",
    )(x2, wa, wr, lnw2, lnb2)


def _bf16_parts(x):
    parts = []
    for _ in range(SLOPE_TERMS):
        parts.append(float(ml_dtypes.bfloat16(x)))
        x -= parts[-1]
    return parts


def _attn_body(q_ref, qi_ref, wi_ref, k4_ref, v2_ref, ki2_ref, o_ref,
               key_ref, bias_ref, *, t0, tq, width, n_sel, cw):
    lane = lax.broadcasted_iota(I32, (tq, LANES), 1)
    lo = lane < HEAD_DIM
    n_cc = width // cw
    n_lc = width // LANES

    def head_lhs(ref, h, fill=None):
        pair = ref[0, :, (h // 2) * LANES:(h // 2 + 1) * LANES]
        fill = jnp.zeros_like(pair) if fill is None else fill
        return jnp.where(lo if h % 2 == 0 else jnp.logical_not(lo), pair, fill)

    wi = wi_ref[0] * (HEAD_DIM ** -0.5 * HEADS ** -0.5)
    row_c = t0 + lax.broadcasted_iota(I32, (tq, cw), 0)
    col_c = lax.broadcasted_iota(I32, (tq, cw), 1)
    for c in range(n_cc):
        kc = ki2_ref[0, c * cw:(c + 1) * cw, :]
        s = jnp.zeros((tq, cw), F32)
        for h in range(HEADS):
            s = s + jnp.maximum(_nt(head_lhs(qi_ref, h), kc), 0.0) * wi[:, h:h + 1]
        key_ref[:, c * cw:(c + 1) * cw] = jnp.where(col_c + c * cw <= row_c, s, -jnp.inf)

    kf = jnp.float32(n_sel)
    rb = BISECT_ROWS
    n_rb = tq // rb
    blocks = [slice(b * rb, (b + 1) * rb) for b in range(n_rb)]
    sls = [slice(c * LANES, (c + 1) * LANES) for c in range(n_lc)]
    key_lowest = jnp.int32(INT_MIN + 2 ** 23)
    key_highest = jnp.int32(0x7F800000)

    def as_float(key):
        key = jnp.clip(key, key_lowest, key_highest)
        bits = jnp.where(key < 0, key ^ jnp.int32(0x7FFFFFFF), key)
        return lax.bitcast_convert_type(bits, F32)

    def count(pred_fn, rows, read=lambda ref, rows, sl: ref[rows, sl], ref=key_ref):
        acc = jnp.zeros((rb, LANES), F32)
        for sl in sls:
            acc = acc + jnp.where(pred_fn(read(ref, rows, sl)), 1.0, 0.0)
        return jnp.broadcast_to(jnp.sum(acc, axis=1, keepdims=True), (rb, LANES))

    def bis_body(i, ts):
        inc = lax.shift_left(jnp.int32(1), jnp.int32(31) - i)
        out = []
        for rows, t in zip(blocks, ts):
            cand = t + inc
            cand_f = as_float(cand)
            out.append(jnp.where(count(lambda sc: sc >= cand_f, rows) >= kf, cand, t))
        return tuple(out)

    ts = lax.fori_loop(0, 32, bis_body, tuple(jnp.full((rb, LANES), INT_MIN, I32) for _ in blocks),
                       unroll=BISECT_UNROLL)
    thrs = [as_float(t) for t in ts]

    excess = [count(lambda sc: sc >= t, rows) - kf for rows, t in zip(blocks, thrs)]
    has_ties = functools.reduce(jnp.maximum, [jnp.max(e) for e in excess]) > 0.0

    @pl.when(jnp.logical_not(has_ties))
    def _():
        for rows, t in zip(blocks, thrs):
            for sl in sls:
                bias_ref[rows, sl] = jnp.where(key_ref[rows, sl] >= t, 0.0, -jnp.inf)

    @pl.when(has_ties)
    def _():
        lane_rb = lax.broadcasted_iota(I32, (rb, LANES), 1)
        read_pos = lambda ref, rows, sl: lax.bitcast_convert_type(ref[rows, sl], I32)
        nbits = int(np.log2(width - 1)) + 1
        for rows, t in zip(blocks, thrs):
            need = kf - count(lambda kc: kc > t, rows)
            for c, sl in enumerate(sls):
                tpos = jnp.where(key_ref[rows, sl] == t, lane_rb + c * LANES, jnp.int32(width))
                bias_ref[rows, sl] = lax.bitcast_convert_type(tpos, F32)

            def tie_body(i, j, rows=rows, need=need):
                cand = j + lax.shift_left(jnp.int32(1), jnp.int32(nbits - 1) - i)
                below = count(lambda pc: pc < cand, rows, read_pos, bias_ref)
                return jnp.where(below < need, cand, j)

            jmax = lax.fori_loop(0, nbits, tie_body, jnp.zeros((rb, LANES), I32))
            for sl in sls:
                keep = jnp.where(read_pos(bias_ref, rows, sl) <= jmax, 0.0, -jnp.inf)
                bias_ref[rows, sl] = jnp.where(key_ref[rows, sl] > t, 0.0, keep)

    bias = bias_ref[:, 0:width]
    a_idx = jnp.where(lo, lane, lane - HEAD_DIM)
    even = None
    for h in range(HEADS):
        g = h // (HEADS // KV_HEADS)
        parts = _bf16_parts((2.0 ** -(h + 1)) * LOG2E)
        fill = jnp.zeros((tq, LANES), F32)
        for n in reversed(range(SLOPE_TERMS)):
            fill = jnp.where(a_idx < 2 * (n + 1), parts[n], fill)
        fill = fill.astype(BF16)
        kcol = (2 * g + h % 2) * LANES
        logit = _nt(head_lhs(q_ref, h, fill), k4_ref[0, 0:width, kcol:kcol + LANES]) + bias
        m = jnp.max(logit, axis=1, keepdims=True)
        e = jnp.exp2(logit - m)
        ssum = jnp.sum(e, axis=1, keepdims=True)
        out = _nn(e.astype(BF16), v2_ref[0, 0:width, g * LANES:(g + 1) * LANES]) / ssum
        if h % 2 == 0:
            even = out
        else:
            p = h // 2
            o_ref[0, :, p * LANES:(p + 1) * LANES] = jnp.where(lo, even, out).astype(o_ref.dtype)


def _attn_kernel(q_ref, qi_ref, wi_ref, k4_ref, v2_ref, ki2_ref, o_ref,
                 key_ref, bias_ref, *, tq, seq, n_sel, n_classes):
    i = pl.program_id(1)
    tiles_per_class = (seq // tq) // n_classes
    for cls in range(n_classes):
        width = (cls + 1) * tiles_per_class * tq

        @pl.when((i >= cls * tiles_per_class) & (i < (cls + 1) * tiles_per_class))
        def _():
            _attn_body(q_ref, qi_ref, wi_ref, k4_ref, v2_ref, ki2_ref, o_ref, key_ref, bias_ref,
                       t0=i * tq, tq=tq, width=width, n_sel=n_sel, cw=min(512, width))


def _attention(q, qi, wi, k4, v2, ki2, tq):
    b, seq, _ = q.shape
    n_sel = min(TOPK_MAX, seq // 4)
    n_classes = min(4, seq // tq)
    qblk = lambda n: pl.BlockSpec((1, tq, n), lambda bi, i: (bi, i, 0))
    kblk = lambda n: pl.BlockSpec((1, seq, n), lambda bi, i: (bi, 0, 0))
    return pl.pallas_call(
        functools.partial(_attn_kernel, tq=tq, seq=seq, n_sel=n_sel, n_classes=n_classes),
        grid=(b, seq // tq),
        in_specs=[qblk(512), qblk(512), qblk(128), kblk(512), kblk(256), kblk(128)],
        out_specs=qblk(512),
        out_shape=jax.ShapeDtypeStruct((b, seq, 512), BF16),
        scratch_shapes=[pltpu.VMEM((tq, seq), F32), pltpu.VMEM((tq, seq), F32)],
        compiler_params=pltpu.CompilerParams(
            dimension_semantics=("parallel", "parallel"), vmem_limit_bytes=VMEM_LIMIT),
        name="dsa_attention",
    )(q, qi, wi, k4, v2, ki2)


def _bf(x):
    return x.astype(BF16)


def _rwkv_kernel(pr_ref, mu_ref, w0_ref, wup_ref, a0_ref, aup_ref, gup_ref, kk_ref, ka_ref,
                 rk_ref, lnw_ref, lnb_ref, o_ref,
                 state_ref, prev_ref, tinv_ref, w_ref, atrt_ref, bhm_ref, y_ref, *, c):
    tl = o_ref.shape[1]
    width = o_ref.shape[-1]
    n_pairs = width // LANES
    n_chunks = tl // c
    rows2 = 2 * c

    @pl.when(pl.program_id(1) == 0)
    def _():
        state_ref[...] = jnp.zeros_like(state_ref)
        prev_ref[...] = jnp.zeros_like(prev_ref)

    p = pr_ref[0]
    rows = lax.broadcasted_iota(I32, p.shape, 0)
    shifted = jnp.where(rows == 0, prev_ref[7:8, :], pltpu.roll(p, 1, 0))
    prev_ref[...] = p[tl - 8:tl, :]
    ps = p + (shifted - p) * mu_ref[...]

    r = ps[:, 0:width]
    k = ps[:, width:2 * width]
    v = ps[:, 2 * width:3 * width]
    lora = ps[:, 3 * width:3 * width + LORA_W]
    nz = -(w0_ref[...] + _nn(_bf(jnp.tanh(lora)), wup_ref[...]))
    log_w = -(jnp.maximum(nz, 0.0) + jnp.log(1.0 + jnp.exp(-jnp.abs(nz)))) - 0.5
    lw = -jnp.exp(log_w)
    a_sig = jax.nn.sigmoid(a0_ref[...] + _nn(_bf(lora), aup_ref[...]))
    gate = _nn(_bf(jax.nn.sigmoid(lora)), gup_ref[...])

    ti = lax.broadcasted_iota(I32, (tl, tl), 0)
    tj = lax.broadcasted_iota(I32, (tl, tl), 1)
    same_chunk = (ti // c) == (tj // c)
    tri = jnp.where(same_chunk & (ti >= tj), 1.0, 0.0).astype(BF16)
    blk = jnp.where(same_chunk, 1.0, 0.0).astype(BF16)
    lw_hi = _bf(lw)
    lw_lo = _bf(lw - lw_hi.astype(F32))
    cum = _nn(tri, lw_hi) + _nn(tri, lw_lo)
    tot = _nn(blk, lw_hi) + _nn(blk, lw_lo)

    li = lax.broadcasted_iota(I32, (LANES, LANES), 0)
    lj = lax.broadcasted_iota(I32, (LANES, LANES), 1)
    ones_bd = jnp.where((li // HEAD_DIM) == (lj // HEAD_DIM), 1.0, 0.0).astype(BF16)
    upper_strict = li < lj
    upper_incl = li <= lj
    eye = jnp.where(li == lj, 1.0, 0.0)
    lo = lax.broadcasted_iota(I32, (c, LANES), 1) < HEAD_DIM

    def head_sum(x):
        return _nn(_bf(x), ones_bd)

    def stack(x):
        return _bf(jnp.concatenate([jnp.where(lo, x, 0.0), jnp.where(lo, 0.0, x)], axis=0))

    n_pc = n_chunks * n_pairs
    atrt_l, btkt_l, vst_l, kh_l, bh_l = ([None] * n_pc for _ in range(5))
    for pi in range(n_pairs):
        sl = slice(pi * LANES, (pi + 1) * LANES)
        r_p, k_p, v_p, a_p = r[:, sl], k[:, sl], v[:, sl], a_sig[:, sl]
        kk = k_p * kk_ref[:, sl]
        kk = kk * lax.rsqrt(jnp.maximum(head_sum(kk * kk), 1e-24))
        kmod = k_p * (1.0 + (a_p - 1.0) * ka_ref[:, sl])
        bvec = kk * a_p
        cum_p, tot_p = cum[:, sl], tot[:, sl]
        p_inv = jnp.exp(-cum_p)
        p_end = jnp.exp(tot_p - cum_p)
        at_all = -kk * jnp.exp(cum_p - lw[:, sl])
        rt_all = r_p * jnp.exp(cum_p)
        bt_all, kt_all = bvec * p_inv, kmod * p_inv
        bh_all, kh_all = bvec * p_end, kmod * p_end
        for ci in range(n_chunks):
            rs = slice(ci * c, (ci + 1) * c)
            idx = ci * n_pairs + pi
            atrt_l[idx] = jnp.concatenate([stack(at_all[rs]), stack(rt_all[rs])], axis=0)
            btkt_l[idx] = jnp.concatenate([stack(bt_all[rs]), stack(kt_all[rs])], axis=0)
            vst_l[idx] = _bf(jnp.concatenate([jnp.where(lo, v_p[rs], 0.0),
                                              jnp.where(lo, 0.0, v_p[rs])], axis=0).T)
            kh_l[idx] = stack(kh_all[rs])
            bh_l[idx] = stack(bh_all[rs])

    atrt = jnp.stack(atrt_l)
    gt = _bnt(jnp.stack(btkt_l), atrt)
    n_t = jnp.where(upper_strict, gt[:, 0:rows2, 0:rows2], 0.0)
    m_rb_t = jnp.where(upper_incl, gt[:, 0:rows2, rows2:], 0.0)
    m_ak_t = jnp.where(upper_strict, gt[:, rows2:, 0:rows2], 0.0)
    m_rk_t = jnp.where(upper_incl, gt[:, rows2:, rows2:], 0.0)
    tinv = eye + n_t
    npow = _bf(n_t)
    for _ in range(int(np.log2(c)) - 1):
        npow = _bf(_bnn(npow, npow))
        tinv = tinv + _bnn(_bf(tinv), npow)
    rhs = jnp.concatenate([_bf(m_ak_t), jnp.stack(kh_l), _bf(m_rk_t)], axis=2)
    w_ref[...] = _bnn(jnp.stack(vst_l), rhs)
    tinv_ref[...] = _bf(tinv)
    atrt_ref[...] = atrt
    bhm_ref[...] = jnp.concatenate([jnp.stack(bh_l), _bf(m_rb_t)], axis=2)

    for ci in range(n_chunks):
        rs = slice(ci * c, (ci + 1) * c)
        ids = slice(ci * n_pairs, (ci + 1) * n_pairs)
        state = state_ref[...]
        x = _bnt(_bf(state), atrt_ref[ids])
        w = w_ref[ids]
        us_t = _bnn(_bf(x[:, :, 0:rows2] + w[:, :, 0:rows2]), tinv_ref[ids])
        z = _bnn(_bf(us_t), bhm_ref[ids])
        decay_c = jnp.exp(jnp.stack([tot[ci * c:ci * c + 1, pi * LANES:(pi + 1) * LANES]
                                     for pi in range(n_pairs)]))
        state_ref[...] = state * decay_c + z[:, :, 0:rows2] + w[:, :, rows2:2 * rows2]
        ys_t = x[:, :, rows2:] + z[:, :, rows2:] + w[:, :, 2 * rows2:]
        for pi in range(n_pairs):
            ys = ys_t[pi].T
            y_ref[rs, pi * LANES:(pi + 1) * LANES] = ys[0:c, :] + ys[c:rows2, :]

    for pi in range(n_pairs):
        sl = slice(pi * LANES, (pi + 1) * LANES)
        y = y_ref[:, sl]
        mean = head_sum(y) * (1.0 / HEAD_DIM)
        yc = y - mean
        var = head_sum(yc * yc) * (1.0 / HEAD_DIM)
        yn = yc * lax.rsqrt(var + GN_EPS) * lnw_ref[:, sl] + lnb_ref[:, sl]
        kmod = k[:, sl] * (1.0 + (a_sig[:, sl] - 1.0) * ka_ref[:, sl])
        bonus = head_sum(r[:, sl] * kmod * rk_ref[:, sl]) * v[:, sl]
        o_ref[0, :, sl] = ((yn + bonus) * gate[:, sl]).astype(o_ref.dtype)


def _rwkv(pr, vecs, mats, c, tl):
    b, seq, pw = pr.shape
    width = (pw - LORA_W) // 3
    n_pairs = width // LANES
    n_pc = (tl // c) * n_pairs
    full = lambda a: pl.BlockSpec(a.shape, lambda bi, i: (0,) * a.ndim)
    mu, w0, a0, kk, ka, rk, lnw, lnb = vecs
    wup, aup, gup = [m.astype(BF16) for m in mats]
    return pl.pallas_call(
        functools.partial(_rwkv_kernel, c=c),
        grid=(b, seq // tl),
        in_specs=[pl.BlockSpec((1, tl, pw), lambda bi, i: (bi, i, 0)),
                  full(mu), full(w0), full(wup), full(a0), full(aup), full(gup),
                  full(kk), full(ka), full(rk), full(lnw), full(lnb)],
        out_specs=pl.BlockSpec((1, tl, width), lambda bi, i: (bi, i, 0)),
        out_shape=jax.ShapeDtypeStruct((b, seq, width), BF16),
        scratch_shapes=[pltpu.VMEM((n_pairs, LANES, LANES), F32),
                        pltpu.VMEM((8, pw), F32),
                        pltpu.VMEM((n_pc, 2 * c, 2 * c), BF16),
                        pltpu.VMEM((n_pc, 2 * c, 6 * c), F32),
                        pltpu.VMEM((n_pc, 4 * c, LANES), BF16),
                        pltpu.VMEM((n_pc, 2 * c, 4 * c), BF16),
                        pltpu.VMEM((tl, width), F32)],
        compiler_params=pltpu.CompilerParams(
            dimension_semantics=("parallel", "arbitrary"), vmem_limit_bytes=VMEM_LIMIT),
        name="rwkv7",
    )(pr, mu, w0, wup, a0, aup, gup, kk, ka, rk, lnw, lnb)


def _layer_norm(z, w, b):
    mu = jnp.mean(z, axis=-1, keepdims=True)
    zc = z - mu
    var = jnp.mean(zc * zc, axis=-1, keepdims=True)
    return zc * lax.rsqrt(var + LN_EPS) * w + b


def _merge_kernel(x_ref, ya_ref, yr_ref, wga_ref, wgr_ref, wa_ref, wb_ref, wo_ref, lnw_ref, lnb_ref,
                  o_ref, *, alpha):
    x = x_ref[...]
    xb = x.astype(BF16)
    mix = (jax.nn.sigmoid(_nn(xb, wga_ref[...])) * _nn(ya_ref[...], wa_ref[...])
           + jax.nn.sigmoid(_nn(xb, wgr_ref[...])) * _nn(yr_ref[...], wb_ref[...]))
    z = alpha * x + _nn(mix.astype(BF16), wo_ref[...])
    o_ref[...] = _layer_norm(z, lnw_ref[...], lnb_ref[...])


def _merge(x2, ya, yr, wga, wgr, wa, wb, wo, lnw, lnb, tm, alpha):
    m, d = x2.shape
    full = lambda a: pl.BlockSpec(a.shape, lambda i: (0,) * a.ndim)
    row = lambda n: pl.BlockSpec((tm, n), lambda i: (i, 0))
    return pl.pallas_call(
        functools.partial(_merge_kernel, alpha=alpha),
        grid=(m // tm,),
        in_specs=[row(d), row(ya.shape[1]), row(yr.shape[1]),
                  full(wga), full(wgr), full(wa), full(wb), full(wo), full(lnw), full(lnb)],
        out_specs=row(d),
        out_shape=jax.ShapeDtypeStruct((m, d), F32),
        compiler_params=pltpu.CompilerParams(
            dimension_semantics=("parallel",), vmem_limit_bytes=VMEM_LIMIT),
        name="merge_out_ln",
    )(x2, ya, yr, wga, wgr, wa, wb, wo, lnw, lnb)


SUBLANES = 8
FFN_COLS = 256
FFN_VMEM_LIMIT = 56 * 1024 * 1024


def _ffn_kernel(h_ref, wug_ref, wuu_ref, cwg_ref, cwu_ref, cbg_ref, cbu_ref, wd_ref,
                lnw_ref, lnb_ref, o_ref, pg_ref, pu_ref, act_ref, *, alpha, tiles_per_seq):
    tm = h_ref.shape[0]
    dff = wd_ref.shape[0]
    h = h_ref[...]
    hb = h.astype(BF16)

    @pl.when(pl.program_id(0) % tiles_per_seq == 0)
    def _():
        pg_ref[...] = jnp.zeros_like(pg_ref)
        pu_ref[...] = jnp.zeros_like(pu_ref)

    def conv(u, prev_ref, cw_ref, cb_ref, cols):
        ext = jnp.concatenate([prev_ref[:, cols], u], axis=0)
        prev_ref[:, cols] = u[tm - SUBLANES:, :]
        w = cw_ref[:, cols]
        back1 = pltpu.roll(ext, 1, 0)[SUBLANES:]
        back2 = pltpu.roll(ext, 2, 0)[SUBLANES:]
        return w[0:1] * back2 + w[1:2] * back1 + w[2:3] * u + cb_ref[:, cols]

    for c in range(dff // FFN_COLS):
        cols = slice(c * FFN_COLS, (c + 1) * FFN_COLS)
        gate = conv(_nn(hb, wug_ref[:, cols]), pg_ref, cwg_ref, cbg_ref, cols)
        up = conv(_nn(hb, wuu_ref[:, cols]), pu_ref, cwu_ref, cbu_ref, cols)
        act_ref[:, cols] = (gate * jax.nn.sigmoid(gate) * up).astype(BF16)
    o_ref[...] = _layer_norm(alpha * h + _nn(act_ref[...], wd_ref[...]), lnw_ref[...], lnb_ref[...])


def _ffn(h1, wug, wuu, cwg, cwu, cbg, cbu, wd, lnw, lnb, tm, seq, alpha):
    m, d = h1.shape
    dff = wd.shape[0]
    assert dff % FFN_COLS == 0 and seq % tm == 0
    const = lambda a: pl.BlockSpec(a.shape, lambda i: (0,) * a.ndim, pipeline_mode=pl.Buffered(1))
    return pl.pallas_call(
        functools.partial(_ffn_kernel, alpha=alpha, tiles_per_seq=seq // tm),
        grid=(m // tm,),
        in_specs=[pl.BlockSpec((tm, d), lambda i: (i, 0))]
        + [const(a) for a in (wug, wuu, cwg, cwu, cbg, cbu, wd, lnw, lnb)],
        out_specs=pl.BlockSpec((tm, d), lambda i: (i, 0)),
        out_shape=jax.ShapeDtypeStruct((m, d), F32),
        scratch_shapes=[pltpu.VMEM((SUBLANES, dff), F32), pltpu.VMEM((SUBLANES, dff), F32),
                        pltpu.VMEM((tm, dff), BF16)],
        compiler_params=pltpu.CompilerParams(
            dimension_semantics=("arbitrary",), vmem_limit_bytes=FFN_VMEM_LIMIT),
        name="conv_ffn_ln",
    )(h1, wug, wuu, cwg, cwu, cbg, cbu, wd, lnw, lnb)


def _tile_sizes(seq):
    return dict(tm_proj=min(256, seq), tq=min(256, seq), chunk=64, tl_rwkv=min(256, seq),
                tm_merge=min(256, seq), tm_ffn=min(512, seq))


def _layer(h, w_in, idx_w, idx_b, mu, w0, w_up, a0, a_up, g_up, k_k, k_a, r_k, gn_w, gn_b,
           w_ba, w_br, w_out, ln1_w, ln1_b, w_up_ffn, conv_w, conv_b, w_down, ln2_w, ln2_b, alpha):
    b, seq, d = h.shape
    ts = _tile_sizes(seq)
    aw = HEADS * HEAD_DIM
    kvw = KV_HEADS * HEAD_DIM
    o = np.cumsum([0, aw, kvw, kvw, aw, HEAD_DIM, HEADS, 3 * aw + LORA_W, d, d])
    col = lambda i: w_in[:, o[i]:o[i + 1]]
    head = lambda w, hh: w[:, hh * HEAD_DIM:(hh + 1) * HEAD_DIM]
    zero = jnp.zeros((d, HEAD_DIM), w_in.dtype)
    k_cols = [blk for hh in range(KV_HEADS)
              for blk in (head(col(1), hh), zero, zero, head(col(1), hh))]
    v_cols = [head(col(2), hh) for hh in range(KV_HEADS) for _ in range(2)]
    wi_pad = jnp.pad(col(5), ((0, 0), (0, LANES - HEADS)))
    wa = jnp.concatenate([col(0)] + k_cols + v_cols + [col(3), col(4), col(4), wi_pad],
                         axis=1).astype(BF16)
    wr = col(6).astype(BF16)
    row2 = lambda v: v.reshape(1, -1)
    lnw2 = row2(jnp.concatenate([idx_w, idx_w]))
    lnb2 = row2(jnp.concatenate([idx_b, idx_b]))

    x2 = h.reshape(b * seq, d)
    q, k4, v2, qi, ki2, wi, pr = _inproj(x2, wa, wr, lnw2, lnb2, ts["tm_proj"], seq)
    r3 = lambda a: a.reshape(b, seq, a.shape[-1])

    y_attn = _attention(r3(q), r3(qi), r3(wi), r3(k4), r3(v2), r3(ki2), ts["tq"])

    ld, la = w_up.shape[0], a_up.shape[0]
    pad_rows = lambda w, start: jnp.pad(w, ((start, LORA_W - start - w.shape[0]), (0, 0)))
    vecs = [row2(v) for v in (mu, w0, a0, k_k, k_a, r_k.reshape(-1), gn_w, gn_b)]
    mats = [pad_rows(w_up, 0), pad_rows(a_up, ld), pad_rows(g_up, ld + la)]
    y_rwkv = _rwkv(r3(pr), vecs, mats, ts["chunk"], ts["tl_rwkv"])

    h1 = _merge(x2, y_attn.reshape(b * seq, aw), y_rwkv.reshape(b * seq, aw),
                col(7).astype(BF16), col(8).astype(BF16),
                w_ba.astype(BF16), w_br.astype(BF16), w_out.astype(BF16),
                row2(ln1_w), row2(ln1_b), ts["tm_merge"], alpha)

    dff = w_down.shape[0]
    out = _ffn(h1, w_up_ffn[:, :dff].astype(BF16), w_up_ffn[:, dff:].astype(BF16),
               conv_w[:, :dff], conv_w[:, dff:], row2(conv_b[:dff]), row2(conv_b[dff:]),
               w_down.astype(BF16), row2(ln2_w), row2(ln2_b), ts["tm_ffn"], seq, alpha)
    return out.reshape(b, seq, d)


def kernel(x, w_in, idx_k_norm_w, idx_k_norm_b, rwkv_mu, rwkv_w0, rwkv_w_up, rwkv_a0, rwkv_a_up,
           rwkv_g_up, rwkv_k_k, rwkv_k_a, rwkv_r_k, rwkv_ln_w, rwkv_ln_b, w_branch_attn,
           w_branch_rwkv, w_out, ln1_w, ln1_b, w_up, conv_w, conv_b, w_down, ln2_w, ln2_b):
    depth = w_in.shape[0]
    alpha = (2.0 * depth) ** 0.25
    h = x
    for l in range(depth):
        h = _layer(h, w_in[l], idx_k_norm_w[l], idx_k_norm_b[l], rwkv_mu[l], rwkv_w0[l],
                   rwkv_w_up[l], rwkv_a0[l], rwkv_a_up[l], rwkv_g_up[l], rwkv_k_k[l], rwkv_k_a[l],
                   rwkv_r_k[l], rwkv_ln_w[l], rwkv_ln_b[l], w_branch_attn[l], w_branch_rwkv[l],
                   w_out[l], ln1_w[l], ln1_b[l], w_up[l], conv_w[l], conv_b[l], w_down[l],
                   ln2_w[l], ln2_b[l], alpha)
    return h
```

```python
import functools

import jax
import jax.numpy as jnp
import ml_dtypes
import numpy as np
from jax import lax
from jax.experimental import pallas as pl
from jax.experimental.pallas import tpu as pltpu

F32 = jnp.float32
BF16 = jnp.bfloat16
I32 = jnp.int32

LANES = 128
HEAD_DIM = 64
HEADS = 8
KV_HEADS = 2
TOPK_MAX = 256
LORA_W = 128
LN_EPS = 1e-5
GN_EPS = 64e-5
INT_MIN = -(2 ** 31)
ATTN_CLASSES = 8
BISECT_ROWS = 64
BISECT_UNROLL = 4
VMEM_LIMIT = 48 * 1024 * 1024


def _nt(a, b):
    return lax.dot_general(a, b, (((1,), (1,)), ((), ())), preferred_element_type=F32)


def _nn(a, b):
    return lax.dot_general(a, b, (((1,), (0,)), ((), ())), preferred_element_type=F32)


def _bnt(a, b):
    return lax.dot_general(a, b, (((2,), (2,)), ((0,), (0,))), preferred_element_type=F32)


def _bnn(a, b):
    return lax.dot_general(a, b, (((2,), (1,)), ((0,), (0,))), preferred_element_type=F32)


def _split2(x):
    hi = x.astype(BF16)
    lo = (x - hi.astype(F32)).astype(BF16)
    return hi, lo


def _mm3(a, b, dot=_nn):
    ah, al = _split2(a)
    bh, bl = _split2(b)
    return dot(ah, bh) + (dot(ah, bl) + dot(al, bh))


def _mm_exact_lhs(a_bf16, b):
    b0 = b.astype(BF16)
    r1 = b - b0.astype(F32)
    b1 = r1.astype(BF16)
    b2 = (r1 - b1.astype(F32)).astype(BF16)
    return _nn(a_bf16, b0) + (_nn(a_bf16, b1) + _nn(a_bf16, b2))


LOG2E = 1.4426950408889634
POS_SPLIT = 16
SLOPE_TERMS = 3


def _inproj_kernel(x_ref, wa_ref, wr_ref, lnw_ref, lnb_ref,
                   q_ref, k4_ref, v2_ref, qi_ref, ki2_ref, wi_ref, pr_ref, *, seq):
    tm = x_ref.shape[0]
    xb = x_ref[...].astype(BF16)
    pa = _nn(xb, wa_ref[...])
    q_ref[...] = (pa[:, 0:512] * (HEAD_DIM ** -0.5 * LOG2E)).astype(BF16)
    pos = (pl.program_id(0) * tm) % seq + lax.broadcasted_iota(I32, (tm, 2 * LANES), 0)
    s_lo = pos & (POS_SPLIT - 1)
    ln = lax.broadcasted_iota(I32, (tm, 2 * LANES), 1)
    a = jnp.where(ln < LANES, ln - HEAD_DIM, ln - LANES)
    in_aug = (a >= 0) & (a < 2 * SLOPE_TERMS)
    aug = jnp.where(in_aug, jnp.where((a & 1) == 0, pos - s_lo, s_lo), 0).astype(F32)
    for g in range(KV_HEADS):
        sl = slice(512 + g * 2 * LANES, 512 + (g + 1) * 2 * LANES)
        k4_ref[:, g * 2 * LANES:(g + 1) * 2 * LANES] = (pa[:, sl] + aug).astype(BF16)
    v2_ref[...] = pa[:, 1024:1280].astype(BF16)
    qi_ref[...] = pa[:, 1280:1792].astype(BF16)
    ki = pa[:, 1792:1920]
    mu = jnp.mean(ki, axis=-1, keepdims=True)
    var = jnp.mean(jnp.square(ki - mu), axis=-1, keepdims=True)
    ki2_ref[...] = ((ki - mu) * lax.rsqrt(var + LN_EPS) * lnw_ref[...] + lnb_ref[...]).astype(BF16)
    wi_ref[...] = pa[:, 1920:2048]
    pr_ref[...] = _nn(xb, wr_ref[...])


def _inproj(x2, wa, wr, lnw2, lnb2, tm, seq):
    m, d = x2.shape
    assert seq <= 256 * POS_SPLIT
    full = lambda a: pl.BlockSpec(a.shape, lambda i: (0,) * a.ndim)
    row = lambda n: pl.BlockSpec((tm, n), lambda i: (i, 0))
    outs = [(512, BF16), (512, BF16), (256, BF16), (512, BF16), (128, BF16), (128, F32),
            (wr.shape[1], F32)]
    return pl.pallas_call(
        functools.partial(_inproj_kernel, seq=seq),
        grid=(m // tm,),
        in_specs=[row(d), full(wa), full(wr), full(lnw2), full(lnb2)],
        out_specs=[row(n) for n, _ in outs],
        out_shape=[jax.ShapeDtypeStruct((m, n), dt) for n, dt in outs],
        compiler_params=pltpu.CompilerParams(
            dimension_semantics=("parallel",), vmem_limit_bytes=VMEM_LIMIT),
        name="inproj",
    )(x2, wa, wr, lnw2, lnb2)


def _bf16_parts(x):
    parts = []
    for _ in range(SLOPE_TERMS):
        parts.append(float(ml_dtypes.bfloat16(x)))
        x -= parts[-1]
    return parts


def _attn_body(q_ref, qi_ref, wi_ref, k4_ref, v2_ref, ki2_ref, o_ref,
               key_ref, bias_ref, *, t0, t0_max, tq, width, n_sel, cw):
    lane = lax.broadcasted_iota(I32, (tq, LANES), 1)
    lo = lane < HEAD_DIM
    n_cc = width // cw
    n_lc = width // LANES

    def head_lhs(ref, h, fill=None):
        pair = ref[0, :, (h // 2) * LANES:(h // 2 + 1) * LANES]
        fill = jnp.zeros_like(pair) if fill is None else fill
        return jnp.where(lo if h % 2 == 0 else jnp.logical_not(lo), pair, fill)

    wi = wi_ref[0] * (HEAD_DIM ** -0.5 * HEADS ** -0.5)
    row_c = t0 + lax.broadcasted_iota(I32, (tq, cw), 0)
    col_c = lax.broadcasted_iota(I32, (tq, cw), 1)
    for c in range(n_cc):
        kc = ki2_ref[0, c * cw:(c + 1) * cw, :]
        s = jnp.zeros((tq, cw), F32)
        for h in range(HEADS):
            s = s + jnp.maximum(_nt(head_lhs(qi_ref, h), kc), 0.0) * wi[:, h:h + 1]
        key_ref[:, c * cw:(c + 1) * cw] = jnp.where(col_c + c * cw <= row_c, s, -jnp.inf)

    kf = jnp.float32(n_sel)
    rb = BISECT_ROWS
    n_rb = tq // rb
    blocks = [slice(b * rb, (b + 1) * rb) for b in range(n_rb)]
    sls = [slice(c * LANES, (c + 1) * LANES) for c in range(n_lc)]
    n_live = {blk.start: min(n_lc, -(-(t0_max + blk.stop) // LANES)) for blk in blocks}
    key_lowest = jnp.int32(INT_MIN + 2 ** 23)
    key_highest = jnp.int32(0x7F800000)

    def as_float(key):
        key = jnp.clip(key, key_lowest, key_highest)
        bits = jnp.where(key < 0, key ^ jnp.int32(0x7FFFFFFF), key)
        return lax.bitcast_convert_type(bits, F32)

    def count(pred_fn, rows, read=lambda ref, rows, sl: ref[rows, sl], ref=key_ref):
        acc = jnp.zeros((rb, LANES), F32)
        for sl in sls[:n_live[rows.start]]:
            acc = acc + jnp.where(pred_fn(read(ref, rows, sl)), 1.0, 0.0)
        return jnp.broadcast_to(jnp.sum(acc, axis=1, keepdims=True), (rb, LANES))

    def bis_body(i, ts):
        inc = lax.shift_left(jnp.int32(1), jnp.int32(31) - i)
        out = []
        for rows, t in zip(blocks, ts):
            cand = t + inc
            cand_f = as_float(cand)
            out.append(jnp.where(count(lambda sc: sc >= cand_f, rows) >= kf, cand, t))
        return tuple(out)

    ts = lax.fori_loop(0, 32, bis_body, tuple(jnp.full((rb, LANES), INT_MIN, I32) for _ in blocks),
                       unroll=BISECT_UNROLL)
    thrs = [as_float(t) for t in ts]

    excess = [count(lambda sc: sc >= t, rows) - kf for rows, t in zip(blocks, thrs)]
    has_ties = functools.reduce(jnp.maximum, [jnp.max(e) for e in excess]) > 0.0

    @pl.when(jnp.logical_not(has_ties))
    def _():
        for rows, t in zip(blocks, thrs):
            for sl in sls:
                bias_ref[rows, sl] = jnp.where(key_ref[rows, sl] >= t, 0.0, -jnp.inf)

    @pl.when(has_ties)
    def _():
        lane_rb = lax.broadcasted_iota(I32, (rb, LANES), 1)
        read_pos = lambda ref, rows, sl: lax.bitcast_convert_type(ref[rows, sl], I32)
        nbits = int(np.log2(width - 1)) + 1
        for rows, t in zip(blocks, thrs):
            need = kf - count(lambda kc: kc > t, rows)
            for c, sl in enumerate(sls):
                tpos = jnp.where(key_ref[rows, sl] == t, lane_rb + c * LANES, jnp.int32(2 ** 30))
                bias_ref[rows, sl] = lax.bitcast_convert_type(tpos, F32)

            def tie_body(i, j, rows=rows, need=need):
                cand = j + lax.shift_left(jnp.int32(1), jnp.int32(nbits - 1) - i)
                below = count(lambda pc: pc < cand, rows, read_pos, bias_ref)
                return jnp.where(below < need, cand, j)

            jmax = lax.fori_loop(0, nbits, tie_body, jnp.zeros((rb, LANES), I32))
            for sl in sls:
                keep = jnp.where(read_pos(bias_ref, rows, sl) <= jmax, 0.0, -jnp.inf)
                bias_ref[rows, sl] = jnp.where(key_ref[rows, sl] > t, 0.0, keep)

    bias = bias_ref[:, 0:width]
    a_idx = jnp.where(lo, lane, lane - HEAD_DIM)
    even = None
    for h in range(HEADS):
        g = h // (HEADS // KV_HEADS)
        parts = _bf16_parts((2.0 ** -(h + 1)) * LOG2E)
        fill = jnp.zeros((tq, LANES), F32)
        for n in reversed(range(SLOPE_TERMS)):
            fill = jnp.where(a_idx < 2 * (n + 1), parts[n], fill)
        fill = fill.astype(BF16)
        kcol = (2 * g + h % 2) * LANES
        logit = _nt(head_lhs(q_ref, h, fill), k4_ref[0, 0:width, kcol:kcol + LANES]) + bias
        m = jnp.max(logit, axis=1, keepdims=True)
        e = jnp.exp2(logit - m)
        ssum = jnp.sum(e, axis=1, keepdims=True)
        out = _nn(e.astype(BF16), v2_ref[0, 0:width, g * LANES:(g + 1) * LANES]) / ssum
        if h % 2 == 0:
            even = out
        else:
            p = h // 2
            o_ref[0, :, p * LANES:(p + 1) * LANES] = jnp.where(lo, even, out).astype(o_ref.dtype)


def _attn_kernel(q_ref, qi_ref, wi_ref, k4_ref, v2_ref, ki2_ref, o_ref,
                 key_ref, bias_ref, *, tq, seq, n_sel, n_classes):
    i = pl.program_id(1)
    tiles_per_class = (seq // tq) // n_classes
    for cls in range(n_classes):
        width = (cls + 1) * tiles_per_class * tq

        @pl.when((i >= cls * tiles_per_class) & (i < (cls + 1) * tiles_per_class))
        def _():
            _attn_body(q_ref, qi_ref, wi_ref, k4_ref, v2_ref, ki2_ref, o_ref, key_ref, bias_ref,
                       t0=i * tq, t0_max=width - tq, tq=tq, width=width, n_sel=n_sel,
                       cw=512 if width % 512 == 0 else 256)


def _attention(q, qi, wi, k4, v2, ki2, tq):
    b, seq, _ = q.shape
    n_sel = min(TOPK_MAX, seq // 4)
    n_classes = min(ATTN_CLASSES, seq // tq)
    qblk = lambda n: pl.BlockSpec((1, tq, n), lambda bi, i: (bi, i, 0))
    kblk = lambda n: pl.BlockSpec((1, seq, n), lambda bi, i: (bi, 0, 0))
    return pl.pallas_call(
        functools.partial(_attn_kernel, tq=tq, seq=seq, n_sel=n_sel, n_classes=n_classes),
        grid=(b, seq // tq),
        in_specs=[qblk(512), qblk(512), qblk(128), kblk(512), kblk(256), kblk(128)],
        out_specs=qblk(512),
        out_shape=jax.ShapeDtypeStruct((b, seq, 512), BF16),
        scratch_shapes=[pltpu.VMEM((tq, seq), F32), pltpu.VMEM((tq, seq), F32)],
        compiler_params=pltpu.CompilerParams(
            dimension_semantics=("parallel", "parallel"), vmem_limit_bytes=VMEM_LIMIT),
        name="dsa_attention",
    )(q, qi, wi, k4, v2, ki2)


def _bf(x):
    return x.astype(BF16)


def _rwkv_kernel(pr_ref, mu_ref, w0_ref, wup_ref, a0_ref, aup_ref, gup_ref, kk_ref, ka_ref,
                 rk_ref, lnw_ref, lnb_ref, o_ref,
                 state_ref, prev_ref, tinv_ref, w_ref, atrt_ref, bhm_ref, y_ref, *, c):
    nb, tl, width = o_ref.shape
    n_pairs = width // LANES
    n_chunks = tl // c
    rows2 = 2 * c
    n_bp = nb * n_pairs

    @pl.when(pl.program_id(1) == 0)
    def _():
        state_ref[...] = jnp.zeros_like(state_ref)
        prev_ref[...] = jnp.zeros_like(prev_ref)

    ti = lax.broadcasted_iota(I32, (tl, tl), 0)
    tj = lax.broadcasted_iota(I32, (tl, tl), 1)
    same_chunk = (ti // c) == (tj // c)
    tri = jnp.where(same_chunk & (ti >= tj), 1.0, 0.0).astype(BF16)
    blk = jnp.where(same_chunk, 1.0, 0.0).astype(BF16)
    li = lax.broadcasted_iota(I32, (LANES, LANES), 0)
    lj = lax.broadcasted_iota(I32, (LANES, LANES), 1)
    ones_bd = jnp.where((li // HEAD_DIM) == (lj // HEAD_DIM), 1.0, 0.0).astype(BF16)
    upper_strict = li < lj
    upper_incl = li <= lj
    eye = jnp.where(li == lj, 1.0, 0.0)
    lo = lax.broadcasted_iota(I32, (c, LANES), 1) < HEAD_DIM
    rows = lax.broadcasted_iota(I32, (tl, pr_ref.shape[-1]), 0)

    def head_sum(x):
        return _nn(_bf(x), ones_bd)

    def stack(x):
        return _bf(jnp.concatenate([jnp.where(lo, x, 0.0), jnp.where(lo, 0.0, x)], axis=0))

    n_pc = n_chunks * n_bp
    atrt_l, btkt_l, vst_l, kh_l, bh_l = ([None] * n_pc for _ in range(5))
    kept = []
    for bi in range(nb):
        p = pr_ref[bi]
        shifted = jnp.where(rows == 0, prev_ref[bi, 7:8, :], pltpu.roll(p, 1, 0))
        prev_ref[bi] = p[tl - 8:tl, :]
        ps = p + (shifted - p) * mu_ref[...]
        r = ps[:, 0:width]
        k = ps[:, width:2 * width]
        v = ps[:, 2 * width:3 * width]
        lora = ps[:, 3 * width:3 * width + LORA_W]
        nz = -(w0_ref[...] + _nn(_bf(jnp.tanh(lora)), wup_ref[...]))
        log_w = -(jnp.maximum(nz, 0.0) + jnp.log(1.0 + jnp.exp(-jnp.abs(nz)))) - 0.5
        lw = -jnp.exp(log_w)
        a_sig = jax.nn.sigmoid(a0_ref[...] + _nn(_bf(lora), aup_ref[...]))
        gate = _nn(_bf(jax.nn.sigmoid(lora)), gup_ref[...])
        lw_hi = _bf(lw)
        lw_lo = _bf(lw - lw_hi.astype(F32))
        cum = _nn(tri, lw_hi) + _nn(tri, lw_lo)
        tot = _nn(blk, lw_hi) + _nn(blk, lw_lo)
        kept.append((r, k, v, a_sig, gate, tot))
        for pi in range(n_pairs):
            sl = slice(pi * LANES, (pi + 1) * LANES)
            r_p, k_p, v_p, a_p = r[:, sl], k[:, sl], v[:, sl], a_sig[:, sl]
            kk = k_p * kk_ref[:, sl]
            kk = kk * lax.rsqrt(jnp.maximum(head_sum(kk * kk), 1e-24))
            kmod = k_p * (1.0 + (a_p - 1.0) * ka_ref[:, sl])
            bvec = kk * a_p
            cum_p, tot_p = cum[:, sl], tot[:, sl]
            p_inv = jnp.exp(-cum_p)
            p_end = jnp.exp(tot_p - cum_p)
            at_all = -kk * jnp.exp(cum_p - lw[:, sl])
            rt_all = r_p * jnp.exp(cum_p)
            bt_all, kt_all = bvec * p_inv, kmod * p_inv
            bh_all, kh_all = bvec * p_end, kmod * p_end
            for ci in range(n_chunks):
                rs = slice(ci * c, (ci + 1) * c)
                idx = (ci * nb + bi) * n_pairs + pi
                atrt_l[idx] = jnp.concatenate([stack(at_all[rs]), stack(rt_all[rs])], axis=0)
                btkt_l[idx] = jnp.concatenate([stack(bt_all[rs]), stack(kt_all[rs])], axis=0)
                vst_l[idx] = _bf(jnp.concatenate([jnp.where(lo, v_p[rs], 0.0),
                                                  jnp.where(lo, 0.0, v_p[rs])], axis=0).T)
                kh_l[idx] = stack(kh_all[rs])
                bh_l[idx] = stack(bh_all[rs])

    atrt = jnp.stack(atrt_l)
    gt = _bnt(jnp.stack(btkt_l), atrt)
    n_t = jnp.where(upper_strict, gt[:, 0:rows2, 0:rows2], 0.0)
    m_rb_t = jnp.where(upper_incl, gt[:, 0:rows2, rows2:], 0.0)
    m_ak_t = jnp.where(upper_strict, gt[:, rows2:, 0:rows2], 0.0)
    m_rk_t = jnp.where(upper_incl, gt[:, rows2:, rows2:], 0.0)
    tinv = eye + n_t
    npow = _bf(n_t)
    for _ in range(int(np.log2(c)) - 1):
        npow = _bf(_bnn(npow, npow))
        tinv = tinv + _bnn(_bf(tinv), npow)
    rhs = jnp.concatenate([_bf(m_ak_t), jnp.stack(kh_l), _bf(m_rk_t)], axis=2)
    w_ref[...] = _bnn(jnp.stack(vst_l), rhs)
    tinv_ref[...] = _bf(tinv)
    atrt_ref[...] = atrt
    bhm_ref[...] = jnp.concatenate([jnp.stack(bh_l), _bf(m_rb_t)], axis=2)

    for ci in range(n_chunks):
        rs = slice(ci * c, (ci + 1) * c)
        ids = slice(ci * n_bp, (ci + 1) * n_bp)
        state = state_ref[...]
        x = _bnt(_bf(state), atrt_ref[ids])
        w = w_ref[ids]
        us_t = _bnn(_bf(x[:, :, 0:rows2] + w[:, :, 0:rows2]), tinv_ref[ids])
        z = _bnn(_bf(us_t), bhm_ref[ids])
        decay_c = jnp.exp(jnp.stack([kept[bi][5][ci * c:ci * c + 1, pi * LANES:(pi + 1) * LANES]
                                     for bi in range(nb) for pi in range(n_pairs)]))
        state_ref[...] = state * decay_c + z[:, :, 0:rows2] + w[:, :, rows2:2 * rows2]
        ys_t = x[:, :, rows2:] + z[:, :, rows2:] + w[:, :, 2 * rows2:]
        for bi in range(nb):
            for pi in range(n_pairs):
                ys = ys_t[bi * n_pairs + pi].T
                y_ref[bi, rs, pi * LANES:(pi + 1) * LANES] = ys[0:c, :] + ys[c:rows2, :]

    for bi in range(nb):
        r, k, v, a_sig, gate, _ = kept[bi]
        for pi in range(n_pairs):
            sl = slice(pi * LANES, (pi + 1) * LANES)
            y = y_ref[bi, :, sl]
            mean = head_sum(y) * (1.0 / HEAD_DIM)
            yc = y - mean
            var = head_sum(yc * yc) * (1.0 / HEAD_DIM)
            yn = yc * lax.rsqrt(var + GN_EPS) * lnw_ref[:, sl] + lnb_ref[:, sl]
            kmod = k[:, sl] * (1.0 + (a_sig[:, sl] - 1.0) * ka_ref[:, sl])
            bonus = head_sum(r[:, sl] * kmod * rk_ref[:, sl]) * v[:, sl]
            o_ref[bi, :, sl] = ((yn + bonus) * gate[:, sl]).astype(o_ref.dtype)


def _rwkv(pr, vecs, mats, c, tl, nb):
    b, seq, pw = pr.shape
    width = (pw - LORA_W) // 3
    n_pairs = width // LANES
    n_pc = (tl // c) * nb * n_pairs
    full = lambda a: pl.BlockSpec(a.shape, lambda bi, i: (0,) * a.ndim)
    mu, w0, a0, kk, ka, rk, lnw, lnb = vecs
    wup, aup, gup = [m.astype(BF16) for m in mats]
    return pl.pallas_call(
        functools.partial(_rwkv_kernel, c=c),
        grid=(b // nb, seq // tl),
        in_specs=[pl.BlockSpec((nb, tl, pw), lambda bi, i: (bi, i, 0)),
                  full(mu), full(w0), full(wup), full(a0), full(aup), full(gup),
                  full(kk), full(ka), full(rk), full(lnw), full(lnb)],
        out_specs=pl.BlockSpec((nb, tl, width), lambda bi, i: (bi, i, 0)),
        out_shape=jax.ShapeDtypeStruct((b, seq, width), BF16),
        scratch_shapes=[pltpu.VMEM((nb * n_pairs, LANES, LANES), F32),
                        pltpu.VMEM((nb, 8, pw), F32),
                        pltpu.VMEM((n_pc, 2 * c, 2 * c), BF16),
                        pltpu.VMEM((n_pc, 2 * c, 6 * c), F32),
                        pltpu.VMEM((n_pc, 4 * c, LANES), BF16),
                        pltpu.VMEM((n_pc, 2 * c, 4 * c), BF16),
                        pltpu.VMEM((nb, tl, width), F32)],
        compiler_params=pltpu.CompilerParams(
            dimension_semantics=("parallel", "arbitrary"), vmem_limit_bytes=VMEM_LIMIT),
        name="rwkv7",
    )(pr, mu, w0, wup, a0, aup, gup, kk, ka, rk, lnw, lnb)


def _layer_norm(z, w, b):
    mu = jnp.mean(z, axis=-1, keepdims=True)
    zc = z - mu
    var = jnp.mean(zc * zc, axis=-1, keepdims=True)
    return zc * lax.rsqrt(var + LN_EPS) * w + b


def _merge_kernel(x_ref, ya_ref, yr_ref, wga_ref, wgr_ref, wa_ref, wb_ref, wo_ref, lnw_ref, lnb_ref,
                  o_ref, *, alpha):
    x = x_ref[...]
    xb = x.astype(BF16)
    mix = (jax.nn.sigmoid(_nn(xb, wga_ref[...])) * _nn(ya_ref[...], wa_ref[...])
           + jax.nn.sigmoid(_nn(xb, wgr_ref[...])) * _nn(yr_ref[...], wb_ref[...]))
    z = alpha * x + _nn(mix.astype(BF16), wo_ref[...])
    o_ref[...] = _layer_norm(z, lnw_ref[...], lnb_ref[...])


def _merge(x2, ya, yr, wga, wgr, wa, wb, wo, lnw, lnb, tm, alpha):
    m, d = x2.shape
    full = lambda a: pl.BlockSpec(a.shape, lambda i: (0,) * a.ndim)
    row = lambda n: pl.BlockSpec((tm, n), lambda i: (i, 0))
    return pl.pallas_call(
        functools.partial(_merge_kernel, alpha=alpha),
        grid=(m // tm,),
        in_specs=[row(d), row(ya.shape[1]), row(yr.shape[1]),
                  full(wga), full(wgr), full(wa), full(wb), full(wo), full(lnw), full(lnb)],
        out_specs=row(d),
        out_shape=jax.ShapeDtypeStruct((m, d), F32),
        compiler_params=pltpu.CompilerParams(
            dimension_semantics=("parallel",), vmem_limit_bytes=VMEM_LIMIT),
        name="merge_out_ln",
    )(x2, ya, yr, wga, wgr, wa, wb, wo, lnw, lnb)


SUBLANES = 8
FFN_COLS = 256
FFN_VMEM_LIMIT = 56 * 1024 * 1024


def _ffn_kernel(h_ref, wug_ref, wuu_ref, cwg_ref, cwu_ref, cbg_ref, cbu_ref, wd_ref,
                lnw_ref, lnb_ref, o_ref, pg_ref, pu_ref, act_ref, *, alpha, tiles_per_seq):
    tm = h_ref.shape[0]
    dff = wd_ref.shape[0]
    h = h_ref[...]
    hb = h.astype(BF16)

    @pl.when(pl.program_id(0) % tiles_per_seq == 0)
    def _():
        pg_ref[...] = jnp.zeros_like(pg_ref)
        pu_ref[...] = jnp.zeros_like(pu_ref)

    def conv(u, prev_ref, cw_ref, cb_ref, cols):
        ext = jnp.concatenate([prev_ref[:, cols], u], axis=0)
        prev_ref[:, cols] = u[tm - SUBLANES:, :]
        w = cw_ref[:, cols]
        back1 = pltpu.roll(ext, 1, 0)[SUBLANES:]
        back2 = pltpu.roll(ext, 2, 0)[SUBLANES:]
        return w[0:1] * back2 + w[1:2] * back1 + w[2:3] * u + cb_ref[:, cols]

    for c in range(dff // FFN_COLS):
        cols = slice(c * FFN_COLS, (c + 1) * FFN_COLS)
        gate = conv(_nn(hb, wug_ref[:, cols]), pg_ref, cwg_ref, cbg_ref, cols)
        up = conv(_nn(hb, wuu_ref[:, cols]), pu_ref, cwu_ref, cbu_ref, cols)
        act_ref[:, cols] = (gate * jax.nn.sigmoid(gate) * up).astype(BF16)
    o_ref[...] = _layer_norm(alpha * h + _nn(act_ref[...], wd_ref[...]), lnw_ref[...], lnb_ref[...])


def _ffn(h1, wug, wuu, cwg, cwu, cbg, cbu, wd, lnw, lnb, tm, seq, alpha):
    m, d = h1.shape
    dff = wd.shape[0]
    assert dff % FFN_COLS == 0 and seq % tm == 0
    const = lambda a: pl.BlockSpec(a.shape, lambda i: (0,) * a.ndim, pipeline_mode=pl.Buffered(1))
    return pl.pallas_call(
        functools.partial(_ffn_kernel, alpha=alpha, tiles_per_seq=seq // tm),
        grid=(m // tm,),
        in_specs=[pl.BlockSpec((tm, d), lambda i: (i, 0))]
        + [const(a) for a in (wug, wuu, cwg, cwu, cbg, cbu, wd, lnw, lnb)],
        out_specs=pl.BlockSpec((tm, d), lambda i: (i, 0)),
        out_shape=jax.ShapeDtypeStruct((m, d), F32),
        scratch_shapes=[pltpu.VMEM((SUBLANES, dff), F32), pltpu.VMEM((SUBLANES, dff), F32),
                        pltpu.VMEM((tm, dff), BF16)],
        compiler_params=pltpu.CompilerParams(
            dimension_semantics=("arbitrary",), vmem_limit_bytes=FFN_VMEM_LIMIT),
        name="conv_ffn_ln",
    )(h1, wug, wuu, cwg, cwu, cbg, cbu, wd, lnw, lnb)


def _tile_sizes(batch, seq):
    return dict(tm_proj=min(256, seq), tq=min(256, seq), chunk=64, tl_rwkv=min(256, seq),
                nb_rwkv=2 if batch % 2 == 0 else 1, tm_merge=min(256, seq), tm_ffn=min(512, seq))


def _layer(h, w_in, idx_w, idx_b, mu, w0, w_up, a0, a_up, g_up, k_k, k_a, r_k, gn_w, gn_b,
           w_ba, w_br, w_out, ln1_w, ln1_b, w_up_ffn, conv_w, conv_b, w_down, ln2_w, ln2_b, alpha):
    b, seq, d = h.shape
    ts = _tile_sizes(b, seq)
    aw = HEADS * HEAD_DIM
    kvw = KV_HEADS * HEAD_DIM
    o = np.cumsum([0, aw, kvw, kvw, aw, HEAD_DIM, HEADS, 3 * aw + LORA_W, d, d])
    col = lambda i: w_in[:, o[i]:o[i + 1]]
    head = lambda w, hh: w[:, hh * HEAD_DIM:(hh + 1) * HEAD_DIM]
    zero = jnp.zeros((d, HEAD_DIM), w_in.dtype)
    k_cols = [blk for hh in range(KV_HEADS)
              for blk in (head(col(1), hh), zero, zero, head(col(1), hh))]
    v_cols = [head(col(2), hh) for hh in range(KV_HEADS) for _ in range(2)]
    wi_pad = jnp.pad(col(5), ((0, 0), (0, LANES - HEADS)))
    wa = jnp.concatenate([col(0)] + k_cols + v_cols + [col(3), col(4), col(4), wi_pad],
                         axis=1).astype(BF16)
    wr = col(6).astype(BF16)
    row2 = lambda v: v.reshape(1, -1)
    lnw2 = row2(jnp.concatenate([idx_w, idx_w]))
    lnb2 = row2(jnp.concatenate([idx_b, idx_b]))

    x2 = h.reshape(b * seq, d)
    q, k4, v2, qi, ki2, wi, pr = _inproj(x2, wa, wr, lnw2, lnb2, ts["tm_proj"], seq)
    r3 = lambda a: a.reshape(b, seq, a.shape[-1])

    y_attn = _attention(r3(q), r3(qi), r3(wi), r3(k4), r3(v2), r3(ki2), ts["tq"])

    ld, la = w_up.shape[0], a_up.shape[0]
    pad_rows = lambda w, start: jnp.pad(w, ((start, LORA_W - start - w.shape[0]), (0, 0)))
    vecs = [row2(v) for v in (mu, w0, a0, k_k, k_a, r_k.reshape(-1), gn_w, gn_b)]
    mats = [pad_rows(w_up, 0), pad_rows(a_up, ld), pad_rows(g_up, ld + la)]
    y_rwkv = _rwkv(r3(pr), vecs, mats, ts["chunk"], ts["tl_rwkv"], ts["nb_rwkv"])

    h1 = _merge(x2, y_attn.reshape(b * seq, aw), y_rwkv.reshape(b * seq, aw),
                col(7).astype(BF16), col(8).astype(BF16),
                w_ba.astype(BF16), w_br.astype(BF16), w_out.astype(BF16),
                row2(ln1_w), row2(ln1_b), ts["tm_merge"], alpha)

    dff = w_down.shape[0]
    out = _ffn(h1, w_up_ffn[:, :dff].astype(BF16), w_up_ffn[:, dff:].astype(BF16),
               conv_w[:, :dff], conv_w[:, dff:], row2(conv_b[:dff]), row2(conv_b[dff:]),
               w_down.astype(BF16), row2(ln2_w), row2(ln2_b), ts["tm_ffn"], seq, alpha)
    return out.reshape(b, seq, d)


def kernel(x, w_in, idx_k_norm_w, idx_k_norm_b, rwkv_mu, rwkv_w0, rwkv_w_up, rwkv_a0, rwkv_a_up,
           rwkv_g_up, rwkv_k_k, rwkv_k_a, rwkv_r_k, rwkv_ln_w, rwkv_ln_b, w_branch_attn,
           w_branch_rwkv, w_out, ln1_w, ln1_b, w_up, conv_w, conv_b, w_down, ln2_w, ln2_b):
    depth = w_in.shape[0]
    alpha = (2.0 * depth) ** 0.25
    h = x
    for l in range(depth):
        h = _layer(h, w_in[l], idx_k_norm_w[l], idx_k_norm_b[l], rwkv_mu[l], rwkv_w0[l],
                   rwkv_w_up[l], rwkv_a0[l], rwkv_a_up[l], rwkv_g_up[l], rwkv_k_k[l], rwkv_k_a[l],
                   rwkv_r_k[l], rwkv_ln_w[l], rwkv_ln_b[l], w_branch_attn[l], w_branch_rwkv[l],
                   w_out[l], ln1_w[l], ln1_b[l], w_up[l], conv_w[l], conv_b[l], w_down[l],
                   ln2_w[l], ln2_b[l], alpha)
    return h
```

```python
import functools

import jax
import jax.numpy as jnp
import ml_dtypes
import numpy as np
from jax import lax
from jax.experimental import pallas as pl
from jax.experimental.pallas import tpu as pltpu

F32 = jnp.float32
BF16 = jnp.bfloat16
I32 = jnp.int32

LANES = 128
HEAD_DIM = 64
HEADS = 8
KV_HEADS = 2
TOPK_MAX = 256
LORA_W = 128
LN_EPS = 1e-5
GN_EPS = 64e-5
INT_MIN = -(2 ** 31)
ATTN_CLASSES = 4
BISECT_PASSES = 20
BISECT_ROWS = 64
BISECT_UNROLL = 4
VMEM_LIMIT = 48 * 1024 * 1024


def _nt(a, b):
    return lax.dot_general(a, b, (((1,), (1,)), ((), ())), preferred_element_type=F32)


def _nn(a, b):
    return lax.dot_general(a, b, (((1,), (0,)), ((), ())), preferred_element_type=F32)


def _bnt(a, b):
    return lax.dot_general(a, b, (((2,), (2,)), ((0,), (0,))), preferred_element_type=F32)


def _bnn(a, b):
    return lax.dot_general(a, b, (((2,), (1,)), ((0,), (0,))), preferred_element_type=F32)


def _split2(x):
    hi = x.astype(BF16)
    lo = (x - hi.astype(F32)).astype(BF16)
    return hi, lo


def _mm3(a, b, dot=_nn):
    ah, al = _split2(a)
    bh, bl = _split2(b)
    return dot(ah, bh) + (dot(ah, bl) + dot(al, bh))


def _mm_exact_lhs(a_bf16, b):
    b0 = b.astype(BF16)
    r1 = b - b0.astype(F32)
    b1 = r1.astype(BF16)
    b2 = (r1 - b1.astype(F32)).astype(BF16)
    return _nn(a_bf16, b0) + (_nn(a_bf16, b1) + _nn(a_bf16, b2))


LOG2E = 1.4426950408889634
POS_SPLIT = 16
SLOPE_TERMS = 3


def _inproj_kernel(x_ref, wa_ref, wr_ref, lnw_ref, lnb_ref,
                   q_ref, k4_ref, v2_ref, qi_ref, ki2_ref, wi_ref, pr_ref, *, seq):
    tm = x_ref.shape[0]
    xb = x_ref[...].astype(BF16)
    pa = _nn(xb, wa_ref[...])
    q_ref[...] = (pa[:, 0:512] * (HEAD_DIM ** -0.5 * LOG2E)).astype(BF16)
    pos = (pl.program_id(0) * tm) % seq + lax.broadcasted_iota(I32, (tm, 2 * LANES), 0)
    s_lo = pos & (POS_SPLIT - 1)
    ln = lax.broadcasted_iota(I32, (tm, 2 * LANES), 1)
    a = jnp.where(ln < LANES, ln - HEAD_DIM, ln - LANES)
    in_aug = (a >= 0) & (a < 2 * SLOPE_TERMS)
    aug = jnp.where(in_aug, jnp.where((a & 1) == 0, pos - s_lo, s_lo), 0).astype(F32)
    for g in range(KV_HEADS):
        sl = slice(512 + g * 2 * LANES, 512 + (g + 1) * 2 * LANES)
        k4_ref[:, g * 2 * LANES:(g + 1) * 2 * LANES] = (pa[:, sl] + aug).astype(BF16)
    v2_ref[...] = pa[:, 1024:1280].astype(BF16)
    qi_ref[...] = pa[:, 1280:1792].astype(BF16)
    ki = pa[:, 1792:1920]
    mu = jnp.mean(ki, axis=-1, keepdims=True)
    var = jnp.mean(jnp.square(ki - mu), axis=-1, keepdims=True)
    ki2_ref[...] = ((ki - mu) * lax.rsqrt(var + LN_EPS) * lnw_ref[...] + lnb_ref[...]).astype(BF16)
    wi_ref[...] = pa[:, 1920:2048]
    pr_ref[...] = _nn(xb, wr_ref[...])


def _inproj(x2, wa, wr, lnw2, lnb2, tm, seq):
    m, d = x2.shape
    assert seq <= 256 * POS_SPLIT
    full = lambda a: pl.BlockSpec(a.shape, lambda i: (0,) * a.ndim)
    row = lambda n: pl.BlockSpec((tm, n), lambda i: (i, 0))
    outs = [(512, BF16), (512, BF16), (256, BF16), (512, BF16), (128, BF16), (128, F32),
            (wr.shape[1], F32)]
    return pl.pallas_call(
        functools.partial(_inproj_kernel, seq=seq),
        grid=(m // tm,),
        in_specs=[row(d), full(wa), full(wr), full(lnw2), full(lnb2)],
        out_specs=[row(n) for n, _ in outs],
        out_shape=[jax.ShapeDtypeStruct((m, n), dt) for n, dt in outs],
        compiler_params=pltpu.CompilerParams(
            dimension_semantics=("parallel",), vmem_limit_bytes=VMEM_LIMIT),
        name="inproj",
    )(x2, wa, wr, lnw2, lnb2)


def _bf16_parts(x):
    parts = []
    for _ in range(SLOPE_TERMS):
        parts.append(float(ml_dtypes.bfloat16(x)))
        x -= parts[-1]
    return parts


def _attn_body(q_ref, qi_ref, wi_ref, k4_ref, v2_ref, ki2_ref, o_ref,
               key_ref, bias_ref, *, t0, t0_max, tq, width, n_sel, cw):
    lane = lax.broadcasted_iota(I32, (tq, LANES), 1)
    lo = lane < HEAD_DIM
    n_cc = width // cw
    n_lc = width // LANES

    def head_lhs(ref, h, fill=None):
        pair = ref[0, :, (h // 2) * LANES:(h // 2 + 1) * LANES]
        fill = jnp.zeros_like(pair) if fill is None else fill
        return jnp.where(lo if h % 2 == 0 else jnp.logical_not(lo), pair, fill)

    wi = wi_ref[0] * (HEAD_DIM ** -0.5 * HEADS ** -0.5)
    row_c = t0 + lax.broadcasted_iota(I32, (tq, cw), 0)
    col_c = lax.broadcasted_iota(I32, (tq, cw), 1)
    for c in range(n_cc):
        kc = ki2_ref[0, c * cw:(c + 1) * cw, :]
        s = jnp.zeros((tq, cw), F32)
        for h in range(HEADS):
            s = s + jnp.maximum(_nt(head_lhs(qi_ref, h), kc), 0.0) * wi[:, h:h + 1]
        key_ref[:, c * cw:(c + 1) * cw] = jnp.where(col_c + c * cw <= row_c, s, -jnp.inf)

    kf = jnp.float32(n_sel)
    rb = BISECT_ROWS
    n_rb = tq // rb
    blocks = [slice(b * rb, (b + 1) * rb) for b in range(n_rb)]
    sls = [slice(c * LANES, (c + 1) * LANES) for c in range(n_lc)]
    n_live = {blk.start: min(n_lc, -(-(t0_max + blk.stop) // LANES)) for blk in blocks}
    key_lowest = jnp.int32(INT_MIN + 2 ** 23)
    key_highest = jnp.int32(0x7F800000)

    def as_float(key):
        key = jnp.clip(key, key_lowest, key_highest)
        bits = jnp.where(key < 0, key ^ jnp.int32(0x7FFFFFFF), key)
        return lax.bitcast_convert_type(bits, F32)

    def count(pred_fn, rows, read=lambda ref, rows, sl: ref[rows, sl], ref=key_ref):
        acc = jnp.zeros((rb, LANES), F32)
        for sl in sls[:n_live[rows.start]]:
            acc = acc + jnp.where(pred_fn(read(ref, rows, sl)), 1.0, 0.0)
        return jnp.broadcast_to(jnp.sum(acc, axis=1, keepdims=True), (rb, LANES))

    def bis_body(i, ts):
        inc = lax.shift_left(jnp.int32(1), jnp.int32(31) - i)
        out = []
        for rows, t in zip(blocks, ts):
            cand = t + inc
            cand_f = as_float(cand)
            out.append(jnp.where(count(lambda sc: sc >= cand_f, rows) >= kf, cand, t))
        return tuple(out)

    ts = lax.fori_loop(0, BISECT_PASSES, bis_body,
                       tuple(jnp.full((rb, LANES), INT_MIN, I32) for _ in blocks),
                       unroll=BISECT_UNROLL)
    lows = [as_float(t) for t in ts]

    def smallest(pred_fn, rows):
        acc = jnp.full((rb, LANES), jnp.inf, F32)
        for sl in sls[:n_live[rows.start]]:
            sc = key_ref[rows, sl]
            acc = jnp.minimum(acc, jnp.where(pred_fn(sc), sc, jnp.inf))
        return jnp.broadcast_to(jnp.min(acc, axis=1, keepdims=True), (rb, LANES))

    def settle(m, thr, done, rows):
        newly = jnp.where(count(lambda sc: sc > m, rows) < kf, 1.0 - done, 0.0)
        return jnp.where(newly > 0.0, m, thr), jnp.maximum(done, newly)

    walk = []
    for rows, low in zip(blocks, lows):
        m = smallest(lambda sc: sc >= low, rows)
        walk.append((m,) + settle(m, low, jnp.zeros((rb, LANES), F32), rows))

    def walk_pending(state):
        return functools.reduce(jnp.minimum, [jnp.min(done) for _, _, done in state]) < 1.0

    def walk_step(state):
        out = []
        for rows, (m, thr, done) in zip(blocks, state):
            m = smallest(lambda sc: sc > m, rows)
            out.append((m,) + settle(m, thr, done, rows))
        return tuple(out)

    thrs = [thr for _, thr, _ in lax.while_loop(walk_pending, walk_step, tuple(walk))]

    excess = [count(lambda sc: sc >= t, rows) - kf for rows, t in zip(blocks, thrs)]
    has_ties = functools.reduce(jnp.maximum, [jnp.max(e) for e in excess]) > 0.0

    @pl.when(jnp.logical_not(has_ties))
    def _():
        for rows, t in zip(blocks, thrs):
            for sl in sls:
                bias_ref[rows, sl] = jnp.where(key_ref[rows, sl] >= t, 0.0, -jnp.inf)

    @pl.when(has_ties)
    def _():
        lane_rb = lax.broadcasted_iota(I32, (rb, LANES), 1)
        read_pos = lambda ref, rows, sl: lax.bitcast_convert_type(ref[rows, sl], I32)
        nbits = int(np.log2(width - 1)) + 1
        for rows, t in zip(blocks, thrs):
            need = kf - count(lambda kc: kc > t, rows)
            for c, sl in enumerate(sls):
                tpos = jnp.where(key_ref[rows, sl] == t, lane_rb + c * LANES, jnp.int32(2 ** 30))
                bias_ref[rows, sl] = lax.bitcast_convert_type(tpos, F32)

            def tie_body(i, j, rows=rows, need=need):
                cand = j + lax.shift_left(jnp.int32(1), jnp.int32(nbits - 1) - i)
                below = count(lambda pc: pc < cand, rows, read_pos, bias_ref)
                return jnp.where(below < need, cand, j)

            jmax = lax.fori_loop(0, nbits, tie_body, jnp.zeros((rb, LANES), I32))
            for sl in sls:
                keep = jnp.where(read_pos(bias_ref, rows, sl) <= jmax, 0.0, -jnp.inf)
                bias_ref[rows, sl] = jnp.where(key_ref[rows, sl] > t, 0.0, keep)

    bias = bias_ref[:, 0:width]
    a_idx = jnp.where(lo, lane, lane - HEAD_DIM)
    even = None
    for h in range(HEADS):
        g = h // (HEADS // KV_HEADS)
        parts = _bf16_parts((2.0 ** -(h + 1)) * LOG2E)
        fill = jnp.zeros((tq, LANES), F32)
        for n in reversed(range(SLOPE_TERMS)):
            fill = jnp.where(a_idx < 2 * (n + 1), parts[n], fill)
        fill = fill.astype(BF16)
        kcol = (2 * g + h % 2) * LANES
        logit = _nt(head_lhs(q_ref, h, fill), k4_ref[0, 0:width, kcol:kcol + LANES]) + bias
        m = jnp.max(logit, axis=1, keepdims=True)
        e = jnp.exp2(logit - m)
        ssum = jnp.sum(e, axis=1, keepdims=True)
        out = _nn(e.astype(BF16), v2_ref[0, 0:width, g * LANES:(g + 1) * LANES]) / ssum
        if h % 2 == 0:
            even = out
        else:
            p = h // 2
            o_ref[0, :, p * LANES:(p + 1) * LANES] = jnp.where(lo, even, out).astype(o_ref.dtype)


def _attn_kernel(q_ref, qi_ref, wi_ref, k4_ref, v2_ref, ki2_ref, o_ref,
                 key_ref, bias_ref, *, tq, seq, n_sel, n_classes):
    i = pl.program_id(0)
    tiles_per_class = (seq // tq) // n_classes
    for cls in range(n_classes):
        width = (cls + 1) * tiles_per_class * tq

        @pl.when((i >= cls * tiles_per_class) & (i < (cls + 1) * tiles_per_class))
        def _():
            _attn_body(q_ref, qi_ref, wi_ref, k4_ref, v2_ref, ki2_ref, o_ref, key_ref, bias_ref,
                       t0=i * tq, t0_max=width - tq, tq=tq, width=width, n_sel=n_sel,
                       cw=512 if width % 512 == 0 else 256)


def _attention(q, qi, wi, k4, v2, ki2, tq):
    b, seq, _ = q.shape
    n_sel = min(TOPK_MAX, seq // 4)
    n_classes = min(ATTN_CLASSES, seq // tq)
    qblk = lambda n: pl.BlockSpec((1, tq, n), lambda i, bi: (bi, i, 0))
    kblk = lambda n: pl.BlockSpec((1, seq, n), lambda i, bi: (bi, 0, 0))
    return pl.pallas_call(
        functools.partial(_attn_kernel, tq=tq, seq=seq, n_sel=n_sel, n_classes=n_classes),
        grid=(seq // tq, b),
        in_specs=[qblk(512), qblk(512), qblk(128), kblk(512), kblk(256), kblk(128)],
        out_specs=qblk(512),
        out_shape=jax.ShapeDtypeStruct((b, seq, 512), BF16),
        scratch_shapes=[pltpu.VMEM((tq, seq), F32), pltpu.VMEM((tq, seq), F32)],
        compiler_params=pltpu.CompilerParams(
            dimension_semantics=("parallel", "parallel"), vmem_limit_bytes=VMEM_LIMIT),
        name="dsa_attention",
    )(q, qi, wi, k4, v2, ki2)


def _bf(x):
    return x.astype(BF16)


def _rwkv_kernel(pr_ref, mu_ref, w0_ref, wup_ref, a0_ref, aup_ref, gup_ref, kk_ref, ka_ref,
                 rk_ref, lnw_ref, lnb_ref, o_ref,
                 state_ref, prev_ref, tinv_ref, w_ref, atrt_ref, bhm_ref, y_ref, *, c):
    nb, tl, width = o_ref.shape
    n_pairs = width // LANES
    n_chunks = tl // c
    rows2 = 2 * c
    n_bp = nb * n_pairs

    @pl.when(pl.program_id(1) == 0)
    def _():
        state_ref[...] = jnp.zeros_like(state_ref)
        prev_ref[...] = jnp.zeros_like(prev_ref)

    ti = lax.broadcasted_iota(I32, (tl, tl), 0)
    tj = lax.broadcasted_iota(I32, (tl, tl), 1)
    same_chunk = (ti // c) == (tj // c)
    tri = jnp.where(same_chunk & (ti >= tj), 1.0, 0.0).astype(BF16)
    blk = jnp.where(same_chunk, 1.0, 0.0).astype(BF16)
    li = lax.broadcasted_iota(I32, (LANES, LANES), 0)
    lj = lax.broadcasted_iota(I32, (LANES, LANES), 1)
    ones_bd = jnp.where((li // HEAD_DIM) == (lj // HEAD_DIM), 1.0, 0.0).astype(BF16)
    upper_strict = li < lj
    upper_incl = li <= lj
    eye = jnp.where(li == lj, 1.0, 0.0)
    lo = lax.broadcasted_iota(I32, (c, LANES), 1) < HEAD_DIM
    rows = lax.broadcasted_iota(I32, (tl, pr_ref.shape[-1]), 0)

    def head_sum(x):
        return _nn(_bf(x), ones_bd)

    def stack(x):
        return _bf(jnp.concatenate([jnp.where(lo, x, 0.0), jnp.where(lo, 0.0, x)], axis=0))

    n_pc = n_chunks * n_bp
    atrt_l, btkt_l, vst_l, kh_l, bh_l = ([None] * n_pc for _ in range(5))
    kept = []
    for bi in range(nb):
        p = pr_ref[bi]
        shifted = jnp.where(rows == 0, prev_ref[bi, 7:8, :], pltpu.roll(p, 1, 0))
        prev_ref[bi] = p[tl - 8:tl, :]
        ps = p + (shifted - p) * mu_ref[...]
        r = ps[:, 0:width]
        k = ps[:, width:2 * width]
        v = ps[:, 2 * width:3 * width]
        lora = ps[:, 3 * width:3 * width + LORA_W]
        nz = -(w0_ref[...] + _nn(_bf(jnp.tanh(lora)), wup_ref[...]))
        log_w = -(jnp.maximum(nz, 0.0) + jnp.log(1.0 + jnp.exp(-jnp.abs(nz)))) - 0.5
        lw = -jnp.exp(log_w)
        a_sig = jax.nn.sigmoid(a0_ref[...] + _nn(_bf(lora), aup_ref[...]))
        gate = _nn(_bf(jax.nn.sigmoid(lora)), gup_ref[...])
        lw_hi = _bf(lw)
        lw_lo = _bf(lw - lw_hi.astype(F32))
        cum = _nn(tri, lw_hi) + _nn(tri, lw_lo)
        tot = _nn(blk, lw_hi) + _nn(blk, lw_lo)
        kept.append((r, k, v, a_sig, gate, tot))
        for pi in range(n_pairs):
            sl = slice(pi * LANES, (pi + 1) * LANES)
            r_p, k_p, v_p, a_p = r[:, sl], k[:, sl], v[:, sl], a_sig[:, sl]
            kk = k_p * kk_ref[:, sl]
            kk = kk * lax.rsqrt(jnp.maximum(head_sum(kk * kk), 1e-24))
            kmod = k_p * (1.0 + (a_p - 1.0) * ka_ref[:, sl])
            bvec = kk * a_p
            cum_p, tot_p = cum[:, sl], tot[:, sl]
            p_inv = jnp.exp(-cum_p)
            p_end = jnp.exp(tot_p - cum_p)
            at_all = -kk * jnp.exp(cum_p - lw[:, sl])
            rt_all = r_p * jnp.exp(cum_p)
            bt_all, kt_all = bvec * p_inv, kmod * p_inv
            bh_all, kh_all = bvec * p_end, kmod * p_end
            for ci in range(n_chunks):
                rs = slice(ci * c, (ci + 1) * c)
                idx = (ci * nb + bi) * n_pairs + pi
                atrt_l[idx] = jnp.concatenate([stack(at_all[rs]), stack(rt_all[rs])], axis=0)
                btkt_l[idx] = jnp.concatenate([stack(bt_all[rs]), stack(kt_all[rs])], axis=0)
                vst_l[idx] = _bf(jnp.concatenate([jnp.where(lo, v_p[rs], 0.0),
                                                  jnp.where(lo, 0.0, v_p[rs])], axis=0).T)
                kh_l[idx] = stack(kh_all[rs])
                bh_l[idx] = stack(bh_all[rs])

    atrt = jnp.stack(atrt_l)
    gt = _bnt(jnp.stack(btkt_l), atrt)
    n_t = jnp.where(upper_strict, gt[:, 0:rows2, 0:rows2], 0.0)
    m_rb_t = jnp.where(upper_incl, gt[:, 0:rows2, rows2:], 0.0)
    m_ak_t = jnp.where(upper_strict, gt[:, rows2:, 0:rows2], 0.0)
    m_rk_t = jnp.where(upper_incl, gt[:, rows2:, rows2:], 0.0)
    tinv = eye + n_t
    npow = _bf(n_t)
    for _ in range(int(np.log2(c)) - 1):
        npow = _bf(_bnn(npow, npow))
        tinv = tinv + _bnn(_bf(tinv), npow)
    rhs = jnp.concatenate([_bf(m_ak_t), jnp.stack(kh_l), _bf(m_rk_t)], axis=2)
    w_ref[...] = _bnn(jnp.stack(vst_l), rhs)
    tinv_ref[...] = _bf(tinv)
    atrt_ref[...] = atrt
    bhm_ref[...] = jnp.concatenate([jnp.stack(bh_l), _bf(m_rb_t)], axis=2)

    for ci in range(n_chunks):
        rs = slice(ci * c, (ci + 1) * c)
        ids = slice(ci * n_bp, (ci + 1) * n_bp)
        state = state_ref[...]
        x = _bnt(_bf(state), atrt_ref[ids])
        w = w_ref[ids]
        us_t = _bnn(_bf(x[:, :, 0:rows2] + w[:, :, 0:rows2]), tinv_ref[ids])
        z = _bnn(_bf(us_t), bhm_ref[ids])
        decay_c = jnp.exp(jnp.stack([kept[bi][5][ci * c:ci * c + 1, pi * LANES:(pi + 1) * LANES]
                                     for bi in range(nb) for pi in range(n_pairs)]))
        state_ref[...] = state * decay_c + z[:, :, 0:rows2] + w[:, :, rows2:2 * rows2]
        ys_t = x[:, :, rows2:] + z[:, :, rows2:] + w[:, :, 2 * rows2:]
        for bi in range(nb):
            for pi in range(n_pairs):
                ys = ys_t[bi * n_pairs + pi].T
                y_ref[bi, rs, pi * LANES:(pi + 1) * LANES] = ys[0:c, :] + ys[c:rows2, :]

    for bi in range(nb):
        r, k, v, a_sig, gate, _ = kept[bi]
        for pi in range(n_pairs):
            sl = slice(pi * LANES, (pi + 1) * LANES)
            y = y_ref[bi, :, sl]
            mean = head_sum(y) * (1.0 / HEAD_DIM)
            yc = y - mean
            var = head_sum(yc * yc) * (1.0 / HEAD_DIM)
            yn = yc * lax.rsqrt(var + GN_EPS) * lnw_ref[:, sl] + lnb_ref[:, sl]
            kmod = k[:, sl] * (1.0 + (a_sig[:, sl] - 1.0) * ka_ref[:, sl])
            bonus = head_sum(r[:, sl] * kmod * rk_ref[:, sl]) * v[:, sl]
            o_ref[bi, :, sl] = ((yn + bonus) * gate[:, sl]).astype(o_ref.dtype)


def _rwkv(pr, vecs, mats, c, tl, nb):
    b, seq, pw = pr.shape
    width = (pw - LORA_W) // 3
    n_pairs = width // LANES
    n_pc = (tl // c) * nb * n_pairs
    full = lambda a: pl.BlockSpec(a.shape, lambda bi, i: (0,) * a.ndim)
    mu, w0, a0, kk, ka, rk, lnw, lnb = vecs
    wup, aup, gup = [m.astype(BF16) for m in mats]
    return pl.pallas_call(
        functools.partial(_rwkv_kernel, c=c),
        grid=(b // nb, seq // tl),
        in_specs=[pl.BlockSpec((nb, tl, pw), lambda bi, i: (bi, i, 0)),
                  full(mu), full(w0), full(wup), full(a0), full(aup), full(gup),
                  full(kk), full(ka), full(rk), full(lnw), full(lnb)],
        out_specs=pl.BlockSpec((nb, tl, width), lambda bi, i: (bi, i, 0)),
        out_shape=jax.ShapeDtypeStruct((b, seq, width), BF16),
        scratch_shapes=[pltpu.VMEM((nb * n_pairs, LANES, LANES), F32),
                        pltpu.VMEM((nb, 8, pw), F32),
                        pltpu.VMEM((n_pc, 2 * c, 2 * c), BF16),
                        pltpu.VMEM((n_pc, 2 * c, 6 * c), F32),
                        pltpu.VMEM((n_pc, 4 * c, LANES), BF16),
                        pltpu.VMEM((n_pc, 2 * c, 4 * c), BF16),
                        pltpu.VMEM((nb, tl, width), F32)],
        compiler_params=pltpu.CompilerParams(
            dimension_semantics=("parallel", "arbitrary"), vmem_limit_bytes=VMEM_LIMIT),
        name="rwkv7",
    )(pr, mu, w0, wup, a0, aup, gup, kk, ka, rk, lnw, lnb)


def _layer_norm(z, w, b):
    mu = jnp.mean(z, axis=-1, keepdims=True)
    zc = z - mu
    var = jnp.mean(zc * zc, axis=-1, keepdims=True)
    return zc * lax.rsqrt(var + LN_EPS) * w + b


def _merge_kernel(x_ref, ya_ref, yr_ref, wga_ref, wgr_ref, wa_ref, wb_ref, wo_ref, lnw_ref, lnb_ref,
                  o_ref, *, alpha):
    x = x_ref[...]
    xb = x.astype(BF16)
    mix = (jax.nn.sigmoid(_nn(xb, wga_ref[...])) * _nn(ya_ref[...], wa_ref[...])
           + jax.nn.sigmoid(_nn(xb, wgr_ref[...])) * _nn(yr_ref[...], wb_ref[...]))
    z = alpha * x + _nn(mix.astype(BF16), wo_ref[...])
    o_ref[...] = _layer_norm(z, lnw_ref[...], lnb_ref[...])


def _merge(x2, ya, yr, wga, wgr, wa, wb, wo, lnw, lnb, tm, alpha):
    m, d = x2.shape
    full = lambda a: pl.BlockSpec(a.shape, lambda i: (0,) * a.ndim)
    row = lambda n: pl.BlockSpec((tm, n), lambda i: (i, 0))
    return pl.pallas_call(
        functools.partial(_merge_kernel, alpha=alpha),
        grid=(m // tm,),
        in_specs=[row(d), row(ya.shape[1]), row(yr.shape[1]),
                  full(wga), full(wgr), full(wa), full(wb), full(wo), full(lnw), full(lnb)],
        out_specs=row(d),
        out_shape=jax.ShapeDtypeStruct((m, d), F32),
        compiler_params=pltpu.CompilerParams(
            dimension_semantics=("parallel",), vmem_limit_bytes=VMEM_LIMIT),
        name="merge_out_ln",
    )(x2, ya, yr, wga, wgr, wa, wb, wo, lnw, lnb)


SUBLANES = 8
FFN_COLS = 256
FFN_VMEM_LIMIT = 56 * 1024 * 1024


def _ffn_kernel(h_ref, wug_ref, wuu_ref, cwg_ref, cwu_ref, cbg_ref, cbu_ref, wd_ref,
                lnw_ref, lnb_ref, o_ref, pg_ref, pu_ref, act_ref, *, alpha, tiles_per_seq):
    tm = h_ref.shape[0]
    dff = wd_ref.shape[0]
    h = h_ref[...]
    hb = h.astype(BF16)

    @pl.when(pl.program_id(0) % tiles_per_seq == 0)
    def _():
        pg_ref[...] = jnp.zeros_like(pg_ref)
        pu_ref[...] = jnp.zeros_like(pu_ref)

    def conv(u, prev_ref, cw_ref, cb_ref, cols):
        ext = jnp.concatenate([prev_ref[:, cols], u], axis=0)
        prev_ref[:, cols] = u[tm - SUBLANES:, :]
        w = cw_ref[:, cols]
        back1 = pltpu.roll(ext, 1, 0)[SUBLANES:]
        back2 = pltpu.roll(ext, 2, 0)[SUBLANES:]
        return w[0:1] * back2 + w[1:2] * back1 + w[2:3] * u + cb_ref[:, cols]

    for c in range(dff // FFN_COLS):
        cols = slice(c * FFN_COLS, (c + 1) * FFN_COLS)
        gate = conv(_nn(hb, wug_ref[:, cols]), pg_ref, cwg_ref, cbg_ref, cols)
        up = conv(_nn(hb, wuu_ref[:, cols]), pu_ref, cwu_ref, cbu_ref, cols)
        act_ref[:, cols] = (gate * jax.nn.sigmoid(gate) * up).astype(BF16)
    o_ref[...] = _layer_norm(alpha * h + _nn(act_ref[...], wd_ref[...]), lnw_ref[...], lnb_ref[...])


def _ffn(h1, wug, wuu, cwg, cwu, cbg, cbu, wd, lnw, lnb, tm, seq, alpha):
    m, d = h1.shape
    dff = wd.shape[0]
    assert dff % FFN_COLS == 0 and seq % tm == 0
    const = lambda a: pl.BlockSpec(a.shape, lambda i: (0,) * a.ndim, pipeline_mode=pl.Buffered(1))
    return pl.pallas_call(
        functools.partial(_ffn_kernel, alpha=alpha, tiles_per_seq=seq // tm),
        grid=(m // tm,),
        in_specs=[pl.BlockSpec((tm, d), lambda i: (i, 0))]
        + [const(a) for a in (wug, wuu, cwg, cwu, cbg, cbu, wd, lnw, lnb)],
        out_specs=pl.BlockSpec((tm, d), lambda i: (i, 0)),
        out_shape=jax.ShapeDtypeStruct((m, d), F32),
        scratch_shapes=[pltpu.VMEM((SUBLANES, dff), F32), pltpu.VMEM((SUBLANES, dff), F32),
                        pltpu.VMEM((tm, dff), BF16)],
        compiler_params=pltpu.CompilerParams(
            dimension_semantics=("arbitrary",), vmem_limit_bytes=FFN_VMEM_LIMIT),
        name="conv_ffn_ln",
    )(h1, wug, wuu, cwg, cwu, cbg, cbu, wd, lnw, lnb)


def _tile_sizes(batch, seq):
    return dict(tm_proj=min(256, seq), tq=min(256, seq), chunk=64, tl_rwkv=min(256, seq),
                nb_rwkv=2 if batch % 2 == 0 else 1, tm_merge=min(256, seq), tm_ffn=min(512, seq))


def _layer(h, w_in, idx_w, idx_b, mu, w0, w_up, a0, a_up, g_up, k_k, k_a, r_k, gn_w, gn_b,
           w_ba, w_br, w_out, ln1_w, ln1_b, w_up_ffn, conv_w, conv_b, w_down, ln2_w, ln2_b, alpha):
    b, seq, d = h.shape
    ts = _tile_sizes(b, seq)
    aw = HEADS * HEAD_DIM
    kvw = KV_HEADS * HEAD_DIM
    o = np.cumsum([0, aw, kvw, kvw, aw, HEAD_DIM, HEADS, 3 * aw + LORA_W, d, d])
    col = lambda i: w_in[:, o[i]:o[i + 1]]
    head = lambda w, hh: w[:, hh * HEAD_DIM:(hh + 1) * HEAD_DIM]
    zero = jnp.zeros((d, HEAD_DIM), w_in.dtype)
    k_cols = [blk for hh in range(KV_HEADS)
              for blk in (head(col(1), hh), zero, zero, head(col(1), hh))]
    v_cols = [head(col(2), hh) for hh in range(KV_HEADS) for _ in range(2)]
    wi_pad = jnp.pad(col(5), ((0, 0), (0, LANES - HEADS)))
    wa = jnp.concatenate([col(0)] + k_cols + v_cols + [col(3), col(4), col(4), wi_pad],
                         axis=1).astype(BF16)
    wr = col(6).astype(BF16)
    row2 = lambda v: v.reshape(1, -1)
    lnw2 = row2(jnp.concatenate([idx_w, idx_w]))
    lnb2 = row2(jnp.concatenate([idx_b, idx_b]))

    x2 = h.reshape(b * seq, d)
    q, k4, v2, qi, ki2, wi, pr = _inproj(x2, wa, wr, lnw2, lnb2, ts["tm_proj"], seq)
    r3 = lambda a: a.reshape(b, seq, a.shape[-1])

    y_attn = _attention(r3(q), r3(qi), r3(wi), r3(k4), r3(v2), r3(ki2), ts["tq"])

    ld, la = w_up.shape[0], a_up.shape[0]
    pad_rows = lambda w, start: jnp.pad(w, ((start, LORA_W - start - w.shape[0]), (0, 0)))
    vecs = [row2(v) for v in (mu, w0, a0, k_k, k_a, r_k.reshape(-1), gn_w, gn_b)]
    mats = [pad_rows(w_up, 0), pad_rows(a_up, ld), pad_rows(g_up, ld + la)]
    y_rwkv = _rwkv(r3(pr), vecs, mats, ts["chunk"], ts["tl_rwkv"], ts["nb_rwkv"])

    h1 = _merge(x2, y_attn.reshape(b * seq, aw), y_rwkv.reshape(b * seq, aw),
                col(7).astype(BF16), col(8).astype(BF16),
                w_ba.astype(BF16), w_br.astype(BF16), w_out.astype(BF16),
                row2(ln1_w), row2(ln1_b), ts["tm_merge"], alpha)

    dff = w_down.shape[0]
    out = _ffn(h1, w_up_ffn[:, :dff].astype(BF16), w_up_ffn[:, dff:].astype(BF16),
               conv_w[:, :dff], conv_w[:, dff:], row2(conv_b[:dff]), row2(conv_b[dff:]),
               w_down.astype(BF16), row2(ln2_w), row2(ln2_b), ts["tm_ffn"], seq, alpha)
    return out.reshape(b, seq, d)


def kernel(x, w_in, idx_k_norm_w, idx_k_norm_b, rwkv_mu, rwkv_w0, rwkv_w_up, rwkv_a0, rwkv_a_up,
           rwkv_g_up, rwkv_k_k, rwkv_k_a, rwkv_r_k, rwkv_ln_w, rwkv_ln_b, w_branch_attn,
           w_branch_rwkv, w_out, ln1_w, ln1_b, w_up, conv_w, conv_b, w_down, ln2_w, ln2_b):
    depth = w_in.shape[0]
    alpha = (2.0 * depth) ** 0.25
    h = x
    for l in range(depth):
        h = _layer(h, w_in[l], idx_k_norm_w[l], idx_k_norm_b[l], rwkv_mu[l], rwkv_w0[l],
                   rwkv_w_up[l], rwkv_a0[l], rwkv_a_up[l], rwkv_g_up[l], rwkv_k_k[l], rwkv_k_a[l],
                   rwkv_r_k[l], rwkv_ln_w[l], rwkv_ln_b[l], w_branch_attn[l], w_branch_rwkv[l],
                   w_out[l], ln1_w[l], ln1_b[l], w_up[l], conv_w[l], conv_b[l], w_down[l],
                   ln2_w[l], ln2_b[l], alpha)
    return h
```

```python
import functools

import jax
import jax.numpy as jnp
import ml_dtypes
import numpy as np
from jax import lax
from jax.experimental import pallas as pl
from jax.experimental.pallas import tpu as pltpu

F32 = jnp.float32
BF16 = jnp.bfloat16
I32 = jnp.int32

LANES = 128
HEAD_DIM = 64
HEADS = 8
KV_HEADS = 2
TOPK_MAX = 256
LORA_W = 128
LN_EPS = 1e-5
GN_EPS = 64e-5
INT_MIN = -(2 ** 31)
ATTN_CLASSES = 4
BISECT_PASSES = 20
BISECT_ROWS = 64
BISECT_UNROLL = 4
VMEM_LIMIT = 48 * 1024 * 1024


def _nt(a, b):
    return lax.dot_general(a, b, (((1,), (1,)), ((), ())), preferred_element_type=F32)


def _nn(a, b):
    return lax.dot_general(a, b, (((1,), (0,)), ((), ())), preferred_element_type=F32)


def _bnt(a, b):
    return lax.dot_general(a, b, (((2,), (2,)), ((0,), (0,))), preferred_element_type=F32)


def _bnn(a, b):
    return lax.dot_general(a, b, (((2,), (1,)), ((0,), (0,))), preferred_element_type=F32)


def _split2(x):
    hi = x.astype(BF16)
    lo = (x - hi.astype(F32)).astype(BF16)
    return hi, lo


def _mm3(a, b, dot=_nn):
    ah, al = _split2(a)
    bh, bl = _split2(b)
    return dot(ah, bh) + (dot(ah, bl) + dot(al, bh))


def _mm_exact_lhs(a_bf16, b):
    b0 = b.astype(BF16)
    r1 = b - b0.astype(F32)
    b1 = r1.astype(BF16)
    b2 = (r1 - b1.astype(F32)).astype(BF16)
    return _nn(a_bf16, b0) + (_nn(a_bf16, b1) + _nn(a_bf16, b2))


LOG2E = 1.4426950408889634
POS_SPLIT = 16
SLOPE_TERMS = 3


def _inproj_kernel(x_ref, wa_ref, wr_ref, lnw_ref, lnb_ref,
                   q_ref, k4_ref, v2_ref, qi_ref, ki2_ref, wi_ref, pr_ref, *, seq):
    tm = x_ref.shape[0]
    xb = x_ref[...].astype(BF16)
    pa = _nn(xb, wa_ref[...])
    q_ref[...] = (pa[:, 0:512] * (HEAD_DIM ** -0.5 * LOG2E)).astype(BF16)
    pos = (pl.program_id(0) * tm) % seq + lax.broadcasted_iota(I32, (tm, 2 * LANES), 0)
    s_lo = pos & (POS_SPLIT - 1)
    ln = lax.broadcasted_iota(I32, (tm, 2 * LANES), 1)
    a = jnp.where(ln < LANES, ln - HEAD_DIM, ln - LANES)
    in_aug = (a >= 0) & (a < 2 * SLOPE_TERMS)
    aug = jnp.where(in_aug, jnp.where((a & 1) == 0, pos - s_lo, s_lo), 0).astype(F32)
    for g in range(KV_HEADS):
        sl = slice(512 + g * 2 * LANES, 512 + (g + 1) * 2 * LANES)
        k4_ref[:, g * 2 * LANES:(g + 1) * 2 * LANES] = (pa[:, sl] + aug).astype(BF16)
    v2_ref[...] = pa[:, 1024:1280].astype(BF16)
    qi_ref[...] = pa[:, 1280:1792].astype(BF16)
    ki = pa[:, 1792:1920]
    mu = jnp.mean(ki, axis=-1, keepdims=True)
    var = jnp.mean(jnp.square(ki - mu), axis=-1, keepdims=True)
    ki2_ref[...] = ((ki - mu) * lax.rsqrt(var + LN_EPS) * lnw_ref[...] + lnb_ref[...]).astype(BF16)
    wi_ref[...] = pa[:, 1920:2048]
    pr_ref[...] = _nn(xb, wr_ref[...])


def _inproj(x2, wa, wr, lnw2, lnb2, tm, seq):
    m, d = x2.shape
    assert seq <= 256 * POS_SPLIT
    full = lambda a: pl.BlockSpec(a.shape, lambda i: (0,) * a.ndim)
    row = lambda n: pl.BlockSpec((tm, n), lambda i: (i, 0))
    outs = [(512, BF16), (512, BF16), (256, BF16), (512, BF16), (128, BF16), (128, F32),
            (wr.shape[1], F32)]
    return pl.pallas_call(
        functools.partial(_inproj_kernel, seq=seq),
        grid=(m // tm,),
        in_specs=[row(d), full(wa), full(wr), full(lnw2), full(lnb2)],
        out_specs=[row(n) for n, _ in outs],
        out_shape=[jax.ShapeDtypeStruct((m, n), dt) for n, dt in outs],
        compiler_params=pltpu.CompilerParams(
            dimension_semantics=("parallel",), vmem_limit_bytes=VMEM_LIMIT),
        name="inproj",
    )(x2, wa, wr, lnw2, lnb2)


def _bf16_parts(x):
    parts = []
    for _ in range(SLOPE_TERMS):
        parts.append(float(ml_dtypes.bfloat16(x)))
        x -= parts[-1]
    return parts


def _attn_body(q_ref, qi_ref, wi_ref, k4_ref, v2_ref, ki2_ref, o_ref,
               key_ref, bias_ref, *, t0, t0_max, tq, width, n_sel, cw):
    lane = lax.broadcasted_iota(I32, (tq, LANES), 1)
    lo = lane < HEAD_DIM
    n_cc = width // cw
    n_lc = width // LANES

    def head_lhs(ref, h, fill=None):
        pair = ref[0, :, (h // 2) * LANES:(h // 2 + 1) * LANES]
        fill = jnp.zeros_like(pair) if fill is None else fill
        return jnp.where(lo if h % 2 == 0 else jnp.logical_not(lo), pair, fill)

    wi = wi_ref[0] * (HEAD_DIM ** -0.5 * HEADS ** -0.5)
    row_c = t0 + lax.broadcasted_iota(I32, (tq, cw), 0)
    col_c = lax.broadcasted_iota(I32, (tq, cw), 1)
    for c in range(n_cc):
        kc = ki2_ref[0, c * cw:(c + 1) * cw, :]
        s = jnp.zeros((tq, cw), F32)
        for h in range(HEADS):
            s = s + jnp.maximum(_nt(head_lhs(qi_ref, h), kc), 0.0) * wi[:, h:h + 1]
        key_ref[:, c * cw:(c + 1) * cw] = jnp.where(col_c + c * cw <= row_c, s, -jnp.inf)

    kf = jnp.float32(n_sel)
    rb = BISECT_ROWS
    n_rb = tq // rb
    blocks = [slice(b * rb, (b + 1) * rb) for b in range(n_rb)]
    sls = [slice(c * LANES, (c + 1) * LANES) for c in range(n_lc)]
    n_live = {blk.start: min(n_lc, -(-(t0_max + blk.stop) // LANES)) for blk in blocks}
    key_lowest = jnp.int32(INT_MIN + 2 ** 23)
    key_highest = jnp.int32(0x7F800000)

    def as_float(key):
        key = jnp.clip(key, key_lowest, key_highest)
        bits = jnp.where(key < 0, key ^ jnp.int32(0x7FFFFFFF), key)
        return lax.bitcast_convert_type(bits, F32)

    def count(pred_fn, rows, read=lambda ref, rows, sl: ref[rows, sl], ref=key_ref):
        acc = jnp.zeros((rb, LANES), F32)
        for sl in sls[:n_live[rows.start]]:
            acc = acc + jnp.where(pred_fn(read(ref, rows, sl)), 1.0, 0.0)
        return jnp.broadcast_to(jnp.sum(acc, axis=1, keepdims=True), (rb, LANES))

    def bis_body(i, ts):
        inc = lax.shift_left(jnp.int32(1), jnp.int32(31) - i)
        out = []
        for rows, t in zip(blocks, ts):
            cand = t + inc
            cand_f = as_float(cand)
            out.append(jnp.where(count(lambda sc: sc >= cand_f, rows) >= kf, cand, t))
        return tuple(out)

    ts = lax.fori_loop(0, BISECT_PASSES, bis_body,
                       tuple(jnp.full((rb, LANES), INT_MIN, I32) for _ in blocks),
                       unroll=BISECT_UNROLL)
    lows = [as_float(t) for t in ts]

    def smallest(pred_fn, rows):
        acc = jnp.full((rb, LANES), jnp.inf, F32)
        for sl in sls[:n_live[rows.start]]:
            sc = key_ref[rows, sl]
            acc = jnp.minimum(acc, jnp.where(pred_fn(sc), sc, jnp.inf))
        return jnp.broadcast_to(jnp.min(acc, axis=1, keepdims=True), (rb, LANES))

    def settle(m, thr, done, rows):
        newly = jnp.where(count(lambda sc: sc > m, rows) < kf, 1.0 - done, 0.0)
        return jnp.where(newly > 0.0, m, thr), jnp.maximum(done, newly)

    walk = []
    for rows, low in zip(blocks, lows):
        m = smallest(lambda sc: sc >= low, rows)
        walk.append((m,) + settle(m, low, jnp.zeros((rb, LANES), F32), rows))

    def walk_pending(state):
        return functools.reduce(jnp.minimum, [jnp.min(done) for _, _, done in state]) < 1.0

    def walk_step(state):
        out = []
        for rows, (m, thr, done) in zip(blocks, state):
            m = smallest(lambda sc: sc > m, rows)
            out.append((m,) + settle(m, thr, done, rows))
        return tuple(out)

    thrs = [thr for _, thr, _ in lax.while_loop(walk_pending, walk_step, tuple(walk))]

    excess = [count(lambda sc: sc >= t, rows) - kf for rows, t in zip(blocks, thrs)]
    has_ties = functools.reduce(jnp.maximum, [jnp.max(e) for e in excess]) > 0.0

    @pl.when(jnp.logical_not(has_ties))
    def _():
        for rows, t in zip(blocks, thrs):
            for sl in sls:
                bias_ref[rows, sl] = jnp.where(key_ref[rows, sl] >= t, 0.0, -jnp.inf)

    @pl.when(has_ties)
    def _():
        lane_rb = lax.broadcasted_iota(I32, (rb, LANES), 1)
        read_pos = lambda ref, rows, sl: lax.bitcast_convert_type(ref[rows, sl], I32)
        nbits = int(np.log2(width - 1)) + 1
        for rows, t in zip(blocks, thrs):
            need = kf - count(lambda kc: kc > t, rows)
            for c, sl in enumerate(sls):
                tpos = jnp.where(key_ref[rows, sl] == t, lane_rb + c * LANES, jnp.int32(2 ** 30))
                bias_ref[rows, sl] = lax.bitcast_convert_type(tpos, F32)

            def tie_body(i, j, rows=rows, need=need):
                cand = j + lax.shift_left(jnp.int32(1), jnp.int32(nbits - 1) - i)
                below = count(lambda pc: pc < cand, rows, read_pos, bias_ref)
                return jnp.where(below < need, cand, j)

            jmax = lax.fori_loop(0, nbits, tie_body, jnp.zeros((rb, LANES), I32))
            for sl in sls:
                keep = jnp.where(read_pos(bias_ref, rows, sl) <= jmax, 0.0, -jnp.inf)
                bias_ref[rows, sl] = jnp.where(key_ref[rows, sl] > t, 0.0, keep)

    bias = bias_ref[:, 0:width]
    a_idx = jnp.where(lo, lane, lane - HEAD_DIM)

    def qk(h):
        parts = _bf16_parts((2.0 ** -(h + 1)) * LOG2E)
        fill = jnp.zeros((tq, LANES), F32)
        for n in reversed(range(SLOPE_TERMS)):
            fill = jnp.where(a_idx < 2 * (n + 1), parts[n], fill)
        kcol = (2 * (h // (HEADS // KV_HEADS)) + h % 2) * LANES
        return _nt(head_lhs(q_ref, h, fill.astype(BF16)), k4_ref[0, 0:width, kcol:kcol + LANES])

    even = None
    qk_next = qk(0)
    for h in range(HEADS):
        g = h // (HEADS // KV_HEADS)
        qk_cur, qk_next = qk_next, (qk(h + 1) if h + 1 < HEADS else None)
        logit = qk_cur + bias
        m = jnp.max(logit, axis=1, keepdims=True)
        e = jnp.exp2(logit - m)
        ssum = jnp.sum(e, axis=1, keepdims=True)
        out = _nn(e.astype(BF16), v2_ref[0, 0:width, g * LANES:(g + 1) * LANES]) / ssum
        if h % 2 == 0:
            even = out
        else:
            p = h // 2
            o_ref[0, :, p * LANES:(p + 1) * LANES] = jnp.where(lo, even, out).astype(o_ref.dtype)


def _attn_kernel(q_ref, qi_ref, wi_ref, k4_ref, v2_ref, ki2_ref, o_ref,
                 key_ref, bias_ref, *, tq, seq, n_sel, n_classes):
    i = pl.program_id(0)
    tiles_per_class = (seq // tq) // n_classes
    for cls in range(n_classes):
        width = (cls + 1) * tiles_per_class * tq

        @pl.when((i >= cls * tiles_per_class) & (i < (cls + 1) * tiles_per_class))
        def _():
            _attn_body(q_ref, qi_ref, wi_ref, k4_ref, v2_ref, ki2_ref, o_ref, key_ref, bias_ref,
                       t0=i * tq, t0_max=width - tq, tq=tq, width=width, n_sel=n_sel,
                       cw=512 if width % 512 == 0 else 256)


def _attention(q, qi, wi, k4, v2, ki2, tq):
    b, seq, _ = q.shape
    n_sel = min(TOPK_MAX, seq // 4)
    n_classes = min(ATTN_CLASSES, seq // tq)
    qblk = lambda n: pl.BlockSpec((1, tq, n), lambda i, bi: (bi, i, 0))
    kblk = lambda n: pl.BlockSpec((1, seq, n), lambda i, bi: (bi, 0, 0))
    return pl.pallas_call(
        functools.partial(_attn_kernel, tq=tq, seq=seq, n_sel=n_sel, n_classes=n_classes),
        grid=(seq // tq, b),
        in_specs=[qblk(512), qblk(512), qblk(128), kblk(512), kblk(256), kblk(128)],
        out_specs=qblk(512),
        out_shape=jax.ShapeDtypeStruct((b, seq, 512), BF16),
        scratch_shapes=[pltpu.VMEM((tq, seq), F32), pltpu.VMEM((tq, seq), F32)],
        compiler_params=pltpu.CompilerParams(
            dimension_semantics=("parallel", "parallel"), vmem_limit_bytes=VMEM_LIMIT),
        name="dsa_attention",
    )(q, qi, wi, k4, v2, ki2)


def _bf(x):
    return x.astype(BF16)


def _rwkv_kernel(pr_ref, mu_ref, w0_ref, wup_ref, a0_ref, aup_ref, gup_ref, kk_ref, ka_ref,
                 rk_ref, lnw_ref, lnb_ref, o_ref,
                 state_ref, prev_ref, tinv_ref, w_ref, atrt_ref, bhm_ref, y_ref, *, c):
    nb, tl, width = o_ref.shape
    n_pairs = width // LANES
    n_chunks = tl // c
    rows2 = 2 * c
    n_bp = nb * n_pairs

    @pl.when(pl.program_id(1) == 0)
    def _():
        state_ref[...] = jnp.zeros_like(state_ref)
        prev_ref[...] = jnp.zeros_like(prev_ref)

    ti = lax.broadcasted_iota(I32, (tl, tl), 0)
    tj = lax.broadcasted_iota(I32, (tl, tl), 1)
    same_chunk = (ti // c) == (tj // c)
    tri = jnp.where(same_chunk & (ti >= tj), 1.0, 0.0).astype(BF16)
    blk = jnp.where(same_chunk, 1.0, 0.0).astype(BF16)
    li = lax.broadcasted_iota(I32, (LANES, LANES), 0)
    lj = lax.broadcasted_iota(I32, (LANES, LANES), 1)
    ones_bd = jnp.where((li // HEAD_DIM) == (lj // HEAD_DIM), 1.0, 0.0).astype(BF16)
    upper_strict = li < lj
    upper_incl = li <= lj
    eye = jnp.where(li == lj, 1.0, 0.0)
    lo = lax.broadcasted_iota(I32, (c, LANES), 1) < HEAD_DIM
    rows = lax.broadcasted_iota(I32, (tl, pr_ref.shape[-1]), 0)

    def head_sum(x):
        return _nn(_bf(x), ones_bd)

    def stack(x):
        return _bf(jnp.concatenate([jnp.where(lo, x, 0.0), jnp.where(lo, 0.0, x)], axis=0))

    n_pc = n_chunks * n_bp
    atrt_l, btkt_l, vst_l, kh_l, bh_l = ([None] * n_pc for _ in range(5))
    kept = []
    for bi in range(nb):
        p = pr_ref[bi]
        shifted = jnp.where(rows == 0, prev_ref[bi, 7:8, :], pltpu.roll(p, 1, 0))
        prev_ref[bi] = p[tl - 8:tl, :]
        ps = p + (shifted - p) * mu_ref[...]
        r = ps[:, 0:width]
        k = ps[:, width:2 * width]
        v = ps[:, 2 * width:3 * width]
        lora = ps[:, 3 * width:3 * width + LORA_W]
        nz = -(w0_ref[...] + _nn(_bf(jnp.tanh(lora)), wup_ref[...]))
        log_w = -(jnp.maximum(nz, 0.0) + jnp.log(1.0 + jnp.exp(-jnp.abs(nz)))) - 0.5
        lw = -jnp.exp(log_w)
        a_sig = jax.nn.sigmoid(a0_ref[...] + _nn(_bf(lora), aup_ref[...]))
        gate = _nn(_bf(jax.nn.sigmoid(lora)), gup_ref[...])
        lw_hi = _bf(lw)
        lw_lo = _bf(lw - lw_hi.astype(F32))
        cum = _nn(tri, lw_hi) + _nn(tri, lw_lo)
        tot = _nn(blk, lw_hi) + _nn(blk, lw_lo)
        kept.append((r, k, v, a_sig, gate, tot))
        for pi in range(n_pairs):
            sl = slice(pi * LANES, (pi + 1) * LANES)
            r_p, k_p, v_p, a_p = r[:, sl], k[:, sl], v[:, sl], a_sig[:, sl]
            kk = k_p * kk_ref[:, sl]
            kk = kk * lax.rsqrt(jnp.maximum(head_sum(kk * kk), 1e-24))
            kmod = k_p * (1.0 + (a_p - 1.0) * ka_ref[:, sl])
            bvec = kk * a_p
            cum_p, tot_p = cum[:, sl], tot[:, sl]
            p_inv = jnp.exp(-cum_p)
            p_end = jnp.exp(tot_p - cum_p)
            at_all = -kk * jnp.exp(cum_p - lw[:, sl])
            rt_all = r_p * jnp.exp(cum_p)
            bt_all, kt_all = bvec * p_inv, kmod * p_inv
            bh_all, kh_all = bvec * p_end, kmod * p_end
            for ci in range(n_chunks):
                rs = slice(ci * c, (ci + 1) * c)
                idx = (ci * nb + bi) * n_pairs + pi
                atrt_l[idx] = jnp.concatenate([stack(at_all[rs]), stack(rt_all[rs])], axis=0)
                btkt_l[idx] = jnp.concatenate([stack(bt_all[rs]), stack(kt_all[rs])], axis=0)
                vst_l[idx] = _bf(jnp.concatenate([jnp.where(lo, v_p[rs], 0.0),
                                                  jnp.where(lo, 0.0, v_p[rs])], axis=0).T)
                kh_l[idx] = stack(kh_all[rs])
                bh_l[idx] = stack(bh_all[rs])

    atrt = jnp.stack(atrt_l)
    gt = _bnt(jnp.stack(btkt_l), atrt)
    n_t = jnp.where(upper_strict, gt[:, 0:rows2, 0:rows2], 0.0)
    m_rb_t = jnp.where(upper_incl, gt[:, 0:rows2, rows2:], 0.0)
    m_ak_t = jnp.where(upper_strict, gt[:, rows2:, 0:rows2], 0.0)
    m_rk_t = jnp.where(upper_incl, gt[:, rows2:, rows2:], 0.0)
    tinv = eye + n_t
    npow = _bf(n_t)
    for _ in range(int(np.log2(c)) - 1):
        npow = _bf(_bnn(npow, npow))
        tinv = tinv + _bnn(_bf(tinv), npow)
    rhs = jnp.concatenate([_bf(m_ak_t), jnp.stack(kh_l), _bf(m_rk_t)], axis=2)
    w_ref[...] = _bnn(jnp.stack(vst_l), rhs)
    tinv_ref[...] = _bf(tinv)
    atrt_ref[...] = atrt
    bhm_ref[...] = jnp.concatenate([jnp.stack(bh_l), _bf(m_rb_t)], axis=2)

    for ci in range(n_chunks):
        rs = slice(ci * c, (ci + 1) * c)
        ids = slice(ci * n_bp, (ci + 1) * n_bp)
        state = state_ref[...]
        x = _bnt(_bf(state), atrt_ref[ids])
        w = w_ref[ids]
        us_t = _bnn(_bf(x[:, :, 0:rows2] + w[:, :, 0:rows2]), tinv_ref[ids])
        z = _bnn(_bf(us_t), bhm_ref[ids])
        decay_c = jnp.exp(jnp.stack([kept[bi][5][ci * c:ci * c + 1, pi * LANES:(pi + 1) * LANES]
                                     for bi in range(nb) for pi in range(n_pairs)]))
        state_ref[...] = state * decay_c + z[:, :, 0:rows2] + w[:, :, rows2:2 * rows2]
        ys_t = x[:, :, rows2:] + z[:, :, rows2:] + w[:, :, 2 * rows2:]
        for bi in range(nb):
            for pi in range(n_pairs):
                ys = ys_t[bi * n_pairs + pi].T
                y_ref[bi, rs, pi * LANES:(pi + 1) * LANES] = ys[0:c, :] + ys[c:rows2, :]

    for bi in range(nb):
        r, k, v, a_sig, gate, _ = kept[bi]
        for pi in range(n_pairs):
            sl = slice(pi * LANES, (pi + 1) * LANES)
            y = y_ref[bi, :, sl]
            mean = head_sum(y) * (1.0 / HEAD_DIM)
            yc = y - mean
            var = head_sum(yc * yc) * (1.0 / HEAD_DIM)
            yn = yc * lax.rsqrt(var + GN_EPS) * lnw_ref[:, sl] + lnb_ref[:, sl]
            kmod = k[:, sl] * (1.0 + (a_sig[:, sl] - 1.0) * ka_ref[:, sl])
            bonus = head_sum(r[:, sl] * kmod * rk_ref[:, sl]) * v[:, sl]
            o_ref[bi, :, sl] = ((yn + bonus) * gate[:, sl]).astype(o_ref.dtype)


def _rwkv(pr, vecs, mats, c, tl, nb):
    b, seq, pw = pr.shape
    width = (pw - LORA_W) // 3
    n_pairs = width // LANES
    n_pc = (tl // c) * nb * n_pairs
    full = lambda a: pl.BlockSpec(a.shape, lambda bi, i: (0,) * a.ndim)
    mu, w0, a0, kk, ka, rk, lnw, lnb = vecs
    wup, aup, gup = [m.astype(BF16) for m in mats]
    return pl.pallas_call(
        functools.partial(_rwkv_kernel, c=c),
        grid=(b // nb, seq // tl),
        in_specs=[pl.BlockSpec((nb, tl, pw), lambda bi, i: (bi, i, 0)),
                  full(mu), full(w0), full(wup), full(a0), full(aup), full(gup),
                  full(kk), full(ka), full(rk), full(lnw), full(lnb)],
        out_specs=pl.BlockSpec((nb, tl, width), lambda bi, i: (bi, i, 0)),
        out_shape=jax.ShapeDtypeStruct((b, seq, width), BF16),
        scratch_shapes=[pltpu.VMEM((nb * n_pairs, LANES, LANES), F32),
                        pltpu.VMEM((nb, 8, pw), F32),
                        pltpu.VMEM((n_pc, 2 * c, 2 * c), BF16),
                        pltpu.VMEM((n_pc, 2 * c, 6 * c), F32),
                        pltpu.VMEM((n_pc, 4 * c, LANES), BF16),
                        pltpu.VMEM((n_pc, 2 * c, 4 * c), BF16),
                        pltpu.VMEM((nb, tl, width), F32)],
        compiler_params=pltpu.CompilerParams(
            dimension_semantics=("parallel", "arbitrary"), vmem_limit_bytes=VMEM_LIMIT),
        name="rwkv7",
    )(pr, mu, w0, wup, a0, aup, gup, kk, ka, rk, lnw, lnb)


def _layer_norm(z, w, b):
    mu = jnp.mean(z, axis=-1, keepdims=True)
    zc = z - mu
    var = jnp.mean(zc * zc, axis=-1, keepdims=True)
    return zc * lax.rsqrt(var + LN_EPS) * w + b


def _merge_kernel(x_ref, ya_ref, yr_ref, wg_ref, wa_ref, wb_ref, wo_ref, lnw_ref, lnb_ref,
                  o_ref, *, alpha):
    x = x_ref[...]
    d = x.shape[-1]
    xb = x.astype(BF16)
    mix = (jax.nn.sigmoid(_nn(xb, wg_ref[:, 0:d])) * _nn(ya_ref[...], wa_ref[...])
           + jax.nn.sigmoid(_nn(xb, wg_ref[:, d:2 * d])) * _nn(yr_ref[...], wb_ref[...]))
    z = alpha * x + _nn(mix.astype(BF16), wo_ref[...])
    o_ref[...] = _layer_norm(z, lnw_ref[...], lnb_ref[...])


def _merge(x2, ya, yr, wg, wa, wb, wo, lnw, lnb, tm, alpha):
    m, d = x2.shape
    full = lambda a: pl.BlockSpec(a.shape, lambda i: (0,) * a.ndim)
    row = lambda n: pl.BlockSpec((tm, n), lambda i: (i, 0))
    return pl.pallas_call(
        functools.partial(_merge_kernel, alpha=alpha),
        grid=(m // tm,),
        in_specs=[row(d), row(ya.shape[1]), row(yr.shape[1]),
                  full(wg), full(wa), full(wb), full(wo), full(lnw), full(lnb)],
        out_specs=row(d),
        out_shape=jax.ShapeDtypeStruct((m, d), F32),
        compiler_params=pltpu.CompilerParams(
            dimension_semantics=("parallel",), vmem_limit_bytes=VMEM_LIMIT),
        name="merge_out_ln",
    )(x2, ya, yr, wg, wa, wb, wo, lnw, lnb)


SUBLANES = 8
FFN_COLS = 256
FFN_VMEM_LIMIT = 56 * 1024 * 1024


def _ffn_kernel(h_ref, wu_ref, cw_ref, cb_ref, wd_ref, lnw_ref, lnb_ref, o_ref,
                prev_ref, act_ref, *, alpha, tiles_per_seq):
    tm = h_ref.shape[0]
    dff = wd_ref.shape[0]
    h = h_ref[...]
    hb = h.astype(BF16)

    @pl.when(pl.program_id(0) % tiles_per_seq == 0)
    def _():
        prev_ref[...] = jnp.zeros_like(prev_ref)

    def conv_proj(cols):
        u = _nn(hb, wu_ref[:, cols])
        ext = jnp.concatenate([prev_ref[:, cols], u], axis=0)
        prev_ref[:, cols] = u[tm - SUBLANES:, :]
        w = cw_ref[:, cols]
        back1 = pltpu.roll(ext, 1, 0)[SUBLANES:]
        back2 = pltpu.roll(ext, 2, 0)[SUBLANES:]
        return w[0:1] * back2 + w[1:2] * back1 + w[2:3] * u + cb_ref[:, cols]

    for c in range(dff // FFN_COLS):
        gate = conv_proj(slice(c * FFN_COLS, (c + 1) * FFN_COLS))
        up = conv_proj(slice(dff + c * FFN_COLS, dff + (c + 1) * FFN_COLS))
        act_ref[:, c * FFN_COLS:(c + 1) * FFN_COLS] = (gate * jax.nn.sigmoid(gate) * up).astype(BF16)
    o_ref[...] = _layer_norm(alpha * h + _nn(act_ref[...], wd_ref[...]), lnw_ref[...], lnb_ref[...])


def _ffn(h1, wu, cw, cb, wd, lnw, lnb, tm, seq, alpha):
    m, d = h1.shape
    dff = wd.shape[0]
    assert dff % FFN_COLS == 0 and seq % tm == 0
    const = lambda a: pl.BlockSpec(a.shape, lambda i: (0,) * a.ndim, pipeline_mode=pl.Buffered(1))
    return pl.pallas_call(
        functools.partial(_ffn_kernel, alpha=alpha, tiles_per_seq=seq // tm),
        grid=(m // tm,),
        in_specs=[pl.BlockSpec((tm, d), lambda i: (i, 0))]
        + [const(a) for a in (wu, cw, cb, wd, lnw, lnb)],
        out_specs=pl.BlockSpec((tm, d), lambda i: (i, 0)),
        out_shape=jax.ShapeDtypeStruct((m, d), F32),
        scratch_shapes=[pltpu.VMEM((SUBLANES, 2 * dff), F32), pltpu.VMEM((tm, dff), BF16)],
        compiler_params=pltpu.CompilerParams(
            dimension_semantics=("arbitrary",), vmem_limit_bytes=FFN_VMEM_LIMIT),
        name="conv_ffn_ln",
    )(h1, wu, cw, cb, wd, lnw, lnb)


def _tile_sizes(batch, seq):
    return dict(tm_proj=min(256, seq), tq=min(256, seq), chunk=64, tl_rwkv=min(256, seq),
                nb_rwkv=2 if batch % 2 == 0 else 1, tm_merge=min(256, seq), tm_ffn=min(512, seq))


def _layer(h, w_in, idx_w, idx_b, mu, w0, w_up, a0, a_up, g_up, k_k, k_a, r_k, gn_w, gn_b,
           w_ba, w_br, w_out, ln1_w, ln1_b, w_up_ffn, conv_w, conv_b, w_down, ln2_w, ln2_b, alpha):
    b, seq, d = h.shape
    ts = _tile_sizes(b, seq)
    aw = HEADS * HEAD_DIM
    kvw = KV_HEADS * HEAD_DIM
    o = np.cumsum([0, aw, kvw, kvw, aw, HEAD_DIM, HEADS, 3 * aw + LORA_W, d, d])
    col = lambda i: w_in[:, o[i]:o[i + 1]]
    head = lambda w, hh: w[:, hh * HEAD_DIM:(hh + 1) * HEAD_DIM]
    zero = jnp.zeros((d, HEAD_DIM), w_in.dtype)
    k_cols = [blk for hh in range(KV_HEADS)
              for blk in (head(col(1), hh), zero, zero, head(col(1), hh))]
    v_cols = [head(col(2), hh) for hh in range(KV_HEADS) for _ in range(2)]
    wi_pad = jnp.pad(col(5), ((0, 0), (0, LANES - HEADS)))
    wa = jnp.concatenate([col(0)] + k_cols + v_cols + [col(3), col(4), col(4), wi_pad],
                         axis=1).astype(BF16)
    wr = col(6).astype(BF16)
    row2 = lambda v: v.reshape(1, -1)
    lnw2 = row2(jnp.concatenate([idx_w, idx_w]))
    lnb2 = row2(jnp.concatenate([idx_b, idx_b]))

    x2 = h.reshape(b * seq, d)
    q, k4, v2, qi, ki2, wi, pr = _inproj(x2, wa, wr, lnw2, lnb2, ts["tm_proj"], seq)
    r3 = lambda a: a.reshape(b, seq, a.shape[-1])

    y_attn = _attention(r3(q), r3(qi), r3(wi), r3(k4), r3(v2), r3(ki2), ts["tq"])

    ld, la = w_up.shape[0], a_up.shape[0]
    pad_rows = lambda w, start: jnp.pad(w, ((start, LORA_W - start - w.shape[0]), (0, 0)))
    vecs = [row2(v) for v in (mu, w0, a0, k_k, k_a, r_k.reshape(-1), gn_w, gn_b)]
    mats = [pad_rows(w_up, 0), pad_rows(a_up, ld), pad_rows(g_up, ld + la)]
    y_rwkv = _rwkv(r3(pr), vecs, mats, ts["chunk"], ts["tl_rwkv"], ts["nb_rwkv"])

    h1 = _merge(x2, y_attn.reshape(b * seq, aw), y_rwkv.reshape(b * seq, aw),
                w_in[:, o[7]:o[9]].astype(BF16),
                w_ba.astype(BF16), w_br.astype(BF16), w_out.astype(BF16),
                row2(ln1_w), row2(ln1_b), ts["tm_merge"], alpha)

    out = _ffn(h1, w_up_ffn.astype(BF16), conv_w, row2(conv_b), w_down.astype(BF16),
               row2(ln2_w), row2(ln2_b), ts["tm_ffn"], seq, alpha)
    return out.reshape(b, seq, d)


def kernel(x, w_in, idx_k_norm_w, idx_k_norm_b, rwkv_mu, rwkv_w0, rwkv_w_up, rwkv_a0, rwkv_a_up,
           rwkv_g_up, rwkv_k_k, rwkv_k_a, rwkv_r_k, rwkv_ln_w, rwkv_ln_b, w_branch_attn,
           w_branch_rwkv, w_out, ln1_w, ln1_b, w_up, conv_w, conv_b, w_down, ln2_w, ln2_b):
    depth = w_in.shape[0]
    alpha = (2.0 * depth) ** 0.25
    h = x
    for l in range(depth):
        h = _layer(h, w_in[l], idx_k_norm_w[l], idx_k_norm_b[l], rwkv_mu[l], rwkv_w0[l],
                   rwkv_w_up[l], rwkv_a0[l], rwkv_a_up[l], rwkv_g_up[l], rwkv_k_k[l], rwkv_k_a[l],
                   rwkv_r_k[l], rwkv_ln_w[l], rwkv_ln_b[l], w_branch_attn[l], w_branch_rwkv[l],
                   w_out[l], ln1_w[l], ln1_b[l], w_up[l], conv_w[l], conv_b[l], w_down[l],
                   ln2_w[l], ln2_b[l], alpha)
    return h
```

```python
import functools

import jax
import jax.numpy as jnp
import ml_dtypes
import numpy as np
from jax import lax
from jax.experimental import pallas as pl
from jax.experimental.pallas import tpu as pltpu

F32 = jnp.float32
BF16 = jnp.bfloat16
I32 = jnp.int32

LANES = 128
HEAD_DIM = 64
HEADS = 8
KV_HEADS = 2
TOPK_MAX = 256
LORA_W = 128
LN_EPS = 1e-5
GN_EPS = 64e-5
INT_MIN = -(2 ** 31)
ATTN_CLASSES = 4
BISECT_PASSES = 20
BISECT_ROWS = 64
BISECT_UNROLL = 4
VMEM_LIMIT = 48 * 1024 * 1024


def _nt(a, b):
    return lax.dot_general(a, b, (((1,), (1,)), ((), ())), preferred_element_type=F32)


def _nn(a, b):
    return lax.dot_general(a, b, (((1,), (0,)), ((), ())), preferred_element_type=F32)


def _bnt(a, b):
    return lax.dot_general(a, b, (((2,), (2,)), ((0,), (0,))), preferred_element_type=F32)


def _bnn(a, b):
    return lax.dot_general(a, b, (((2,), (1,)), ((0,), (0,))), preferred_element_type=F32)


def _split2(x):
    hi = x.astype(BF16)
    lo = (x - hi.astype(F32)).astype(BF16)
    return hi, lo


def _mm3(a, b, dot=_nn):
    ah, al = _split2(a)
    bh, bl = _split2(b)
    return dot(ah, bh) + (dot(ah, bl) + dot(al, bh))


def _mm_exact_lhs(a_bf16, b):
    b0 = b.astype(BF16)
    r1 = b - b0.astype(F32)
    b1 = r1.astype(BF16)
    b2 = (r1 - b1.astype(F32)).astype(BF16)
    return _nn(a_bf16, b0) + (_nn(a_bf16, b1) + _nn(a_bf16, b2))


LOG2E = 1.4426950408889634
POS_SPLIT = 16
SLOPE_TERMS = 3


def _inproj_kernel(x_ref, wa_ref, wr_ref, lnw_ref, lnb_ref,
                   q_ref, k4_ref, vt_ref, qi_ref, ki2_ref, wi_ref, pr_ref, *, seq):
    tm = x_ref.shape[0]
    xb = x_ref[...].astype(BF16)
    pa = _nn(xb, wa_ref[...])
    q_ref[...] = (pa[:, 0:512] * (HEAD_DIM ** -0.5 * LOG2E)).astype(BF16)
    pos = (pl.program_id(0) * tm) % seq + lax.broadcasted_iota(I32, (tm, 2 * LANES), 0)
    s_lo = pos & (POS_SPLIT - 1)
    ln = lax.broadcasted_iota(I32, (tm, 2 * LANES), 1)
    a = jnp.where(ln < LANES, ln - HEAD_DIM, ln - LANES)
    in_aug = (a >= 0) & (a < 2 * SLOPE_TERMS)
    aug = jnp.where(in_aug, jnp.where((a & 1) == 0, pos - s_lo, s_lo), 0).astype(F32)
    for g in range(KV_HEADS):
        sl = slice(512 + g * 2 * LANES, 512 + (g + 1) * 2 * LANES)
        k4_ref[:, g * 2 * LANES:(g + 1) * 2 * LANES] = (pa[:, sl] + aug).astype(BF16)
    vt = pa[:, 1024:1152].T
    vt_swapped = pltpu.roll(vt, HEAD_DIM, 0)
    vrow = lax.broadcasted_iota(I32, vt.shape, 0)
    top = vrow < HEAD_DIM
    one_at = lambda r: jnp.where(vrow == r, 1.0, 0.0)
    for g, (v_top, v_bottom) in enumerate(((vt, vt_swapped), (vt_swapped, vt))):
        vt_ref[(2 * g) * LANES:(2 * g + 1) * LANES, :] = jnp.where(top, v_top, one_at(HEAD_DIM)).astype(BF16)
        vt_ref[(2 * g + 1) * LANES:(2 * g + 2) * LANES, :] = jnp.where(top, one_at(0), v_bottom).astype(BF16)
    qi_ref[...] = pa[:, 1152:1664].astype(BF16)
    ki = pa[:, 1664:1792]
    mu = jnp.mean(ki, axis=-1, keepdims=True)
    var = jnp.mean(jnp.square(ki - mu), axis=-1, keepdims=True)
    ki2_ref[...] = ((ki - mu) * lax.rsqrt(var + LN_EPS) * lnw_ref[...] + lnb_ref[...]).astype(BF16)
    wi_ref[...] = pa[:, 1792:1920]
    pr_ref[...] = _nn(xb, wr_ref[...])


def _inproj(x2, wa, wr, lnw2, lnb2, tm, seq):
    m, d = x2.shape
    assert seq <= 256 * POS_SPLIT
    full = lambda a: pl.BlockSpec(a.shape, lambda i: (0,) * a.ndim)
    row = lambda n: pl.BlockSpec((tm, n), lambda i: (i, 0))
    outs = [(512, BF16), (512, BF16), None, (512, BF16), (128, BF16), (128, F32),
            (wr.shape[1], F32)]
    vt_rows = 2 * KV_HEADS * LANES
    return pl.pallas_call(
        functools.partial(_inproj_kernel, seq=seq),
        grid=(m // tm,),
        in_specs=[row(d), full(wa), full(wr), full(lnw2), full(lnb2)],
        out_specs=[row(o[0]) if o else pl.BlockSpec((vt_rows, tm), lambda i: (0, i)) for o in outs],
        out_shape=[jax.ShapeDtypeStruct((m, o[0]), o[1]) if o
                   else jax.ShapeDtypeStruct((vt_rows, m), BF16) for o in outs],
        compiler_params=pltpu.CompilerParams(
            dimension_semantics=("parallel",), vmem_limit_bytes=VMEM_LIMIT),
        name="inproj",
    )(x2, wa, wr, lnw2, lnb2)


def _bf16_parts(x):
    parts = []
    for _ in range(SLOPE_TERMS):
        parts.append(float(ml_dtypes.bfloat16(x)))
        x -= parts[-1]
    return parts


def _attn_body(q_ref, qi_ref, wi_ref, k4_ref, vt_ref, ki2_ref, o_ref,
               key_ref, bias_ref, *, t0, t0_max, tq, width, n_sel, cw):
    lane = lax.broadcasted_iota(I32, (tq, LANES), 1)
    lo = lane < HEAD_DIM
    n_cc = width // cw
    n_lc = width // LANES

    def head_lhs(ref, h, fill=None):
        pair = ref[0, :, (h // 2) * LANES:(h // 2 + 1) * LANES]
        fill = jnp.zeros_like(pair) if fill is None else fill
        return jnp.where(lo if h % 2 == 0 else jnp.logical_not(lo), pair, fill)

    wi = wi_ref[0] * (HEAD_DIM ** -0.5 * HEADS ** -0.5)
    row_c = t0 + lax.broadcasted_iota(I32, (tq, cw), 0)
    col_c = lax.broadcasted_iota(I32, (tq, cw), 1)
    for c in range(n_cc):
        kc = ki2_ref[0, c * cw:(c + 1) * cw, :]
        s = jnp.zeros((tq, cw), F32)
        for h in range(HEADS):
            s = s + jnp.maximum(_nt(head_lhs(qi_ref, h), kc), 0.0) * wi[:, h:h + 1]
        key_ref[:, c * cw:(c + 1) * cw] = jnp.where(col_c + c * cw <= row_c, s, -jnp.inf)

    kf = jnp.float32(n_sel)
    rb = BISECT_ROWS
    n_rb = tq // rb
    blocks = [slice(b * rb, (b + 1) * rb) for b in range(n_rb)]
    sls = [slice(c * LANES, (c + 1) * LANES) for c in range(n_lc)]
    n_live = {blk.start: min(n_lc, -(-(t0_max + blk.stop) // LANES)) for blk in blocks}
    key_lowest = jnp.int32(INT_MIN + 2 ** 23)
    key_highest = jnp.int32(0x7F800000)

    def as_float(key):
        key = jnp.clip(key, key_lowest, key_highest)
        bits = jnp.where(key < 0, key ^ jnp.int32(0x7FFFFFFF), key)
        return lax.bitcast_convert_type(bits, F32)

    def count(pred_fn, rows, read=lambda ref, rows, sl: ref[rows, sl], ref=key_ref):
        acc = jnp.zeros((rb, LANES), F32)
        for sl in sls[:n_live[rows.start]]:
            acc = acc + jnp.where(pred_fn(read(ref, rows, sl)), 1.0, 0.0)
        return jnp.broadcast_to(jnp.sum(acc, axis=1, keepdims=True), (rb, LANES))

    def bis_body(i, ts):
        inc = lax.shift_left(jnp.int32(1), jnp.int32(31) - i)
        out = []
        for rows, t in zip(blocks, ts):
            cand = t + inc
            cand_f = as_float(cand)
            out.append(jnp.where(count(lambda sc: sc >= cand_f, rows) >= kf, cand, t))
        return tuple(out)

    ts = lax.fori_loop(0, BISECT_PASSES, bis_body,
                       tuple(jnp.full((rb, LANES), INT_MIN, I32) for _ in blocks),
                       unroll=BISECT_UNROLL)
    lows = [as_float(t) for t in ts]

    def smallest(pred_fn, rows):
        acc = jnp.full((rb, LANES), jnp.inf, F32)
        for sl in sls[:n_live[rows.start]]:
            sc = key_ref[rows, sl]
            acc = jnp.minimum(acc, jnp.where(pred_fn(sc), sc, jnp.inf))
        return jnp.broadcast_to(jnp.min(acc, axis=1, keepdims=True), (rb, LANES))

    def settle(m, thr, done, rows):
        newly = jnp.where(count(lambda sc: sc > m, rows) < kf, 1.0 - done, 0.0)
        return jnp.where(newly > 0.0, m, thr), jnp.maximum(done, newly)

    walk = []
    for rows, low in zip(blocks, lows):
        m = smallest(lambda sc: sc >= low, rows)
        walk.append((m,) + settle(m, low, jnp.zeros((rb, LANES), F32), rows))

    def walk_pending(state):
        return functools.reduce(jnp.minimum, [jnp.min(done) for _, _, done in state]) < 1.0

    def walk_step(state):
        out = []
        for rows, (m, thr, done) in zip(blocks, state):
            m = smallest(lambda sc: sc > m, rows)
            out.append((m,) + settle(m, thr, done, rows))
        return tuple(out)

    thrs = [thr for _, thr, _ in lax.while_loop(walk_pending, walk_step, tuple(walk))]

    excess = [count(lambda sc: sc >= t, rows) - kf for rows, t in zip(blocks, thrs)]
    has_ties = functools.reduce(jnp.maximum, [jnp.max(e) for e in excess]) > 0.0

    @pl.when(jnp.logical_not(has_ties))
    def _():
        for rows, t in zip(blocks, thrs):
            for sl in sls:
                bias_ref[rows, sl] = jnp.where(key_ref[rows, sl] >= t, 0.0, -jnp.inf)

    @pl.when(has_ties)
    def _():
        lane_rb = lax.broadcasted_iota(I32, (rb, LANES), 1)
        read_pos = lambda ref, rows, sl: lax.bitcast_convert_type(ref[rows, sl], I32)
        nbits = int(np.log2(width - 1)) + 1
        for rows, t in zip(blocks, thrs):
            need = kf - count(lambda kc: kc > t, rows)
            for c, sl in enumerate(sls):
                tpos = jnp.where(key_ref[rows, sl] == t, lane_rb + c * LANES, jnp.int32(2 ** 30))
                bias_ref[rows, sl] = lax.bitcast_convert_type(tpos, F32)

            def tie_body(i, j, rows=rows, need=need):
                cand = j + lax.shift_left(jnp.int32(1), jnp.int32(nbits - 1) - i)
                below = count(lambda pc: pc < cand, rows, read_pos, bias_ref)
                return jnp.where(below < need, cand, j)

            jmax = lax.fori_loop(0, nbits, tie_body, jnp.zeros((rb, LANES), I32))
            for sl in sls:
                keep = jnp.where(read_pos(bias_ref, rows, sl) <= jmax, 0.0, -jnp.inf)
                bias_ref[rows, sl] = jnp.where(key_ref[rows, sl] > t, 0.0, keep)

    bias = bias_ref[:, 0:width]
    a_idx = jnp.where(lo, lane, lane - HEAD_DIM)

    def qk(h):
        parts = _bf16_parts((2.0 ** -(h + 1)) * LOG2E)
        fill = jnp.zeros((tq, LANES), F32)
        for n in reversed(range(SLOPE_TERMS)):
            fill = jnp.where(a_idx < 2 * (n + 1), parts[n], fill)
        kcol = (2 * (h // (HEADS // KV_HEADS)) + h % 2) * LANES
        return _nt(head_lhs(q_ref, h, fill.astype(BF16)), k4_ref[0, 0:width, kcol:kcol + LANES])

    even = None
    qk_next = qk(0)
    for h in range(HEADS):
        g = h // (HEADS // KV_HEADS)
        qk_cur, qk_next = qk_next, (qk(h + 1) if h + 1 < HEADS else None)
        logit = qk_cur + bias
        m = jnp.max(logit, axis=1, keepdims=True)
        e = jnp.exp2(logit - m)
        vrows = (2 * g + h % 2) * LANES
        out = _nt(vt_ref[vrows:vrows + LANES, 0:width], e.astype(BF16)).T
        if h % 2 == 0:
            even = out
        else:
            p = h // 2
            denom = jnp.where(lo, jnp.broadcast_to(even[:, HEAD_DIM:HEAD_DIM + 1], (tq, LANES)),
                              jnp.broadcast_to(out[:, 0:1], (tq, LANES)))
            o_ref[0, :, p * LANES:(p + 1) * LANES] = (jnp.where(lo, even, out) / denom).astype(o_ref.dtype)


def _attn_kernel(q_ref, qi_ref, wi_ref, k4_ref, vt_ref, ki2_ref, o_ref,
                 key_ref, bias_ref, *, tq, seq, n_sel, n_classes):
    i = pl.program_id(0)
    tiles_per_class = (seq // tq) // n_classes
    for cls in range(n_classes):
        width = (cls + 1) * tiles_per_class * tq

        @pl.when((i >= cls * tiles_per_class) & (i < (cls + 1) * tiles_per_class))
        def _():
            _attn_body(q_ref, qi_ref, wi_ref, k4_ref, vt_ref, ki2_ref, o_ref, key_ref, bias_ref,
                       t0=i * tq, t0_max=width - tq, tq=tq, width=width, n_sel=n_sel,
                       cw=512 if width % 512 == 0 else 256)


def _attention(q, qi, wi, k4, vt, ki2, tq):
    b, seq, _ = q.shape
    n_sel = min(TOPK_MAX, seq // 4)
    n_classes = min(ATTN_CLASSES, seq // tq)
    qblk = lambda n: pl.BlockSpec((1, tq, n), lambda i, bi: (bi, i, 0))
    kblk = lambda n: pl.BlockSpec((1, seq, n), lambda i, bi: (bi, 0, 0))
    vblk = pl.BlockSpec((vt.shape[0], seq), lambda i, bi: (0, bi))
    return pl.pallas_call(
        functools.partial(_attn_kernel, tq=tq, seq=seq, n_sel=n_sel, n_classes=n_classes),
        grid=(seq // tq, b),
        in_specs=[qblk(512), qblk(512), qblk(128), kblk(512), vblk, kblk(128)],
        out_specs=qblk(512),
        out_shape=jax.ShapeDtypeStruct((b, seq, 512), BF16),
        scratch_shapes=[pltpu.VMEM((tq, seq), F32), pltpu.VMEM((tq, seq), F32)],
        compiler_params=pltpu.CompilerParams(
            dimension_semantics=("parallel", "parallel"), vmem_limit_bytes=VMEM_LIMIT),
        name="dsa_attention",
    )(q, qi, wi, k4, vt, ki2)


def _bf(x):
    return x.astype(BF16)


def _rwkv_kernel(pr_ref, mu_ref, w0_ref, wup_ref, a0_ref, aup_ref, gup_ref, kk_ref, ka_ref,
                 rk_ref, lnw_ref, lnb_ref, o_ref,
                 state_ref, prev_ref, tinv_ref, w_ref, atrt_ref, bhm_ref, y_ref, *, c):
    nb, tl, width = o_ref.shape
    n_pairs = width // LANES
    n_chunks = tl // c
    rows2 = 2 * c
    n_bp = nb * n_pairs

    @pl.when(pl.program_id(1) == 0)
    def _():
        state_ref[...] = jnp.zeros_like(state_ref)
        prev_ref[...] = jnp.zeros_like(prev_ref)

    ti = lax.broadcasted_iota(I32, (tl, tl), 0)
    tj = lax.broadcasted_iota(I32, (tl, tl), 1)
    same_chunk = (ti // c) == (tj // c)
    tri = jnp.where(same_chunk & (ti >= tj), 1.0, 0.0).astype(BF16)
    blk = jnp.where(same_chunk, 1.0, 0.0).astype(BF16)
    li = lax.broadcasted_iota(I32, (LANES, LANES), 0)
    lj = lax.broadcasted_iota(I32, (LANES, LANES), 1)
    ones_bd = jnp.where((li // HEAD_DIM) == (lj // HEAD_DIM), 1.0, 0.0).astype(BF16)
    upper_strict = li < lj
    upper_incl = li <= lj
    eye = jnp.where(li == lj, 1.0, 0.0)
    lo = lax.broadcasted_iota(I32, (c, LANES), 1) < HEAD_DIM
    rows = lax.broadcasted_iota(I32, (tl, pr_ref.shape[-1]), 0)

    def head_sum(x):
        return _nn(_bf(x), ones_bd)

    def stack(x):
        return _bf(jnp.concatenate([jnp.where(lo, x, 0.0), jnp.where(lo, 0.0, x)], axis=0))

    n_pc = n_chunks * n_bp
    atrt_l, btkt_l, vst_l, kh_l, bh_l = ([None] * n_pc for _ in range(5))
    kept = []
    for bi in range(nb):
        p = pr_ref[bi]
        shifted = jnp.where(rows == 0, prev_ref[bi, 7:8, :], pltpu.roll(p, 1, 0))
        prev_ref[bi] = p[tl - 8:tl, :]
        ps = p + (shifted - p) * mu_ref[...]
        r = ps[:, 0:width]
        k = ps[:, width:2 * width]
        v = ps[:, 2 * width:3 * width]
        lora = ps[:, 3 * width:3 * width + LORA_W]
        nz = -(w0_ref[...] + _nn(_bf(jnp.tanh(lora)), wup_ref[...]))
        log_w = -(jnp.maximum(nz, 0.0) + jnp.log(1.0 + jnp.exp(-jnp.abs(nz)))) - 0.5
        lw = -jnp.exp(log_w)
        a_sig = jax.nn.sigmoid(a0_ref[...] + _nn(_bf(lora), aup_ref[...]))
        gate = _nn(_bf(jax.nn.sigmoid(lora)), gup_ref[...])
        lw_hi = _bf(lw)
        lw_lo = _bf(lw - lw_hi.astype(F32))
        cum = _nn(tri, lw_hi) + _nn(tri, lw_lo)
        tot = _nn(blk, lw_hi) + _nn(blk, lw_lo)
        kept.append((r, k, v, a_sig, gate, tot))
        for pi in range(n_pairs):
            sl = slice(pi * LANES, (pi + 1) * LANES)
            r_p, k_p, v_p, a_p = r[:, sl], k[:, sl], v[:, sl], a_sig[:, sl]
            kk = k_p * kk_ref[:, sl]
            kk = kk * lax.rsqrt(jnp.maximum(head_sum(kk * kk), 1e-24))
            kmod = k_p * (1.0 + (a_p - 1.0) * ka_ref[:, sl])
            bvec = kk * a_p
            cum_p, tot_p = cum[:, sl], tot[:, sl]
            p_inv = jnp.exp(-cum_p)
            p_end = jnp.exp(tot_p - cum_p)
            at_all = -kk * jnp.exp(cum_p - lw[:, sl])
            rt_all = r_p * jnp.exp(cum_p)
            bt_all, kt_all = bvec * p_inv, kmod * p_inv
            bh_all, kh_all = bvec * p_end, kmod * p_end
            for ci in range(n_chunks):
                rs = slice(ci * c, (ci + 1) * c)
                idx = (ci * nb + bi) * n_pairs + pi
                atrt_l[idx] = jnp.concatenate([stack(at_all[rs]), stack(rt_all[rs])], axis=0)
                btkt_l[idx] = jnp.concatenate([stack(bt_all[rs]), stack(kt_all[rs])], axis=0)
                vst_l[idx] = _bf(jnp.concatenate([jnp.where(lo, v_p[rs], 0.0),
                                                  jnp.where(lo, 0.0, v_p[rs])], axis=0).T)
                kh_l[idx] = stack(kh_all[rs])
                bh_l[idx] = stack(bh_all[rs])

    atrt = jnp.stack(atrt_l)
    gt = _bnt(jnp.stack(btkt_l), atrt)
    n_t = jnp.where(upper_strict, gt[:, 0:rows2, 0:rows2], 0.0)
    m_rb_t = jnp.where(upper_incl, gt[:, 0:rows2, rows2:], 0.0)
    m_ak_t = jnp.where(upper_strict, gt[:, rows2:, 0:rows2], 0.0)
    m_rk_t = jnp.where(upper_incl, gt[:, rows2:, rows2:], 0.0)
    tinv = eye + n_t
    npow = _bf(n_t)
    for _ in range(int(np.log2(c)) - 1):
        npow = _bf(_bnn(npow, npow))
        tinv = tinv + _bnn(_bf(tinv), npow)
    rhs = jnp.concatenate([_bf(m_ak_t), jnp.stack(kh_l), _bf(m_rk_t)], axis=2)
    w_ref[...] = _bnn(jnp.stack(vst_l), rhs)
    tinv_ref[...] = _bf(tinv)
    atrt_ref[...] = atrt
    bhm_ref[...] = jnp.concatenate([jnp.stack(bh_l), _bf(m_rb_t)], axis=2)

    for ci in range(n_chunks):
        rs = slice(ci * c, (ci + 1) * c)
        ids = slice(ci * n_bp, (ci + 1) * n_bp)
        state = state_ref[...]
        x = _bnt(_bf(state), atrt_ref[ids])
        w = w_ref[ids]
        us_t = _bnn(_bf(x[:, :, 0:rows2] + w[:, :, 0:rows2]), tinv_ref[ids])
        z = _bnn(_bf(us_t), bhm_ref[ids])
        decay_c = jnp.exp(jnp.stack([kept[bi][5][ci * c:ci * c + 1, pi * LANES:(pi + 1) * LANES]
                                     for bi in range(nb) for pi in range(n_pairs)]))
        state_ref[...] = state * decay_c + z[:, :, 0:rows2] + w[:, :, rows2:2 * rows2]
        ys_t = x[:, :, rows2:] + z[:, :, rows2:] + w[:, :, 2 * rows2:]
        for bi in range(nb):
            for pi in range(n_pairs):
                ys = ys_t[bi * n_pairs + pi].T
                y_ref[bi, rs, pi * LANES:(pi + 1) * LANES] = ys[0:c, :] + ys[c:rows2, :]

    for bi in range(nb):
        r, k, v, a_sig, gate, _ = kept[bi]
        for pi in range(n_pairs):
            sl = slice(pi * LANES, (pi + 1) * LANES)
            y = y_ref[bi, :, sl]
            mean = head_sum(y) * (1.0 / HEAD_DIM)
            yc = y - mean
            var = head_sum(yc * yc) * (1.0 / HEAD_DIM)
            yn = yc * lax.rsqrt(var + GN_EPS) * lnw_ref[:, sl] + lnb_ref[:, sl]
            kmod = k[:, sl] * (1.0 + (a_sig[:, sl] - 1.0) * ka_ref[:, sl])
            bonus = head_sum(r[:, sl] * kmod * rk_ref[:, sl]) * v[:, sl]
            o_ref[bi, :, sl] = ((yn + bonus) * gate[:, sl]).astype(o_ref.dtype)


def _rwkv(pr, vecs, mats, c, tl, nb):
    b, seq, pw = pr.shape
    width = (pw - LORA_W) // 3
    n_pairs = width // LANES
    n_pc = (tl // c) * nb * n_pairs
    full = lambda a: pl.BlockSpec(a.shape, lambda bi, i: (0,) * a.ndim)
    mu, w0, a0, kk, ka, rk, lnw, lnb = vecs
    wup, aup, gup = [m.astype(BF16) for m in mats]
    return pl.pallas_call(
        functools.partial(_rwkv_kernel, c=c),
        grid=(b // nb, seq // tl),
        in_specs=[pl.BlockSpec((nb, tl, pw), lambda bi, i: (bi, i, 0)),
                  full(mu), full(w0), full(wup), full(a0), full(aup), full(gup),
                  full(kk), full(ka), full(rk), full(lnw), full(lnb)],
        out_specs=pl.BlockSpec((nb, tl, width), lambda bi, i: (bi, i, 0)),
        out_shape=jax.ShapeDtypeStruct((b, seq, width), BF16),
        scratch_shapes=[pltpu.VMEM((nb * n_pairs, LANES, LANES), F32),
                        pltpu.VMEM((nb, 8, pw), F32),
                        pltpu.VMEM((n_pc, 2 * c, 2 * c), BF16),
                        pltpu.VMEM((n_pc, 2 * c, 6 * c), F32),
                        pltpu.VMEM((n_pc, 4 * c, LANES), BF16),
                        pltpu.VMEM((n_pc, 2 * c, 4 * c), BF16),
                        pltpu.VMEM((nb, tl, width), F32)],
        compiler_params=pltpu.CompilerParams(
            dimension_semantics=("parallel", "arbitrary"), vmem_limit_bytes=VMEM_LIMIT),
        name="rwkv7",
    )(pr, mu, w0, wup, a0, aup, gup, kk, ka, rk, lnw, lnb)


def _layer_norm(z, w, b):
    mu = jnp.mean(z, axis=-1, keepdims=True)
    zc = z - mu
    var = jnp.mean(zc * zc, axis=-1, keepdims=True)
    return zc * lax.rsqrt(var + LN_EPS) * w + b


def _merge_kernel(x_ref, ya_ref, yr_ref, wg_ref, wa_ref, wb_ref, wo_ref, lnw_ref, lnb_ref,
                  o_ref, *, alpha):
    x = x_ref[...]
    d = x.shape[-1]
    xb = x.astype(BF16)
    mix = (jax.nn.sigmoid(_nn(xb, wg_ref[:, 0:d])) * _nn(ya_ref[...], wa_ref[...])
           + jax.nn.sigmoid(_nn(xb, wg_ref[:, d:2 * d])) * _nn(yr_ref[...], wb_ref[...]))
    z = alpha * x + _nn(mix.astype(BF16), wo_ref[...])
    o_ref[...] = _layer_norm(z, lnw_ref[...], lnb_ref[...])


def _merge(x2, ya, yr, wg, wa, wb, wo, lnw, lnb, tm, alpha):
    m, d = x2.shape
    full = lambda a: pl.BlockSpec(a.shape, lambda i: (0,) * a.ndim)
    row = lambda n: pl.BlockSpec((tm, n), lambda i: (i, 0))
    return pl.pallas_call(
        functools.partial(_merge_kernel, alpha=alpha),
        grid=(m // tm,),
        in_specs=[row(d), row(ya.shape[1]), row(yr.shape[1]),
                  full(wg), full(wa), full(wb), full(wo), full(lnw), full(lnb)],
        out_specs=row(d),
        out_shape=jax.ShapeDtypeStruct((m, d), F32),
        compiler_params=pltpu.CompilerParams(
            dimension_semantics=("parallel",), vmem_limit_bytes=VMEM_LIMIT),
        name="merge_out_ln",
    )(x2, ya, yr, wg, wa, wb, wo, lnw, lnb)


SUBLANES = 8
FFN_COLS = 256
FFN_VMEM_LIMIT = 56 * 1024 * 1024


def _ffn_kernel(h_ref, wu_ref, cw_ref, cb_ref, wd_ref, lnw_ref, lnb_ref, o_ref,
                prev_ref, act_ref, *, alpha, tiles_per_seq):
    tm = h_ref.shape[0]
    dff = wd_ref.shape[0]
    h = h_ref[...]
    hb = h.astype(BF16)

    @pl.when(pl.program_id(0) % tiles_per_seq == 0)
    def _():
        prev_ref[...] = jnp.zeros_like(prev_ref)

    def conv_proj(cols):
        u = _nn(hb, wu_ref[:, cols])
        ext = jnp.concatenate([prev_ref[:, cols], u], axis=0)
        prev_ref[:, cols] = u[tm - SUBLANES:, :]
        w = cw_ref[:, cols]
        back1 = pltpu.roll(ext, 1, 0)[SUBLANES:]
        back2 = pltpu.roll(ext, 2, 0)[SUBLANES:]
        return w[0:1] * back2 + w[1:2] * back1 + w[2:3] * u + cb_ref[:, cols]

    for c in range(dff // FFN_COLS):
        gate = conv_proj(slice(c * FFN_COLS, (c + 1) * FFN_COLS))
        up = conv_proj(slice(dff + c * FFN_COLS, dff + (c + 1) * FFN_COLS))
        act_ref[:, c * FFN_COLS:(c + 1) * FFN_COLS] = (gate * jax.nn.sigmoid(gate) * up).astype(BF16)
    o_ref[...] = _layer_norm(alpha * h + _nn(act_ref[...], wd_ref[...]), lnw_ref[...], lnb_ref[...])


def _ffn(h1, wu, cw, cb, wd, lnw, lnb, tm, seq, alpha):
    m, d = h1.shape
    dff = wd.shape[0]
    assert dff % FFN_COLS == 0 and seq % tm == 0
    const = lambda a: pl.BlockSpec(a.shape, lambda i: (0,) * a.ndim, pipeline_mode=pl.Buffered(1))
    return pl.pallas_call(
        functools.partial(_ffn_kernel, alpha=alpha, tiles_per_seq=seq // tm),
        grid=(m // tm,),
        in_specs=[pl.BlockSpec((tm, d), lambda i: (i, 0))]
        + [const(a) for a in (wu, cw, cb, wd, lnw, lnb)],
        out_specs=pl.BlockSpec((tm, d), lambda i: (i, 0)),
        out_shape=jax.ShapeDtypeStruct((m, d), F32),
        scratch_shapes=[pltpu.VMEM((SUBLANES, 2 * dff), F32), pltpu.VMEM((tm, dff), BF16)],
        compiler_params=pltpu.CompilerParams(
            dimension_semantics=("arbitrary",), vmem_limit_bytes=FFN_VMEM_LIMIT),
        name="conv_ffn_ln",
    )(h1, wu, cw, cb, wd, lnw, lnb)


def _tile_sizes(batch, seq):
    return dict(tm_proj=min(256, seq), tq=min(256, seq), chunk=64, tl_rwkv=min(256, seq),
                nb_rwkv=2 if batch % 2 == 0 else 1, tm_merge=min(256, seq), tm_ffn=min(512, seq))


def _layer(h, w_in, idx_w, idx_b, mu, w0, w_up, a0, a_up, g_up, k_k, k_a, r_k, gn_w, gn_b,
           w_ba, w_br, w_out, ln1_w, ln1_b, w_up_ffn, conv_w, conv_b, w_down, ln2_w, ln2_b, alpha):
    b, seq, d = h.shape
    ts = _tile_sizes(b, seq)
    aw = HEADS * HEAD_DIM
    kvw = KV_HEADS * HEAD_DIM
    o = np.cumsum([0, aw, kvw, kvw, aw, HEAD_DIM, HEADS, 3 * aw + LORA_W, d, d])
    col = lambda i: w_in[:, o[i]:o[i + 1]]
    head = lambda w, hh: w[:, hh * HEAD_DIM:(hh + 1) * HEAD_DIM]
    zero = jnp.zeros((d, HEAD_DIM), w_in.dtype)
    k_cols = [blk for hh in range(KV_HEADS)
              for blk in (head(col(1), hh), zero, zero, head(col(1), hh))]
    wi_pad = jnp.pad(col(5), ((0, 0), (0, LANES - HEADS)))
    wa = jnp.concatenate([col(0)] + k_cols + [col(2), col(3), col(4), col(4), wi_pad],
                         axis=1).astype(BF16)
    wr = col(6).astype(BF16)
    row2 = lambda v: v.reshape(1, -1)
    lnw2 = row2(jnp.concatenate([idx_w, idx_w]))
    lnb2 = row2(jnp.concatenate([idx_b, idx_b]))

    x2 = h.reshape(b * seq, d)
    q, k4, vt, qi, ki2, wi, pr = _inproj(x2, wa, wr, lnw2, lnb2, ts["tm_proj"], seq)
    r3 = lambda a: a.reshape(b, seq, a.shape[-1])

    y_attn = _attention(r3(q), r3(qi), r3(wi), r3(k4), vt, r3(ki2), ts["tq"])

    ld, la = w_up.shape[0], a_up.shape[0]
    pad_rows = lambda w, start: jnp.pad(w, ((start, LORA_W - start - w.shape[0]), (0, 0)))
    vecs = [row2(v) for v in (mu, w0, a0, k_k, k_a, r_k.reshape(-1), gn_w, gn_b)]
    mats = [pad_rows(w_up, 0), pad_rows(a_up, ld), pad_rows(g_up, ld + la)]
    y_rwkv = _rwkv(r3(pr), vecs, mats, ts["chunk"], ts["tl_rwkv"], ts["nb_rwkv"])

    h1 = _merge(x2, y_attn.reshape(b * seq, aw), y_rwkv.reshape(b * seq, aw),
                w_in[:, o[7]:o[9]].astype(BF16),
                w_ba.astype(BF16), w_br.astype(BF16), w_out.astype(BF16),
                row2(ln1_w), row2(ln1_b), ts["tm_merge"], alpha)

    out = _ffn(h1, w_up_ffn.astype(BF16), conv_w, row2(conv_b), w_down.astype(BF16),
               row2(ln2_w), row2(ln2_b), ts["tm_ffn"], seq, alpha)
    return out.reshape(b, seq, d)


def kernel(x, w_in, idx_k_norm_w, idx_k_norm_b, rwkv_mu, rwkv_w0, rwkv_w_up, rwkv_a0, rwkv_a_up,
           rwkv_g_up, rwkv_k_k, rwkv_k_a, rwkv_r_k, rwkv_ln_w, rwkv_ln_b, w_branch_attn,
           w_branch_rwkv, w_out, ln1_w, ln1_b, w_up, conv_w, conv_b, w_down, ln2_w, ln2_b):
    depth = w_in.shape[0]
    alpha = (2.0 * depth) ** 0.25
    h = x
    for l in range(depth):
        h = _layer(h, w_in[l], idx_k_norm_w[l], idx_k_norm_b[l], rwkv_mu[l], rwkv_w0[l],
                   rwkv_w_up[l], rwkv_a0[l], rwkv_a_up[l], rwkv_g_up[l], rwkv_k_k[l], rwkv_k_a[l],
                   rwkv_r_k[l], rwkv_ln_w[l], rwkv_ln_b[l], w_branch_attn[l], w_branch_rwkv[l],
                   w_out[l], ln1_w[l], ln1_b[l], w_up[l], conv_w[l], conv_b[l], w_down[l],
                   ln2_w[l], ln2_b[l], alpha)
    return h
```

```python
import functools

import jax
import jax.numpy as jnp
import ml_dtypes
import numpy as np
from jax import lax
from jax.experimental import pallas as pl
from jax.experimental.pallas import tpu as pltpu

F32 = jnp.float32
BF16 = jnp.bfloat16
I32 = jnp.int32

LANES = 128
HEAD_DIM = 64
HEADS = 8
KV_HEADS = 2
TOPK_MAX = 256
LORA_W = 128
LN_EPS = 1e-5
GN_EPS = 64e-5
INT_MIN = -(2 ** 31)
ATTN_CLASSES = 4
BISECT_PASSES = 20
BISECT_ROWS = 256
BISECT_UNROLL = 4
VMEM_LIMIT = 48 * 1024 * 1024


def _nt(a, b):
    return lax.dot_general(a, b, (((1,), (1,)), ((), ())), preferred_element_type=F32)


def _nn(a, b):
    return lax.dot_general(a, b, (((1,), (0,)), ((), ())), preferred_element_type=F32)


def _bnt(a, b):
    return lax.dot_general(a, b, (((2,), (2,)), ((0,), (0,))), preferred_element_type=F32)


def _bnn(a, b):
    return lax.dot_general(a, b, (((2,), (1,)), ((0,), (0,))), preferred_element_type=F32)


def _split2(x):
    hi = x.astype(BF16)
    lo = (x - hi.astype(F32)).astype(BF16)
    return hi, lo


def _mm3(a, b, dot=_nn):
    ah, al = _split2(a)
    bh, bl = _split2(b)
    return dot(ah, bh) + (dot(ah, bl) + dot(al, bh))


def _mm_exact_lhs(a_bf16, b):
    b0 = b.astype(BF16)
    r1 = b - b0.astype(F32)
    b1 = r1.astype(BF16)
    b2 = (r1 - b1.astype(F32)).astype(BF16)
    return _nn(a_bf16, b0) + (_nn(a_bf16, b1) + _nn(a_bf16, b2))


LOG2E = 1.4426950408889634
POS_SPLIT = 16
SLOPE_TERMS = 3


def _inproj_kernel(x_ref, wa_ref, wr_ref, lnw_ref, lnb_ref,
                   q_ref, k4_ref, vt_ref, qi_ref, ki2_ref, wi_ref, pr_ref, *, seq):
    tm = x_ref.shape[0]
    xb = x_ref[...].astype(BF16)
    pa = _nn(xb, wa_ref[...])
    q_ref[...] = (pa[:, 0:512] * (HEAD_DIM ** -0.5 * LOG2E)).astype(BF16)
    pos = (pl.program_id(0) * tm) % seq + lax.broadcasted_iota(I32, (tm, 2 * LANES), 0)
    s_lo = pos & (POS_SPLIT - 1)
    ln = lax.broadcasted_iota(I32, (tm, 2 * LANES), 1)
    a = jnp.where(ln < LANES, ln - HEAD_DIM, ln - LANES)
    in_aug = (a >= 0) & (a < 2 * SLOPE_TERMS)
    aug = jnp.where(in_aug, jnp.where((a & 1) == 0, pos - s_lo, s_lo), 0).astype(F32)
    for g in range(KV_HEADS):
        sl = slice(512 + g * 2 * LANES, 512 + (g + 1) * 2 * LANES)
        k4_ref[:, g * 2 * LANES:(g + 1) * 2 * LANES] = (pa[:, sl] + aug).astype(BF16)
    vt = pa[:, 1024:1152].T
    vt_swapped = pltpu.roll(vt, HEAD_DIM, 0)
    vrow = lax.broadcasted_iota(I32, vt.shape, 0)
    top = vrow < HEAD_DIM
    one_at = lambda r: jnp.where(vrow == r, 1.0, 0.0)
    for g, (v_top, v_bottom) in enumerate(((vt, vt_swapped), (vt_swapped, vt))):
        vt_ref[(2 * g) * LANES:(2 * g + 1) * LANES, :] = jnp.where(top, v_top, one_at(HEAD_DIM)).astype(BF16)
        vt_ref[(2 * g + 1) * LANES:(2 * g + 2) * LANES, :] = jnp.where(top, one_at(0), v_bottom).astype(BF16)
    qi_ref[...] = pa[:, 1152:1664].astype(BF16)
    ki = pa[:, 1664:1792]
    mu = jnp.mean(ki, axis=-1, keepdims=True)
    var = jnp.mean(jnp.square(ki - mu), axis=-1, keepdims=True)
    ki2_ref[...] = ((ki - mu) * lax.rsqrt(var + LN_EPS) * lnw_ref[...] + lnb_ref[...]).astype(BF16)
    wi_ref[...] = pa[:, 1792:1920]
    pr_ref[...] = _nn(xb, wr_ref[...])


def _inproj(x2, wa, wr, lnw2, lnb2, tm, seq):
    m, d = x2.shape
    assert seq <= 256 * POS_SPLIT
    full = lambda a: pl.BlockSpec(a.shape, lambda i: (0,) * a.ndim)
    row = lambda n: pl.BlockSpec((tm, n), lambda i: (i, 0))
    outs = [(512, BF16), (512, BF16), None, (512, BF16), (128, BF16), (128, F32),
            (wr.shape[1], F32)]
    vt_rows = 2 * KV_HEADS * LANES
    return pl.pallas_call(
        functools.partial(_inproj_kernel, seq=seq),
        grid=(m // tm,),
        in_specs=[row(d), full(wa), full(wr), full(lnw2), full(lnb2)],
        out_specs=[row(o[0]) if o else pl.BlockSpec((vt_rows, tm), lambda i: (0, i)) for o in outs],
        out_shape=[jax.ShapeDtypeStruct((m, o[0]), o[1]) if o
                   else jax.ShapeDtypeStruct((vt_rows, m), BF16) for o in outs],
        compiler_params=pltpu.CompilerParams(
            dimension_semantics=("parallel",), vmem_limit_bytes=VMEM_LIMIT),
        name="inproj",
    )(x2, wa, wr, lnw2, lnb2)


def _bf16_parts(x):
    parts = []
    for _ in range(SLOPE_TERMS):
        parts.append(float(ml_dtypes.bfloat16(x)))
        x -= parts[-1]
    return parts


def _attn_body(q_ref, qi_ref, wi_ref, k4_ref, vt_ref, ki2_ref, o_ref,
               key_ref, bias_ref, *, t0, t0_max, tq, width, n_sel, cw):
    lane = lax.broadcasted_iota(I32, (tq, LANES), 1)
    lo = lane < HEAD_DIM
    n_cc = width // cw
    n_lc = width // LANES

    def head_lhs(ref, h, fill=None):
        pair = ref[0, :, (h // 2) * LANES:(h // 2 + 1) * LANES]
        fill = jnp.zeros_like(pair) if fill is None else fill
        return jnp.where(lo if h % 2 == 0 else jnp.logical_not(lo), pair, fill)

    wi = wi_ref[0] * (HEAD_DIM ** -0.5 * HEADS ** -0.5)
    row_c = t0 + lax.broadcasted_iota(I32, (tq, cw), 0)
    col_c = lax.broadcasted_iota(I32, (tq, cw), 1)
    for c in range(n_cc):
        kc = ki2_ref[0, c * cw:(c + 1) * cw, :]
        s = jnp.zeros((tq, cw), F32)
        for h in range(HEADS):
            s = s + jnp.maximum(_nt(head_lhs(qi_ref, h), kc), 0.0) * wi[:, h:h + 1]
        key_ref[:, c * cw:(c + 1) * cw] = jnp.where(col_c + c * cw <= row_c, s, -jnp.inf)

    kf = jnp.float32(n_sel)
    rb = min(BISECT_ROWS, tq)
    n_rb = tq // rb
    blocks = [slice(b * rb, (b + 1) * rb) for b in range(n_rb)]
    sls = [slice(c * LANES, (c + 1) * LANES) for c in range(n_lc)]
    n_live = {blk.start: min(n_lc, -(-(t0_max + blk.stop) // LANES)) for blk in blocks}
    key_lowest = jnp.int32(INT_MIN + 2 ** 23)
    key_highest = jnp.int32(0x7F800000)

    def as_float(key):
        key = jnp.clip(key, key_lowest, key_highest)
        bits = jnp.where(key < 0, key ^ jnp.int32(0x7FFFFFFF), key)
        return lax.bitcast_convert_type(bits, F32)

    def count(pred_fn, rows, read=lambda ref, rows, sl: ref[rows, sl], ref=key_ref):
        acc = jnp.zeros((rb, LANES), F32)
        for sl in sls[:n_live[rows.start]]:
            acc = acc + jnp.where(pred_fn(read(ref, rows, sl)), 1.0, 0.0)
        return jnp.broadcast_to(jnp.sum(acc, axis=1, keepdims=True), (rb, LANES))

    def bis_body(i, ts):
        inc = lax.shift_left(jnp.int32(1), jnp.int32(31) - i)
        out = []
        for rows, t in zip(blocks, ts):
            cand = t + inc
            cand_f = as_float(cand)
            out.append(jnp.where(count(lambda sc: sc >= cand_f, rows) >= kf, cand, t))
        return tuple(out)

    ts = lax.fori_loop(0, BISECT_PASSES, bis_body,
                       tuple(jnp.full((rb, LANES), INT_MIN, I32) for _ in blocks),
                       unroll=BISECT_UNROLL)
    lows = [as_float(t) for t in ts]

    def smallest(pred_fn, rows):
        acc = jnp.full((rb, LANES), jnp.inf, F32)
        for sl in sls[:n_live[rows.start]]:
            sc = key_ref[rows, sl]
            acc = jnp.minimum(acc, jnp.where(pred_fn(sc), sc, jnp.inf))
        return jnp.broadcast_to(jnp.min(acc, axis=1, keepdims=True), (rb, LANES))

    def settle(m, thr, done, rows):
        newly = jnp.where(count(lambda sc: sc > m, rows) < kf, 1.0 - done, 0.0)
        return jnp.where(newly > 0.0, m, thr), jnp.maximum(done, newly)

    walk = []
    for rows, low in zip(blocks, lows):
        m = smallest(lambda sc: sc >= low, rows)
        walk.append((m,) + settle(m, low, jnp.zeros((rb, LANES), F32), rows))

    def walk_pending(state):
        return functools.reduce(jnp.minimum, [jnp.min(done) for _, _, done in state]) < 1.0

    def walk_step(state):
        out = []
        for rows, (m, thr, done) in zip(blocks, state):
            m = smallest(lambda sc: sc > m, rows)
            out.append((m,) + settle(m, thr, done, rows))
        return tuple(out)

    thrs = [thr for _, thr, _ in lax.while_loop(walk_pending, walk_step, tuple(walk))]

    excess = [count(lambda sc: sc >= t, rows) - kf for rows, t in zip(blocks, thrs)]
    has_ties = functools.reduce(jnp.maximum, [jnp.max(e) for e in excess]) > 0.0

    @pl.when(jnp.logical_not(has_ties))
    def _():
        for rows, t in zip(blocks, thrs):
            for sl in sls:
                bias_ref[rows, sl] = jnp.where(key_ref[rows, sl] >= t, 0.0, -jnp.inf)

    @pl.when(has_ties)
    def _():
        lane_rb = lax.broadcasted_iota(I32, (rb, LANES), 1)
        read_pos = lambda ref, rows, sl: lax.bitcast_convert_type(ref[rows, sl], I32)
        nbits = int(np.log2(width - 1)) + 1
        for rows, t in zip(blocks, thrs):
            need = kf - count(lambda kc: kc > t, rows)
            for c, sl in enumerate(sls):
                tpos = jnp.where(key_ref[rows, sl] == t, lane_rb + c * LANES, jnp.int32(2 ** 30))
                bias_ref[rows, sl] = lax.bitcast_convert_type(tpos, F32)

            def tie_body(i, j, rows=rows, need=need):
                cand = j + lax.shift_left(jnp.int32(1), jnp.int32(nbits - 1) - i)
                below = count(lambda pc: pc < cand, rows, read_pos, bias_ref)
                return jnp.where(below < need, cand, j)

            jmax = lax.fori_loop(0, nbits, tie_body, jnp.zeros((rb, LANES), I32))
            for sl in sls:
                keep = jnp.where(read_pos(bias_ref, rows, sl) <= jmax, 0.0, -jnp.inf)
                bias_ref[rows, sl] = jnp.where(key_ref[rows, sl] > t, 0.0, keep)

    bias = bias_ref[:, 0:width]
    a_idx = jnp.where(lo, lane, lane - HEAD_DIM)

    def qk(h):
        parts = _bf16_parts((2.0 ** -(h + 1)) * LOG2E)
        fill = jnp.zeros((tq, LANES), F32)
        for n in reversed(range(SLOPE_TERMS)):
            fill = jnp.where(a_idx < 2 * (n + 1), parts[n], fill)
        kcol = (2 * (h // (HEADS // KV_HEADS)) + h % 2) * LANES
        return _nt(head_lhs(q_ref, h, fill.astype(BF16)), k4_ref[0, 0:width, kcol:kcol + LANES])

    even = None
    qk_next = qk(0)
    for h in range(HEADS):
        g = h // (HEADS // KV_HEADS)
        qk_cur, qk_next = qk_next, (qk(h + 1) if h + 1 < HEADS else None)
        logit = qk_cur + bias
        m = jnp.max(logit, axis=1, keepdims=True)
        e = jnp.exp2(logit - m)
        vrows = (2 * g + h % 2) * LANES
        out = _nt(vt_ref[vrows:vrows + LANES, 0:width], e.astype(BF16)).T
        if h % 2 == 0:
            even = out
        else:
            p = h // 2
            denom = jnp.where(lo, jnp.broadcast_to(even[:, HEAD_DIM:HEAD_DIM + 1], (tq, LANES)),
                              jnp.broadcast_to(out[:, 0:1], (tq, LANES)))
            o_ref[0, :, p * LANES:(p + 1) * LANES] = (jnp.where(lo, even, out) / denom).astype(o_ref.dtype)


def _attn_kernel(q_ref, qi_ref, wi_ref, k4_ref, vt_ref, ki2_ref, o_ref,
                 key_ref, bias_ref, *, tq, seq, n_sel, n_classes):
    i = pl.program_id(0)
    tiles_per_class = (seq // tq) // n_classes
    for cls in range(n_classes):
        width = (cls + 1) * tiles_per_class * tq

        @pl.when((i >= cls * tiles_per_class) & (i < (cls + 1) * tiles_per_class))
        def _():
            _attn_body(q_ref, qi_ref, wi_ref, k4_ref, vt_ref, ki2_ref, o_ref, key_ref, bias_ref,
                       t0=i * tq, t0_max=width - tq, tq=tq, width=width, n_sel=n_sel,
                       cw=512 if width % 512 == 0 else 256)


def _attention(q, qi, wi, k4, vt, ki2, tq):
    b, seq, _ = q.shape
    n_sel = min(TOPK_MAX, seq // 4)
    n_classes = min(ATTN_CLASSES, seq // tq)
    qblk = lambda n: pl.BlockSpec((1, tq, n), lambda i, bi: (bi, i, 0))
    kblk = lambda n: pl.BlockSpec((1, seq, n), lambda i, bi: (bi, 0, 0))
    vblk = pl.BlockSpec((vt.shape[0], seq), lambda i, bi: (0, bi))
    return pl.pallas_call(
        functools.partial(_attn_kernel, tq=tq, seq=seq, n_sel=n_sel, n_classes=n_classes),
        grid=(seq // tq, b),
        in_specs=[qblk(512), qblk(512), qblk(128), kblk(512), vblk, kblk(128)],
        out_specs=qblk(512),
        out_shape=jax.ShapeDtypeStruct((b, seq, 512), BF16),
        scratch_shapes=[pltpu.VMEM((tq, seq), F32), pltpu.VMEM((tq, seq), F32)],
        compiler_params=pltpu.CompilerParams(
            dimension_semantics=("parallel", "parallel"), vmem_limit_bytes=VMEM_LIMIT),
        name="dsa_attention",
    )(q, qi, wi, k4, vt, ki2)


def _bf(x):
    return x.astype(BF16)


def _rwkv_kernel(pr_ref, mu_ref, w0_ref, wup_ref, a0_ref, aup_ref, gup_ref, kk_ref, ka_ref,
                 rk_ref, lnw_ref, lnb_ref, o_ref,
                 state_ref, prev_ref, tinv_ref, w_ref, atrt_ref, bhm_ref, y_ref, *, c):
    nb, tl, width = o_ref.shape
    n_pairs = width // LANES
    n_chunks = tl // c
    rows2 = 2 * c
    n_bp = nb * n_pairs

    @pl.when(pl.program_id(1) == 0)
    def _():
        state_ref[...] = jnp.zeros_like(state_ref)
        prev_ref[...] = jnp.zeros_like(prev_ref)

    ti = lax.broadcasted_iota(I32, (tl, tl), 0)
    tj = lax.broadcasted_iota(I32, (tl, tl), 1)
    same_chunk = (ti // c) == (tj // c)
    tri = jnp.where(same_chunk & (ti >= tj), 1.0, 0.0).astype(BF16)
    blk = jnp.where(same_chunk, 1.0, 0.0).astype(BF16)
    li = lax.broadcasted_iota(I32, (LANES, LANES), 0)
    lj = lax.broadcasted_iota(I32, (LANES, LANES), 1)
    ones_bd = jnp.where((li // HEAD_DIM) == (lj // HEAD_DIM), 1.0, 0.0).astype(BF16)
    upper_strict = li < lj
    upper_incl = li <= lj
    eye = jnp.where(li == lj, 1.0, 0.0)
    lo = lax.broadcasted_iota(I32, (c, LANES), 1) < HEAD_DIM
    rows = lax.broadcasted_iota(I32, (tl, pr_ref.shape[-1]), 0)

    def head_sum(x):
        return _nn(_bf(x), ones_bd)

    def stack(x):
        return _bf(jnp.concatenate([jnp.where(lo, x, 0.0), jnp.where(lo, 0.0, x)], axis=0))

    n_pc = n_chunks * n_bp
    atrt_l, btkt_l, vst_l, kh_l, bh_l = ([None] * n_pc for _ in range(5))
    kept = []
    for bi in range(nb):
        p = pr_ref[bi]
        shifted = jnp.where(rows == 0, prev_ref[bi, 7:8, :], pltpu.roll(p, 1, 0))
        prev_ref[bi] = p[tl - 8:tl, :]
        ps = p + (shifted - p) * mu_ref[...]
        r = ps[:, 0:width]
        k = ps[:, width:2 * width]
        v = ps[:, 2 * width:3 * width]
        lora = ps[:, 3 * width:3 * width + LORA_W]
        nz = -(w0_ref[...] + _nn(_bf(jnp.tanh(lora)), wup_ref[...]))
        log_w = -(jnp.maximum(nz, 0.0) + jnp.log(1.0 + jnp.exp(-jnp.abs(nz)))) - 0.5
        lw = -jnp.exp(log_w)
        a_sig = jax.nn.sigmoid(a0_ref[...] + _nn(_bf(lora), aup_ref[...]))
        gate = _nn(_bf(jax.nn.sigmoid(lora)), gup_ref[...])
        lw_hi = _bf(lw)
        lw_lo = _bf(lw - lw_hi.astype(F32))
        cum = _nn(tri, lw_hi) + _nn(tri, lw_lo)
        tot = _nn(blk, lw_hi) + _nn(blk, lw_lo)
        kept.append((r, k, v, a_sig, gate, tot))
        for pi in range(n_pairs):
            sl = slice(pi * LANES, (pi + 1) * LANES)
            r_p, k_p, v_p, a_p = r[:, sl], k[:, sl], v[:, sl], a_sig[:, sl]
            kk = k_p * kk_ref[:, sl]
            kk = kk * lax.rsqrt(jnp.maximum(head_sum(kk * kk), 1e-24))
            kmod = k_p * (1.0 + (a_p - 1.0) * ka_ref[:, sl])
            bvec = kk * a_p
            cum_p, tot_p = cum[:, sl], tot[:, sl]
            p_inv = jnp.exp(-cum_p)
            p_end = jnp.exp(tot_p - cum_p)
            at_all = -kk * jnp.exp(cum_p - lw[:, sl])
            rt_all = r_p * jnp.exp(cum_p)
            bt_all, kt_all = bvec * p_inv, kmod * p_inv
            bh_all, kh_all = bvec * p_end, kmod * p_end
            for ci in range(n_chunks):
                rs = slice(ci * c, (ci + 1) * c)
                idx = (ci * nb + bi) * n_pairs + pi
                atrt_l[idx] = jnp.concatenate([stack(at_all[rs]), stack(rt_all[rs])], axis=0)
                btkt_l[idx] = jnp.concatenate([stack(bt_all[rs]), stack(kt_all[rs])], axis=0)
                vst_l[idx] = _bf(jnp.concatenate([jnp.where(lo, v_p[rs], 0.0),
                                                  jnp.where(lo, 0.0, v_p[rs])], axis=0).T)
                kh_l[idx] = stack(kh_all[rs])
                bh_l[idx] = stack(bh_all[rs])

    atrt = jnp.stack(atrt_l)
    gt = _bnt(jnp.stack(btkt_l), atrt)
    n_t = jnp.where(upper_strict, gt[:, 0:rows2, 0:rows2], 0.0)
    m_rb_t = jnp.where(upper_incl, gt[:, 0:rows2, rows2:], 0.0)
    m_ak_t = jnp.where(upper_strict, gt[:, rows2:, 0:rows2], 0.0)
    m_rk_t = jnp.where(upper_incl, gt[:, rows2:, rows2:], 0.0)
    tinv = eye + n_t
    npow = _bf(n_t)
    for _ in range(int(np.log2(c)) - 1):
        npow = _bf(_bnn(npow, npow))
        tinv = tinv + _bnn(_bf(tinv), npow)
    rhs = jnp.concatenate([_bf(m_ak_t), jnp.stack(kh_l), _bf(m_rk_t)], axis=2)
    w_ref[...] = _bnn(jnp.stack(vst_l), rhs)
    tinv_ref[...] = _bf(tinv)
    atrt_ref[...] = atrt
    bhm_ref[...] = jnp.concatenate([jnp.stack(bh_l), _bf(m_rb_t)], axis=2)

    for ci in range(n_chunks):
        rs = slice(ci * c, (ci + 1) * c)
        ids = slice(ci * n_bp, (ci + 1) * n_bp)
        state = state_ref[...]
        x = _bnt(_bf(state), atrt_ref[ids])
        w = w_ref[ids]
        us_t = _bnn(_bf(x[:, :, 0:rows2] + w[:, :, 0:rows2]), tinv_ref[ids])
        z = _bnn(_bf(us_t), bhm_ref[ids])
        decay_c = jnp.exp(jnp.stack([kept[bi][5][ci * c:ci * c + 1, pi * LANES:(pi + 1) * LANES]
                                     for bi in range(nb) for pi in range(n_pairs)]))
        state_ref[...] = state * decay_c + z[:, :, 0:rows2] + w[:, :, rows2:2 * rows2]
        ys_t = x[:, :, rows2:] + z[:, :, rows2:] + w[:, :, 2 * rows2:]
        for bi in range(nb):
            for pi in range(n_pairs):
                ys = ys_t[bi * n_pairs + pi].T
                y_ref[bi, rs, pi * LANES:(pi + 1) * LANES] = ys[0:c, :] + ys[c:rows2, :]

    for bi in range(nb):
        r, k, v, a_sig, gate, _ = kept[bi]
        for pi in range(n_pairs):
            sl = slice(pi * LANES, (pi + 1) * LANES)
            y = y_ref[bi, :, sl]
            mean = head_sum(y) * (1.0 / HEAD_DIM)
            yc = y - mean
            var = head_sum(yc * yc) * (1.0 / HEAD_DIM)
            yn = yc * lax.rsqrt(var + GN_EPS) * lnw_ref[:, sl] + lnb_ref[:, sl]
            kmod = k[:, sl] * (1.0 + (a_sig[:, sl] - 1.0) * ka_ref[:, sl])
            bonus = head_sum(r[:, sl] * kmod * rk_ref[:, sl]) * v[:, sl]
            o_ref[bi, :, sl] = ((yn + bonus) * gate[:, sl]).astype(o_ref.dtype)


def _rwkv(pr, vecs, mats, c, tl, nb):
    b, seq, pw = pr.shape
    width = (pw - LORA_W) // 3
    n_pairs = width // LANES
    n_pc = (tl // c) * nb * n_pairs
    full = lambda a: pl.BlockSpec(a.shape, lambda bi, i: (0,) * a.ndim)
    mu, w0, a0, kk, ka, rk, lnw, lnb = vecs
    wup, aup, gup = [m.astype(BF16) for m in mats]
    return pl.pallas_call(
        functools.partial(_rwkv_kernel, c=c),
        grid=(b // nb, seq // tl),
        in_specs=[pl.BlockSpec((nb, tl, pw), lambda bi, i: (bi, i, 0)),
                  full(mu), full(w0), full(wup), full(a0), full(aup), full(gup),
                  full(kk), full(ka), full(rk), full(lnw), full(lnb)],
        out_specs=pl.BlockSpec((nb, tl, width), lambda bi, i: (bi, i, 0)),
        out_shape=jax.ShapeDtypeStruct((b, seq, width), BF16),
        scratch_shapes=[pltpu.VMEM((nb * n_pairs, LANES, LANES), F32),
                        pltpu.VMEM((nb, 8, pw), F32),
                        pltpu.VMEM((n_pc, 2 * c, 2 * c), BF16),
                        pltpu.VMEM((n_pc, 2 * c, 6 * c), F32),
                        pltpu.VMEM((n_pc, 4 * c, LANES), BF16),
                        pltpu.VMEM((n_pc, 2 * c, 4 * c), BF16),
                        pltpu.VMEM((nb, tl, width), F32)],
        compiler_params=pltpu.CompilerParams(
            dimension_semantics=("parallel", "arbitrary"), vmem_limit_bytes=VMEM_LIMIT),
        name="rwkv7",
    )(pr, mu, w0, wup, a0, aup, gup, kk, ka, rk, lnw, lnb)


def _layer_norm(z, w, b):
    mu = jnp.mean(z, axis=-1, keepdims=True)
    zc = z - mu
    var = jnp.mean(zc * zc, axis=-1, keepdims=True)
    return zc * lax.rsqrt(var + LN_EPS) * w + b


def _merge_kernel(x_ref, ya_ref, yr_ref, wg_ref, wa_ref, wb_ref, wo_ref, lnw_ref, lnb_ref,
                  o_ref, *, alpha):
    x = x_ref[...]
    d = x.shape[-1]
    xb = x.astype(BF16)
    mix = (jax.nn.sigmoid(_nn(xb, wg_ref[:, 0:d])) * _nn(ya_ref[...], wa_ref[...])
           + jax.nn.sigmoid(_nn(xb, wg_ref[:, d:2 * d])) * _nn(yr_ref[...], wb_ref[...]))
    z = alpha * x + _nn(mix.astype(BF16), wo_ref[...])
    o_ref[...] = _layer_norm(z, lnw_ref[...], lnb_ref[...])


def _merge(x2, ya, yr, wg, wa, wb, wo, lnw, lnb, tm, alpha):
    m, d = x2.shape
    full = lambda a: pl.BlockSpec(a.shape, lambda i: (0,) * a.ndim)
    row = lambda n: pl.BlockSpec((tm, n), lambda i: (i, 0))
    return pl.pallas_call(
        functools.partial(_merge_kernel, alpha=alpha),
        grid=(m // tm,),
        in_specs=[row(d), row(ya.shape[1]), row(yr.shape[1]),
                  full(wg), full(wa), full(wb), full(wo), full(lnw), full(lnb)],
        out_specs=row(d),
        out_shape=jax.ShapeDtypeStruct((m, d), F32),
        compiler_params=pltpu.CompilerParams(
            dimension_semantics=("parallel",), vmem_limit_bytes=VMEM_LIMIT),
        name="merge_out_ln",
    )(x2, ya, yr, wg, wa, wb, wo, lnw, lnb)


SUBLANES = 8
FFN_COLS = 256
FFN_VMEM_LIMIT = 56 * 1024 * 1024


def _ffn_kernel(h_ref, wu_ref, cw_ref, cb_ref, wd_ref, lnw_ref, lnb_ref, o_ref,
                prev_ref, act_ref, *, alpha, tiles_per_seq):
    tm = h_ref.shape[0]
    dff = wd_ref.shape[0]
    h = h_ref[...]
    hb = h.astype(BF16)

    @pl.when(pl.program_id(0) % tiles_per_seq == 0)
    def _():
        prev_ref[...] = jnp.zeros_like(prev_ref)

    def conv_proj(cols):
        u = _nn(hb, wu_ref[:, cols])
        ext = jnp.concatenate([prev_ref[:, cols], u], axis=0)
        prev_ref[:, cols] = u[tm - SUBLANES:, :]
        w = cw_ref[:, cols]
        back1 = pltpu.roll(ext, 1, 0)[SUBLANES:]
        back2 = pltpu.roll(ext, 2, 0)[SUBLANES:]
        return w[0:1] * back2 + w[1:2] * back1 + w[2:3] * u + cb_ref[:, cols]

    for c in range(dff // FFN_COLS):
        gate = conv_proj(slice(c * FFN_COLS, (c + 1) * FFN_COLS))
        up = conv_proj(slice(dff + c * FFN_COLS, dff + (c + 1) * FFN_COLS))
        act_ref[:, c * FFN_COLS:(c + 1) * FFN_COLS] = (gate * jax.nn.sigmoid(gate) * up).astype(BF16)
    o_ref[...] = _layer_norm(alpha * h + _nn(act_ref[...], wd_ref[...]), lnw_ref[...], lnb_ref[...])


def _ffn(h1, wu, cw, cb, wd, lnw, lnb, tm, seq, alpha):
    m, d = h1.shape
    dff = wd.shape[0]
    assert dff % FFN_COLS == 0 and seq % tm == 0
    const = lambda a: pl.BlockSpec(a.shape, lambda i: (0,) * a.ndim, pipeline_mode=pl.Buffered(1))
    return pl.pallas_call(
        functools.partial(_ffn_kernel, alpha=alpha, tiles_per_seq=seq // tm),
        grid=(m // tm,),
        in_specs=[pl.BlockSpec((tm, d), lambda i: (i, 0))]
        + [const(a) for a in (wu, cw, cb, wd, lnw, lnb)],
        out_specs=pl.BlockSpec((tm, d), lambda i: (i, 0)),
        out_shape=jax.ShapeDtypeStruct((m, d), F32),
        scratch_shapes=[pltpu.VMEM((SUBLANES, 2 * dff), F32), pltpu.VMEM((tm, dff), BF16)],
        compiler_params=pltpu.CompilerParams(
            dimension_semantics=("arbitrary",), vmem_limit_bytes=FFN_VMEM_LIMIT),
        name="conv_ffn_ln",
    )(h1, wu, cw, cb, wd, lnw, lnb)


def _tile_sizes(batch, seq):
    return dict(tm_proj=min(512, seq), tq=min(256, seq), chunk=64, tl_rwkv=min(256, seq),
                nb_rwkv=2 if batch % 2 == 0 else 1, tm_merge=min(512, seq), tm_ffn=min(512, seq))


def _layer(h, w_in, idx_w, idx_b, mu, w0, w_up, a0, a_up, g_up, k_k, k_a, r_k, gn_w, gn_b,
           w_ba, w_br, w_out, ln1_w, ln1_b, w_up_ffn, conv_w, conv_b, w_down, ln2_w, ln2_b, alpha):
    b, seq, d = h.shape
    ts = _tile_sizes(b, seq)
    aw = HEADS * HEAD_DIM
    kvw = KV_HEADS * HEAD_DIM
    o = np.cumsum([0, aw, kvw, kvw, aw, HEAD_DIM, HEADS, 3 * aw + LORA_W, d, d])
    col = lambda i: w_in[:, o[i]:o[i + 1]]
    head = lambda w, hh: w[:, hh * HEAD_DIM:(hh + 1) * HEAD_DIM]
    zero = jnp.zeros((d, HEAD_DIM), w_in.dtype)
    k_cols = [blk for hh in range(KV_HEADS)
              for blk in (head(col(1), hh), zero, zero, head(col(1), hh))]
    wi_pad = jnp.pad(col(5), ((0, 0), (0, LANES - HEADS)))
    wa = jnp.concatenate([col(0)] + k_cols + [col(2), col(3), col(4), col(4), wi_pad],
                         axis=1).astype(BF16)
    wr = col(6).astype(BF16)
    row2 = lambda v: v.reshape(1, -1)
    lnw2 = row2(jnp.concatenate([idx_w, idx_w]))
    lnb2 = row2(jnp.concatenate([idx_b, idx_b]))

    x2 = h.reshape(b * seq, d)
    q, k4, vt, qi, ki2, wi, pr = _inproj(x2, wa, wr, lnw2, lnb2, ts["tm_proj"], seq)
    r3 = lambda a: a.reshape(b, seq, a.shape[-1])

    y_attn = _attention(r3(q), r3(qi), r3(wi), r3(k4), vt, r3(ki2), ts["tq"])

    ld, la = w_up.shape[0], a_up.shape[0]
    pad_rows = lambda w, start: jnp.pad(w, ((start, LORA_W - start - w.shape[0]), (0, 0)))
    vecs = [row2(v) for v in (mu, w0, a0, k_k, k_a, r_k.reshape(-1), gn_w, gn_b)]
    mats = [pad_rows(w_up, 0), pad_rows(a_up, ld), pad_rows(g_up, ld + la)]
    y_rwkv = _rwkv(r3(pr), vecs, mats, ts["chunk"], ts["tl_rwkv"], ts["nb_rwkv"])

    h1 = _merge(x2, y_attn.reshape(b * seq, aw), y_rwkv.reshape(b * seq, aw),
                w_in[:, o[7]:o[9]].astype(BF16),
                w_ba.astype(BF16), w_br.astype(BF16), w_out.astype(BF16),
                row2(ln1_w), row2(ln1_b), ts["tm_merge"], alpha)

    out = _ffn(h1, w_up_ffn.astype(BF16), conv_w, row2(conv_b), w_down.astype(BF16),
               row2(ln2_w), row2(ln2_b), ts["tm_ffn"], seq, alpha)
    return out.reshape(b, seq, d)


def kernel(x, w_in, idx_k_norm_w, idx_k_norm_b, rwkv_mu, rwkv_w0, rwkv_w_up, rwkv_a0, rwkv_a_up,
           rwkv_g_up, rwkv_k_k, rwkv_k_a, rwkv_r_k, rwkv_ln_w, rwkv_ln_b, w_branch_attn,
           w_branch_rwkv, w_out, ln1_w, ln1_b, w_up, conv_w, conv_b, w_down, ln2_w, ln2_b):
    depth = w_in.shape[0]
    alpha = (2.0 * depth) ** 0.25
    h = x
    for l in range(depth):
        h = _layer(h, w_in[l], idx_k_norm_w[l], idx_k_norm_b[l], rwkv_mu[l], rwkv_w0[l],
                   rwkv_w_up[l], rwkv_a0[l], rwkv_a_up[l], rwkv_g_up[l], rwkv_k_k[l], rwkv_k_a[l],
                   rwkv_r_k[l], rwkv_ln_w[l], rwkv_ln_b[l], w_branch_attn[l], w_branch_rwkv[l],
                   w_out[l], ln1_w[l], ln1_b[l], w_up[l], conv_w[l], conv_b[l], w_down[l],
                   ln2_w[l], ln2_b[l], alpha)
    return h
```

```python
import functools

import jax
import jax.numpy as jnp
import ml_dtypes
import numpy as np
from jax import lax
from jax.experimental import pallas as pl
from jax.experimental.pallas import tpu as pltpu

F32 = jnp.float32
BF16 = jnp.bfloat16
I32 = jnp.int32

LANES = 128
HEAD_DIM = 64
HEADS = 8
KV_HEADS = 2
TOPK_MAX = 256
LORA_W = 128
LN_EPS = 1e-5
GN_EPS = 64e-5
ATTN_CLASSES = 4
BISECT_PASSES = 16
BISECT_ROWS = 256
BISECT_UNROLL = 4
VMEM_LIMIT = 48 * 1024 * 1024


def _nt(a, b):
    return lax.dot_general(a, b, (((1,), (1,)), ((), ())), preferred_element_type=F32)


def _nn(a, b):
    return lax.dot_general(a, b, (((1,), (0,)), ((), ())), preferred_element_type=F32)


def _bnt(a, b):
    return lax.dot_general(a, b, (((2,), (2,)), ((0,), (0,))), preferred_element_type=F32)


def _bnn(a, b):
    return lax.dot_general(a, b, (((2,), (1,)), ((0,), (0,))), preferred_element_type=F32)


LOG2E = 1.4426950408889634
POS_SPLIT = 16
SLOPE_TERMS = 3


def _inproj_kernel(x_ref, wa_ref, wr_ref, lnw_ref, lnb_ref,
                   q_ref, k4_ref, vt_ref, qi_ref, ki2_ref, wi_ref, pr_ref, *, seq):
    tm = x_ref.shape[0]
    xb = x_ref[...].astype(BF16)
    pa = _nn(xb, wa_ref[...])
    q_ref[...] = (pa[:, 0:512] * (HEAD_DIM ** -0.5 * LOG2E)).astype(BF16)
    pos = (pl.program_id(0) * tm) % seq + lax.broadcasted_iota(I32, (tm, 2 * LANES), 0)
    s_lo = pos & (POS_SPLIT - 1)
    ln = lax.broadcasted_iota(I32, (tm, 2 * LANES), 1)
    a = jnp.where(ln < LANES, ln - HEAD_DIM, ln - LANES)
    in_aug = (a >= 0) & (a < 2 * SLOPE_TERMS)
    aug = jnp.where(in_aug, jnp.where((a & 1) == 0, pos - s_lo, s_lo), 0).astype(F32)
    for g in range(KV_HEADS):
        sl = slice(512 + g * 2 * LANES, 512 + (g + 1) * 2 * LANES)
        k4_ref[:, g * 2 * LANES:(g + 1) * 2 * LANES] = (pa[:, sl] + aug).astype(BF16)
    vt = pa[:, 1024:1152].T
    vt_swapped = pltpu.roll(vt, HEAD_DIM, 0)
    vrow = lax.broadcasted_iota(I32, vt.shape, 0)
    top = vrow < HEAD_DIM
    one_at = lambda r: jnp.where(vrow == r, 1.0, 0.0)
    for g, (v_top, v_bottom) in enumerate(((vt, vt_swapped), (vt_swapped, vt))):
        vt_ref[(2 * g) * LANES:(2 * g + 1) * LANES, :] = jnp.where(top, v_top, one_at(HEAD_DIM)).astype(BF16)
        vt_ref[(2 * g + 1) * LANES:(2 * g + 2) * LANES, :] = jnp.where(top, one_at(0), v_bottom).astype(BF16)
    qi_ref[...] = pa[:, 1152:1664].astype(BF16)
    ki = pa[:, 1664:1792]
    mu = jnp.mean(ki, axis=-1, keepdims=True)
    var = jnp.mean(jnp.square(ki - mu), axis=-1, keepdims=True)
    ki2_ref[...] = ((ki - mu) * lax.rsqrt(var + LN_EPS) * lnw_ref[...] + lnb_ref[...]).astype(BF16)
    wi_ref[...] = pa[:, 1792:1920]
    pr_ref[...] = _nn(xb, wr_ref[...])


def _inproj(x2, wa, wr, lnw2, lnb2, tm, seq):
    m, d = x2.shape
    assert seq <= 256 * POS_SPLIT
    full = lambda a: pl.BlockSpec(a.shape, lambda i: (0,) * a.ndim)
    row = lambda n: pl.BlockSpec((tm, n), lambda i: (i, 0))
    outs = [(512, BF16), (512, BF16), None, (512, BF16), (128, BF16), (128, F32),
            (wr.shape[1], F32)]
    vt_rows = 2 * KV_HEADS * LANES
    return pl.pallas_call(
        functools.partial(_inproj_kernel, seq=seq),
        grid=(m // tm,),
        in_specs=[row(d), full(wa), full(wr), full(lnw2), full(lnb2)],
        out_specs=[row(o[0]) if o else pl.BlockSpec((vt_rows, tm), lambda i: (0, i)) for o in outs],
        out_shape=[jax.ShapeDtypeStruct((m, o[0]), o[1]) if o
                   else jax.ShapeDtypeStruct((vt_rows, m), BF16) for o in outs],
        compiler_params=pltpu.CompilerParams(
            dimension_semantics=("parallel",), vmem_limit_bytes=VMEM_LIMIT),
        name="inproj",
    )(x2, wa, wr, lnw2, lnb2)


def _bf16_parts(x):
    parts = []
    for _ in range(SLOPE_TERMS):
        parts.append(float(ml_dtypes.bfloat16(x)))
        x -= parts[-1]
    return parts


def _attn_body(q_ref, qi_ref, wi_ref, k4_ref, vt_ref, ki2_ref, o_ref,
               key_ref, bias_ref, *, t0, t0_max, tq, width, n_sel, cw):
    lane = lax.broadcasted_iota(I32, (tq, LANES), 1)
    lo = lane < HEAD_DIM
    n_cc = width // cw
    n_lc = width // LANES

    def head_lhs(ref, h, fill=None):
        pair = ref[0, :, (h // 2) * LANES:(h // 2 + 1) * LANES]
        fill = jnp.zeros_like(pair) if fill is None else fill
        return jnp.where(lo if h % 2 == 0 else jnp.logical_not(lo), pair, fill)

    wi = wi_ref[0] * (HEAD_DIM ** -0.5 * HEADS ** -0.5)
    row_c = t0 + lax.broadcasted_iota(I32, (tq, cw), 0)
    col_c = lax.broadcasted_iota(I32, (tq, cw), 1)
    for c in range(n_cc):
        kc = ki2_ref[0, c * cw:(c + 1) * cw, :]
        s = jnp.zeros((tq, cw), F32)
        for h in range(HEADS):
            s = s + jnp.maximum(_nt(head_lhs(qi_ref, h), kc), 0.0) * wi[:, h:h + 1]
        key_ref[:, c * cw:(c + 1) * cw] = jnp.where(col_c + c * cw <= row_c, s, -jnp.inf)

    kf = jnp.float32(n_sel)
    rb = min(BISECT_ROWS, tq)
    n_rb = tq // rb
    blocks = [slice(b * rb, (b + 1) * rb) for b in range(n_rb)]
    sls = [slice(c * LANES, (c + 1) * LANES) for c in range(n_lc)]
    n_live = {blk.start: min(n_lc, -(-(t0_max + blk.stop) // LANES)) for blk in blocks}

    def lanes_all(x):
        return jnp.broadcast_to(x, (rb, LANES))

    def count(pred_fn, rows, read=lambda ref, rows, sl: ref[rows, sl], ref=key_ref):
        acc = jnp.zeros((rb, LANES), F32)
        for sl in sls[:n_live[rows.start]]:
            acc = acc + jnp.where(pred_fn(read(ref, rows, sl)), 1.0, 0.0)
        return lanes_all(jnp.sum(acc, axis=1, keepdims=True))

    def smallest(pred_fn, rows):
        acc = jnp.full((rb, LANES), jnp.inf, F32)
        for sl in sls[:n_live[rows.start]]:
            sc = key_ref[rows, sl]
            acc = jnp.minimum(acc, jnp.where(pred_fn(sc), sc, jnp.inf))
        return lanes_all(jnp.min(acc, axis=1, keepdims=True))

    def largest(rows):
        acc = jnp.full((rb, LANES), -jnp.inf, F32)
        for sl in sls[:n_live[rows.start]]:
            acc = jnp.maximum(acc, key_ref[rows, sl])
        return lanes_all(jnp.max(acc, axis=1, keepdims=True))

    def bis_body(_, bounds):
        out = []
        for rows, (low, high) in zip(blocks, bounds):
            mid = 0.5 * low + 0.5 * high
            enough = count(lambda sc: sc >= mid, rows) >= kf
            out.append((jnp.where(enough, mid, low), jnp.where(enough, high, mid)))
        return tuple(out)

    bounds = tuple((smallest(lambda sc: sc > -jnp.inf, rows), largest(rows)) for rows in blocks)
    bounds = lax.fori_loop(0, BISECT_PASSES, bis_body, bounds, unroll=BISECT_UNROLL)
    lows = [low for low, _ in bounds]


    def settle(m, thr, done, rows):
        newly = jnp.where(count(lambda sc: sc > m, rows) < kf, 1.0 - done, 0.0)
        return jnp.where(newly > 0.0, m, thr), jnp.maximum(done, newly)

    walk = []
    for rows, low in zip(blocks, lows):
        m = smallest(lambda sc: sc >= low, rows)
        walk.append((m,) + settle(m, low, jnp.zeros((rb, LANES), F32), rows))

    def walk_pending(state):
        return functools.reduce(jnp.minimum, [jnp.min(done) for _, _, done in state]) < 1.0

    def walk_step(state):
        out = []
        for rows, (m, thr, done) in zip(blocks, state):
            m = smallest(lambda sc: sc > m, rows)
            out.append((m,) + settle(m, thr, done, rows))
        return tuple(out)

    thrs = [thr for _, thr, _ in lax.while_loop(walk_pending, walk_step, tuple(walk))]

    excess = [count(lambda sc: sc >= t, rows) - kf for rows, t in zip(blocks, thrs)]
    has_ties = functools.reduce(jnp.maximum, [jnp.max(e) for e in excess]) > 0.0

    @pl.when(jnp.logical_not(has_ties))
    def _():
        for rows, t in zip(blocks, thrs):
            for sl in sls:
                bias_ref[rows, sl] = jnp.where(key_ref[rows, sl] >= t, 0.0, -jnp.inf)

    @pl.when(has_ties)
    def _():
        lane_rb = lax.broadcasted_iota(I32, (rb, LANES), 1)
        read_pos = lambda ref, rows, sl: lax.bitcast_convert_type(ref[rows, sl], I32)
        nbits = int(np.log2(width - 1)) + 1
        for rows, t in zip(blocks, thrs):
            need = kf - count(lambda kc: kc > t, rows)
            for c, sl in enumerate(sls):
                tpos = jnp.where(key_ref[rows, sl] == t, lane_rb + c * LANES, jnp.int32(2 ** 30))
                bias_ref[rows, sl] = lax.bitcast_convert_type(tpos, F32)

            def tie_body(i, j, rows=rows, need=need):
                cand = j + lax.shift_left(jnp.int32(1), jnp.int32(nbits - 1) - i)
                below = count(lambda pc: pc < cand, rows, read_pos, bias_ref)
                return jnp.where(below < need, cand, j)

            jmax = lax.fori_loop(0, nbits, tie_body, jnp.zeros((rb, LANES), I32))
            for sl in sls:
                keep = jnp.where(read_pos(bias_ref, rows, sl) <= jmax, 0.0, -jnp.inf)
                bias_ref[rows, sl] = jnp.where(key_ref[rows, sl] > t, 0.0, keep)

    bias = bias_ref[:, 0:width]
    a_idx = jnp.where(lo, lane, lane - HEAD_DIM)

    def qk(h):
        parts = _bf16_parts((2.0 ** -(h + 1)) * LOG2E)
        fill = jnp.zeros((tq, LANES), F32)
        for n in reversed(range(SLOPE_TERMS)):
            fill = jnp.where(a_idx < 2 * (n + 1), parts[n], fill)
        kcol = (2 * (h // (HEADS // KV_HEADS)) + h % 2) * LANES
        return _nt(head_lhs(q_ref, h, fill.astype(BF16)), k4_ref[0, 0:width, kcol:kcol + LANES])

    even = None
    qk_next = qk(0)
    for h in range(HEADS):
        g = h // (HEADS // KV_HEADS)
        qk_cur, qk_next = qk_next, (qk(h + 1) if h + 1 < HEADS else None)
        logit = qk_cur + bias
        m = jnp.max(logit, axis=1, keepdims=True)
        e = jnp.exp2(logit - m)
        vrows = (2 * g + h % 2) * LANES
        out = _nt(vt_ref[vrows:vrows + LANES, 0:width], e.astype(BF16)).T
        if h % 2 == 0:
            even = out
        else:
            p = h // 2
            denom = jnp.where(lo, jnp.broadcast_to(even[:, HEAD_DIM:HEAD_DIM + 1], (tq, LANES)),
                              jnp.broadcast_to(out[:, 0:1], (tq, LANES)))
            o_ref[0, :, p * LANES:(p + 1) * LANES] = (jnp.where(lo, even, out) / denom).astype(o_ref.dtype)


def _attn_kernel(q_ref, qi_ref, wi_ref, k4_ref, vt_ref, ki2_ref, o_ref,
                 key_ref, bias_ref, *, tq, seq, n_sel, n_classes):
    i = pl.program_id(0)
    tiles_per_class = (seq // tq) // n_classes
    for cls in range(n_classes):
        width = (cls + 1) * tiles_per_class * tq

        @pl.when((i >= cls * tiles_per_class) & (i < (cls + 1) * tiles_per_class))
        def _():
            _attn_body(q_ref, qi_ref, wi_ref, k4_ref, vt_ref, ki2_ref, o_ref, key_ref, bias_ref,
                       t0=i * tq, t0_max=width - tq, tq=tq, width=width, n_sel=n_sel,
                       cw=512 if width % 512 == 0 else 256)


def _attention(q, qi, wi, k4, vt, ki2, tq):
    b, seq, _ = q.shape
    n_sel = min(TOPK_MAX, seq // 4)
    n_classes = min(ATTN_CLASSES, seq // tq)
    qblk = lambda n: pl.BlockSpec((1, tq, n), lambda i, bi: (bi, i, 0))
    kblk = lambda n: pl.BlockSpec((1, seq, n), lambda i, bi: (bi, 0, 0))
    vblk = pl.BlockSpec((vt.shape[0], seq), lambda i, bi: (0, bi))
    return pl.pallas_call(
        functools.partial(_attn_kernel, tq=tq, seq=seq, n_sel=n_sel, n_classes=n_classes),
        grid=(seq // tq, b),
        in_specs=[qblk(512), qblk(512), qblk(128), kblk(512), vblk, kblk(128)],
        out_specs=qblk(512),
        out_shape=jax.ShapeDtypeStruct((b, seq, 512), BF16),
        scratch_shapes=[pltpu.VMEM((tq, seq), F32), pltpu.VMEM((tq, seq), F32)],
        compiler_params=pltpu.CompilerParams(
            dimension_semantics=("parallel", "parallel"), vmem_limit_bytes=VMEM_LIMIT),
        name="dsa_attention",
    )(q, qi, wi, k4, vt, ki2)


def _bf(x):
    return x.astype(BF16)


def _rwkv_kernel(pr_ref, mu_ref, w0_ref, wup_ref, a0_ref, aup_ref, gup_ref, kk_ref, ka_ref,
                 rk_ref, lnw_ref, lnb_ref, o_ref,
                 state_ref, prev_ref, tinv_ref, w_ref, atrt_ref, bhm_ref, y_ref, *, c):
    nb, tl, width = o_ref.shape
    n_pairs = width // LANES
    n_chunks = tl // c
    rows2 = 2 * c
    n_bp = nb * n_pairs

    @pl.when(pl.program_id(1) == 0)
    def _():
        state_ref[...] = jnp.zeros_like(state_ref)
        prev_ref[...] = jnp.zeros_like(prev_ref)

    ti = lax.broadcasted_iota(I32, (tl, tl), 0)
    tj = lax.broadcasted_iota(I32, (tl, tl), 1)
    same_chunk = (ti // c) == (tj // c)
    tri = jnp.where(same_chunk & (ti >= tj), 1.0, 0.0).astype(BF16)
    blk = jnp.where(same_chunk, 1.0, 0.0).astype(BF16)
    li = lax.broadcasted_iota(I32, (LANES, LANES), 0)
    lj = lax.broadcasted_iota(I32, (LANES, LANES), 1)
    ones_bd = jnp.where((li // HEAD_DIM) == (lj // HEAD_DIM), 1.0, 0.0).astype(BF16)
    upper_strict = li < lj
    upper_incl = li <= lj
    eye = jnp.where(li == lj, 1.0, 0.0)
    lo = lax.broadcasted_iota(I32, (c, LANES), 1) < HEAD_DIM
    rows = lax.broadcasted_iota(I32, (tl, pr_ref.shape[-1]), 0)

    def head_sum(x):
        return _nn(_bf(x), ones_bd)

    def stack(x):
        return _bf(jnp.concatenate([jnp.where(lo, x, 0.0), jnp.where(lo, 0.0, x)], axis=0))

    n_pc = n_chunks * n_bp
    atrt_l, btkt_l, vst_l, kh_l, bh_l = ([None] * n_pc for _ in range(5))
    kept = []
    for bi in range(nb):
        p = pr_ref[bi]
        shifted = jnp.where(rows == 0, prev_ref[bi, 7:8, :], pltpu.roll(p, 1, 0))
        prev_ref[bi] = p[tl - 8:tl, :]
        ps = p + (shifted - p) * mu_ref[...]
        r = ps[:, 0:width]
        k = ps[:, width:2 * width]
        v = ps[:, 2 * width:3 * width]
        lora = ps[:, 3 * width:3 * width + LORA_W]
        nz = -(w0_ref[...] + _nn(_bf(jnp.tanh(lora)), wup_ref[...]))
        log_w = -(jnp.maximum(nz, 0.0) + jnp.log(1.0 + jnp.exp(-jnp.abs(nz)))) - 0.5
        lw = -jnp.exp(log_w)
        a_sig = jax.nn.sigmoid(a0_ref[...] + _nn(_bf(lora), aup_ref[...]))
        gate = _nn(_bf(jax.nn.sigmoid(lora)), gup_ref[...])
        lw_hi = _bf(lw)
        lw_lo = _bf(lw - lw_hi.astype(F32))
        cum = _nn(tri, lw_hi) + _nn(tri, lw_lo)
        tot = _nn(blk, lw_hi) + _nn(blk, lw_lo)
        kept.append((r, k, v, a_sig, gate, tot))
        for pi in range(n_pairs):
            sl = slice(pi * LANES, (pi + 1) * LANES)
            r_p, k_p, v_p, a_p = r[:, sl], k[:, sl], v[:, sl], a_sig[:, sl]
            kk = k_p * kk_ref[:, sl]
            kk = kk * lax.rsqrt(jnp.maximum(head_sum(kk * kk), 1e-24))
            kmod = k_p * (1.0 + (a_p - 1.0) * ka_ref[:, sl])
            bvec = kk * a_p
            cum_p, tot_p = cum[:, sl], tot[:, sl]
            p_inv = jnp.exp(-cum_p)
            p_end = jnp.exp(tot_p - cum_p)
            at_all = -kk * jnp.exp(cum_p - lw[:, sl])
            rt_all = r_p * jnp.exp(cum_p)
            bt_all, kt_all = bvec * p_inv, kmod * p_inv
            bh_all, kh_all = bvec * p_end, kmod * p_end
            for ci in range(n_chunks):
                rs = slice(ci * c, (ci + 1) * c)
                idx = (ci * nb + bi) * n_pairs + pi
                atrt_l[idx] = jnp.concatenate([stack(at_all[rs]), stack(rt_all[rs])], axis=0)
                btkt_l[idx] = jnp.concatenate([stack(bt_all[rs]), stack(kt_all[rs])], axis=0)
                vst_l[idx] = _bf(jnp.concatenate([jnp.where(lo, v_p[rs], 0.0),
                                                  jnp.where(lo, 0.0, v_p[rs])], axis=0).T)
                kh_l[idx] = stack(kh_all[rs])
                bh_l[idx] = stack(bh_all[rs])

    atrt = jnp.stack(atrt_l)
    gt = _bnt(jnp.stack(btkt_l), atrt)
    n_t = jnp.where(upper_strict, gt[:, 0:rows2, 0:rows2], 0.0)
    m_rb_t = jnp.where(upper_incl, gt[:, 0:rows2, rows2:], 0.0)
    m_ak_t = jnp.where(upper_strict, gt[:, rows2:, 0:rows2], 0.0)
    m_rk_t = jnp.where(upper_incl, gt[:, rows2:, rows2:], 0.0)
    tinv = eye + n_t
    npow = _bf(n_t)
    for _ in range(int(np.log2(c)) - 1):
        npow = _bf(_bnn(npow, npow))
        tinv = tinv + _bnn(_bf(tinv), npow)
    rhs = jnp.concatenate([_bf(m_ak_t), jnp.stack(kh_l), _bf(m_rk_t)], axis=2)
    w_ref[...] = _bnn(jnp.stack(vst_l), rhs)
    tinv_ref[...] = _bf(tinv)
    atrt_ref[...] = atrt
    bhm_ref[...] = jnp.concatenate([jnp.stack(bh_l), _bf(m_rb_t)], axis=2)

    for ci in range(n_chunks):
        rs = slice(ci * c, (ci + 1) * c)
        ids = slice(ci * n_bp, (ci + 1) * n_bp)
        state = state_ref[...]
        x = _bnt(_bf(state), atrt_ref[ids])
        w = w_ref[ids]
        us_t = _bnn(_bf(x[:, :, 0:rows2] + w[:, :, 0:rows2]), tinv_ref[ids])
        z = _bnn(_bf(us_t), bhm_ref[ids])
        decay_c = jnp.exp(jnp.stack([kept[bi][5][ci * c:ci * c + 1, pi * LANES:(pi + 1) * LANES]
                                     for bi in range(nb) for pi in range(n_pairs)]))
        state_ref[...] = state * decay_c + z[:, :, 0:rows2] + w[:, :, rows2:2 * rows2]
        ys_t = x[:, :, rows2:] + z[:, :, rows2:] + w[:, :, 2 * rows2:]
        for bi in range(nb):
            for pi in range(n_pairs):
                ys = ys_t[bi * n_pairs + pi].T
                y_ref[bi, rs, pi * LANES:(pi + 1) * LANES] = ys[0:c, :] + ys[c:rows2, :]

    for bi in range(nb):
        r, k, v, a_sig, gate, _ = kept[bi]
        for pi in range(n_pairs):
            sl = slice(pi * LANES, (pi + 1) * LANES)
            y = y_ref[bi, :, sl]
            mean = head_sum(y) * (1.0 / HEAD_DIM)
            yc = y - mean
            var = head_sum(yc * yc) * (1.0 / HEAD_DIM)
            yn = yc * lax.rsqrt(var + GN_EPS) * lnw_ref[:, sl] + lnb_ref[:, sl]
            kmod = k[:, sl] * (1.0 + (a_sig[:, sl] - 1.0) * ka_ref[:, sl])
            bonus = head_sum(r[:, sl] * kmod * rk_ref[:, sl]) * v[:, sl]
            o_ref[bi, :, sl] = ((yn + bonus) * gate[:, sl]).astype(o_ref.dtype)


def _rwkv(pr, vecs, mats, c, tl, nb):
    b, seq, pw = pr.shape
    width = (pw - LORA_W) // 3
    n_pairs = width // LANES
    n_pc = (tl // c) * nb * n_pairs
    full = lambda a: pl.BlockSpec(a.shape, lambda bi, i: (0,) * a.ndim)
    mu, w0, a0, kk, ka, rk, lnw, lnb = vecs
    wup, aup, gup = [m.astype(BF16) for m in mats]
    return pl.pallas_call(
        functools.partial(_rwkv_kernel, c=c),
        grid=(b // nb, seq // tl),
        in_specs=[pl.BlockSpec((nb, tl, pw), lambda bi, i: (bi, i, 0)),
                  full(mu), full(w0), full(wup), full(a0), full(aup), full(gup),
                  full(kk), full(ka), full(rk), full(lnw), full(lnb)],
        out_specs=pl.BlockSpec((nb, tl, width), lambda bi, i: (bi, i, 0)),
        out_shape=jax.ShapeDtypeStruct((b, seq, width), BF16),
        scratch_shapes=[pltpu.VMEM((nb * n_pairs, LANES, LANES), F32),
                        pltpu.VMEM((nb, 8, pw), F32),
                        pltpu.VMEM((n_pc, 2 * c, 2 * c), BF16),
                        pltpu.VMEM((n_pc, 2 * c, 6 * c), F32),
                        pltpu.VMEM((n_pc, 4 * c, LANES), BF16),
                        pltpu.VMEM((n_pc, 2 * c, 4 * c), BF16),
                        pltpu.VMEM((nb, tl, width), F32)],
        compiler_params=pltpu.CompilerParams(
            dimension_semantics=("parallel", "arbitrary"), vmem_limit_bytes=VMEM_LIMIT),
        name="rwkv7",
    )(pr, mu, w0, wup, a0, aup, gup, kk, ka, rk, lnw, lnb)


def _layer_norm(z, w, b):
    mu = jnp.mean(z, axis=-1, keepdims=True)
    zc = z - mu
    var = jnp.mean(zc * zc, axis=-1, keepdims=True)
    return zc * lax.rsqrt(var + LN_EPS) * w + b


def _merge_kernel(x_ref, ya_ref, yr_ref, wg_ref, wa_ref, wb_ref, wo_ref, lnw_ref, lnb_ref,
                  o_ref, *, alpha):
    x = x_ref[...]
    d = x.shape[-1]
    xb = x.astype(BF16)
    mix = (jax.nn.sigmoid(_nn(xb, wg_ref[:, 0:d])) * _nn(ya_ref[...], wa_ref[...])
           + jax.nn.sigmoid(_nn(xb, wg_ref[:, d:2 * d])) * _nn(yr_ref[...], wb_ref[...]))
    z = alpha * x + _nn(mix.astype(BF16), wo_ref[...])
    o_ref[...] = _layer_norm(z, lnw_ref[...], lnb_ref[...])


def _merge(x2, ya, yr, wg, wa, wb, wo, lnw, lnb, tm, alpha):
    m, d = x2.shape
    full = lambda a: pl.BlockSpec(a.shape, lambda i: (0,) * a.ndim)
    row = lambda n: pl.BlockSpec((tm, n), lambda i: (i, 0))
    return pl.pallas_call(
        functools.partial(_merge_kernel, alpha=alpha),
        grid=(m // tm,),
        in_specs=[row(d), row(ya.shape[1]), row(yr.shape[1]),
                  full(wg), full(wa), full(wb), full(wo), full(lnw), full(lnb)],
        out_specs=row(d),
        out_shape=jax.ShapeDtypeStruct((m, d), F32),
        compiler_params=pltpu.CompilerParams(
            dimension_semantics=("parallel",), vmem_limit_bytes=VMEM_LIMIT),
        name="merge_out_ln",
    )(x2, ya, yr, wg, wa, wb, wo, lnw, lnb)


SUBLANES = 8
FFN_COLS = 256
FFN_VMEM_LIMIT = 56 * 1024 * 1024


def _ffn_kernel(h_ref, wu_ref, cw_ref, cb_ref, wd_ref, lnw_ref, lnb_ref, o_ref,
                prev_ref, act_ref, *, alpha, tiles_per_seq):
    tm = h_ref.shape[0]
    dff = wd_ref.shape[0]
    h = h_ref[...]
    hb = h.astype(BF16)

    @pl.when(pl.program_id(0) % tiles_per_seq == 0)
    def _():
        prev_ref[...] = jnp.zeros_like(prev_ref)

    def conv_proj(cols):
        u = _nn(hb, wu_ref[:, cols])
        ext = jnp.concatenate([prev_ref[:, cols], u], axis=0)
        prev_ref[:, cols] = u[tm - SUBLANES:, :]
        w = cw_ref[:, cols]
        back1 = pltpu.roll(ext, 1, 0)[SUBLANES:]
        back2 = pltpu.roll(ext, 2, 0)[SUBLANES:]
        return w[0:1] * back2 + w[1:2] * back1 + w[2:3] * u + cb_ref[:, cols]

    for c in range(dff // FFN_COLS):
        gate = conv_proj(slice(c * FFN_COLS, (c + 1) * FFN_COLS))
        up = conv_proj(slice(dff + c * FFN_COLS, dff + (c + 1) * FFN_COLS))
        act_ref[:, c * FFN_COLS:(c + 1) * FFN_COLS] = (gate * jax.nn.sigmoid(gate) * up).astype(BF16)
    o_ref[...] = _layer_norm(alpha * h + _nn(act_ref[...], wd_ref[...]), lnw_ref[...], lnb_ref[...])


def _ffn(h1, wu, cw, cb, wd, lnw, lnb, tm, seq, alpha):
    m, d = h1.shape
    dff = wd.shape[0]
    assert dff % FFN_COLS == 0 and seq % tm == 0
    const = lambda a: pl.BlockSpec(a.shape, lambda i: (0,) * a.ndim, pipeline_mode=pl.Buffered(1))
    return pl.pallas_call(
        functools.partial(_ffn_kernel, alpha=alpha, tiles_per_seq=seq // tm),
        grid=(m // tm,),
        in_specs=[pl.BlockSpec((tm, d), lambda i: (i, 0))]
        + [const(a) for a in (wu, cw, cb, wd, lnw, lnb)],
        out_specs=pl.BlockSpec((tm, d), lambda i: (i, 0)),
        out_shape=jax.ShapeDtypeStruct((m, d), F32),
        scratch_shapes=[pltpu.VMEM((SUBLANES, 2 * dff), F32), pltpu.VMEM((tm, dff), BF16)],
        compiler_params=pltpu.CompilerParams(
            dimension_semantics=("arbitrary",), vmem_limit_bytes=FFN_VMEM_LIMIT),
        name="conv_ffn_ln",
    )(h1, wu, cw, cb, wd, lnw, lnb)


def _tile_sizes(batch, seq):
    return dict(tm_proj=min(512, seq), tq=min(256, seq), chunk=64, tl_rwkv=min(256, seq),
                nb_rwkv=2 if batch % 2 == 0 else 1, tm_merge=min(512, seq), tm_ffn=min(512, seq))


def _layer(h, w_in, idx_w, idx_b, mu, w0, w_up, a0, a_up, g_up, k_k, k_a, r_k, gn_w, gn_b,
           w_ba, w_br, w_out, ln1_w, ln1_b, w_up_ffn, conv_w, conv_b, w_down, ln2_w, ln2_b, alpha):
    b, seq, d = h.shape
    ts = _tile_sizes(b, seq)
    aw = HEADS * HEAD_DIM
    kvw = KV_HEADS * HEAD_DIM
    o = np.cumsum([0, aw, kvw, kvw, aw, HEAD_DIM, HEADS, 3 * aw + LORA_W, d, d])
    col = lambda i: w_in[:, o[i]:o[i + 1]]
    head = lambda w, hh: w[:, hh * HEAD_DIM:(hh + 1) * HEAD_DIM]
    zero = jnp.zeros((d, HEAD_DIM), w_in.dtype)
    k_cols = [blk for hh in range(KV_HEADS)
              for blk in (head(col(1), hh), zero, zero, head(col(1), hh))]
    wi_pad = jnp.pad(col(5), ((0, 0), (0, LANES - HEADS)))
    wa = jnp.concatenate([col(0)] + k_cols + [col(2), col(3), col(4), col(4), wi_pad],
                         axis=1).astype(BF16)
    wr = col(6).astype(BF16)
    row2 = lambda v: v.reshape(1, -1)
    lnw2 = row2(jnp.concatenate([idx_w, idx_w]))
    lnb2 = row2(jnp.concatenate([idx_b, idx_b]))

    x2 = h.reshape(b * seq, d)
    q, k4, vt, qi, ki2, wi, pr = _inproj(x2, wa, wr, lnw2, lnb2, ts["tm_proj"], seq)
    r3 = lambda a: a.reshape(b, seq, a.shape[-1])

    y_attn = _attention(r3(q), r3(qi), r3(wi), r3(k4), vt, r3(ki2), ts["tq"])

    ld, la = w_up.shape[0], a_up.shape[0]
    pad_rows = lambda w, start: jnp.pad(w, ((start, LORA_W - start - w.shape[0]), (0, 0)))
    vecs = [row2(v) for v in (mu, w0, a0, k_k, k_a, r_k.reshape(-1), gn_w, gn_b)]
    mats = [pad_rows(w_up, 0), pad_rows(a_up, ld), pad_rows(g_up, ld + la)]
    y_rwkv = _rwkv(r3(pr), vecs, mats, ts["chunk"], ts["tl_rwkv"], ts["nb_rwkv"])

    h1 = _merge(x2, y_attn.reshape(b * seq, aw), y_rwkv.reshape(b * seq, aw),
                w_in[:, o[7]:o[9]].astype(BF16),
                w_ba.astype(BF16), w_br.astype(BF16), w_out.astype(BF16),
                row2(ln1_w), row2(ln1_b), ts["tm_merge"], alpha)

    out = _ffn(h1, w_up_ffn.astype(BF16), conv_w, row2(conv_b), w_down.astype(BF16),
               row2(ln2_w), row2(ln2_b), ts["tm_ffn"], seq, alpha)
    return out.reshape(b, seq, d)


def kernel(x, w_in, idx_k_norm_w, idx_k_norm_b, rwkv_mu, rwkv_w0, rwkv_w_up, rwkv_a0, rwkv_a_up,
           rwkv_g_up, rwkv_k_k, rwkv_k_a, rwkv_r_k, rwkv_ln_w, rwkv_ln_b, w_branch_attn,
           w_branch_rwkv, w_out, ln1_w, ln1_b, w_up, conv_w, conv_b, w_down, ln2_w, ln2_b):
    depth = w_in.shape[0]
    alpha = (2.0 * depth) ** 0.25
    h = x
    for l in range(depth):
        h = _layer(h, w_in[l], idx_k_norm_w[l], idx_k_norm_b[l], rwkv_mu[l], rwkv_w0[l],
                   rwkv_w_up[l], rwkv_a0[l], rwkv_a_up[l], rwkv_g_up[l], rwkv_k_k[l], rwkv_k_a[l],
                   rwkv_r_k[l], rwkv_ln_w[l], rwkv_ln_b[l], w_branch_attn[l], w_branch_rwkv[l],
                   w_out[l], ln1_w[l], ln1_b[l], w_up[l], conv_w[l], conv_b[l], w_down[l],
                   ln2_w[l], ln2_b[l], alpha)
    return h
```

```python
import functools

import jax
import jax.numpy as jnp
import ml_dtypes
import numpy as np
from jax import lax
from jax.experimental import pallas as pl
from jax.experimental.pallas import tpu as pltpu

F32 = jnp.float32
BF16 = jnp.bfloat16
I32 = jnp.int32

LANES = 128
HEAD_DIM = 64
HEADS = 8
KV_HEADS = 2
TOPK_MAX = 256
LORA_W = 128
LN_EPS = 1e-5
GN_EPS = 64e-5
ATTN_CLASSES = 4
SCORE_COLS = 512
BISECT_PASSES = 16
BISECT_ROWS = 256
BISECT_UNROLL = 4
VMEM_LIMIT = 48 * 1024 * 1024


def _nt(a, b):
    return lax.dot_general(a, b, (((1,), (1,)), ((), ())), preferred_element_type=F32)


def _nn(a, b):
    return lax.dot_general(a, b, (((1,), (0,)), ((), ())), preferred_element_type=F32)


def _bnt(a, b):
    return lax.dot_general(a, b, (((2,), (2,)), ((0,), (0,))), preferred_element_type=F32)


def _bnn(a, b):
    return lax.dot_general(a, b, (((2,), (1,)), ((0,), (0,))), preferred_element_type=F32)


LOG2E = 1.4426950408889634
POS_SPLIT = 16
SLOPE_TERMS = 3
ATTN_W = HEADS * HEAD_DIM
K4_W = 2 * KV_HEADS * LANES
_QKV_COLS = np.cumsum([0, ATTN_W, K4_W, KV_HEADS * HEAD_DIM, ATTN_W, LANES, LANES])


def _inproj_kernel(x_ref, wa_ref, wr_ref, lnw_ref, lnb_ref,
                   q_ref, k4_ref, vt_ref, qi_ref, ki2_ref, wi_ref, pr_ref, *, seq):
    tm = x_ref.shape[0]
    xb = x_ref[...].astype(BF16)
    pa = _nn(xb, wa_ref[...])
    part = lambda i: pa[:, _QKV_COLS[i]:_QKV_COLS[i + 1]]
    q_ref[...] = (part(0) * (HEAD_DIM ** -0.5 * LOG2E)).astype(BF16)
    pos = (pl.program_id(0) * tm) % seq + lax.broadcasted_iota(I32, (tm, 2 * LANES), 0)
    s_lo = pos & (POS_SPLIT - 1)
    ln = lax.broadcasted_iota(I32, (tm, 2 * LANES), 1)
    a = jnp.where(ln < LANES, ln - HEAD_DIM, ln - LANES)
    in_aug = (a >= 0) & (a < 2 * SLOPE_TERMS)
    aug = jnp.where(in_aug, jnp.where((a & 1) == 0, pos - s_lo, s_lo), 0).astype(F32)
    keys = part(1)
    for g in range(KV_HEADS):
        sl = slice(g * 2 * LANES, (g + 1) * 2 * LANES)
        k4_ref[:, sl] = (keys[:, sl] + aug).astype(BF16)
    vt = part(2).T
    vt_swapped = pltpu.roll(vt, HEAD_DIM, 0)
    vrow = lax.broadcasted_iota(I32, vt.shape, 0)
    top = vrow < HEAD_DIM
    one_at = lambda r: jnp.where(vrow == r, 1.0, 0.0)
    for g, (v_top, v_bottom) in enumerate(((vt, vt_swapped), (vt_swapped, vt))):
        vt_ref[(2 * g) * LANES:(2 * g + 1) * LANES, :] = jnp.where(top, v_top, one_at(HEAD_DIM)).astype(BF16)
        vt_ref[(2 * g + 1) * LANES:(2 * g + 2) * LANES, :] = jnp.where(top, one_at(0), v_bottom).astype(BF16)
    qi_ref[...] = part(3).astype(BF16)
    ki = part(4)
    mu = jnp.mean(ki, axis=-1, keepdims=True)
    var = jnp.mean(jnp.square(ki - mu), axis=-1, keepdims=True)
    ki2_ref[...] = ((ki - mu) * lax.rsqrt(var + LN_EPS) * lnw_ref[...] + lnb_ref[...]).astype(BF16)
    wi_ref[...] = part(5)
    pr_ref[...] = _nn(xb, wr_ref[...])


def _inproj(x2, wa, wr, lnw2, lnb2, tm, seq):
    m, d = x2.shape
    assert seq <= 2 ** 8 * POS_SPLIT and KV_HEADS == 2 and wa.shape[1] == _QKV_COLS[-1]
    full = lambda a: pl.BlockSpec(a.shape, lambda i: (0,) * a.ndim)
    row = lambda n: pl.BlockSpec((tm, n), lambda i: (i, 0))
    outs = [(ATTN_W, BF16), (K4_W, BF16), None, (ATTN_W, BF16), (LANES, BF16), (LANES, F32),
            (wr.shape[1], F32)]
    vt_rows = 2 * KV_HEADS * LANES
    return pl.pallas_call(
        functools.partial(_inproj_kernel, seq=seq),
        grid=(m // tm,),
        in_specs=[row(d), full(wa), full(wr), full(lnw2), full(lnb2)],
        out_specs=[row(o[0]) if o else pl.BlockSpec((vt_rows, tm), lambda i: (0, i)) for o in outs],
        out_shape=[jax.ShapeDtypeStruct((m, o[0]), o[1]) if o
                   else jax.ShapeDtypeStruct((vt_rows, m), BF16) for o in outs],
        compiler_params=pltpu.CompilerParams(
            dimension_semantics=("parallel",), vmem_limit_bytes=VMEM_LIMIT),
        name="inproj",
    )(x2, wa, wr, lnw2, lnb2)


def _bf16_parts(x):
    parts = []
    for _ in range(SLOPE_TERMS):
        parts.append(float(ml_dtypes.bfloat16(x)))
        x -= parts[-1]
    return parts


def _attn_body(q_ref, qi_ref, wi_ref, k4_ref, vt_ref, ki2_ref, o_ref,
               key_ref, bias_ref, *, t0, t0_max, tq, width, n_sel, cw):
    lane = lax.broadcasted_iota(I32, (tq, LANES), 1)
    lo = lane < HEAD_DIM
    n_cc = width // cw
    n_lc = width // LANES

    def head_lhs(ref, h, fill=None):
        pair = ref[0, :, (h // 2) * LANES:(h // 2 + 1) * LANES]
        fill = jnp.zeros_like(pair) if fill is None else fill
        return jnp.where(lo if h % 2 == 0 else jnp.logical_not(lo), pair, fill)

    wi = wi_ref[0] * (HEAD_DIM ** -0.5 * HEADS ** -0.5)
    row_c = t0 + lax.broadcasted_iota(I32, (tq, cw), 0)
    col_c = lax.broadcasted_iota(I32, (tq, cw), 1)
    for c in range(n_cc):
        kc = ki2_ref[0, c * cw:(c + 1) * cw, :]
        s = jnp.zeros((tq, cw), F32)
        for h in range(HEADS):
            s = s + jnp.maximum(_nt(head_lhs(qi_ref, h), kc), 0.0) * wi[:, h:h + 1]
        key_ref[:, c * cw:(c + 1) * cw] = jnp.where(col_c + c * cw <= row_c, s, -jnp.inf)

    kf = jnp.float32(n_sel)
    rb = min(BISECT_ROWS, tq)
    n_rb = tq // rb
    blocks = [slice(b * rb, (b + 1) * rb) for b in range(n_rb)]
    sls = [slice(c * LANES, (c + 1) * LANES) for c in range(n_lc)]
    n_live = {blk.start: min(n_lc, -(-(t0_max + blk.stop) // LANES)) for blk in blocks}

    def lanes_all(x):
        return jnp.broadcast_to(x, (rb, LANES))

    def count(pred_fn, rows, read=lambda ref, rows, sl: ref[rows, sl], ref=key_ref):
        acc = jnp.zeros((rb, LANES), F32)
        for sl in sls[:n_live[rows.start]]:
            acc = acc + jnp.where(pred_fn(read(ref, rows, sl)), 1.0, 0.0)
        return lanes_all(jnp.sum(acc, axis=1, keepdims=True))

    def smallest(pred_fn, rows):
        acc = jnp.full((rb, LANES), jnp.inf, F32)
        for sl in sls[:n_live[rows.start]]:
            sc = key_ref[rows, sl]
            acc = jnp.minimum(acc, jnp.where(pred_fn(sc), sc, jnp.inf))
        return lanes_all(jnp.min(acc, axis=1, keepdims=True))

    def largest(rows):
        acc = jnp.full((rb, LANES), -jnp.inf, F32)
        for sl in sls[:n_live[rows.start]]:
            acc = jnp.maximum(acc, key_ref[rows, sl])
        return lanes_all(jnp.max(acc, axis=1, keepdims=True))

    def bis_body(_, bounds):
        out = []
        for rows, (low, high) in zip(blocks, bounds):
            mid = 0.5 * low + 0.5 * high
            enough = count(lambda sc: sc >= mid, rows) >= kf
            out.append((jnp.where(enough, mid, low), jnp.where(enough, high, mid)))
        return tuple(out)

    bounds = tuple((smallest(lambda sc: sc > -jnp.inf, rows), largest(rows)) for rows in blocks)
    bounds = lax.fori_loop(0, BISECT_PASSES, bis_body, bounds, unroll=BISECT_UNROLL)
    lows = [low for low, _ in bounds]


    def settle(m, thr, done, rows):
        newly = jnp.where(count(lambda sc: sc > m, rows) < kf, 1.0 - done, 0.0)
        return jnp.where(newly > 0.0, m, thr), jnp.maximum(done, newly)

    walk = []
    for rows, low in zip(blocks, lows):
        m = smallest(lambda sc: sc >= low, rows)
        walk.append((m,) + settle(m, low, jnp.zeros((rb, LANES), F32), rows))

    def walk_pending(state):
        return functools.reduce(jnp.minimum, [jnp.min(done) for _, _, done in state]) < 1.0

    def walk_step(state):
        out = []
        for rows, (m, thr, done) in zip(blocks, state):
            m = smallest(lambda sc: sc > m, rows)
            out.append((m,) + settle(m, thr, done, rows))
        return tuple(out)

    thrs = [thr for _, thr, _ in lax.while_loop(walk_pending, walk_step, tuple(walk))]

    excess = [count(lambda sc: sc >= t, rows) - kf for rows, t in zip(blocks, thrs)]
    has_ties = functools.reduce(jnp.maximum, [jnp.max(e) for e in excess]) > 0.0

    @pl.when(jnp.logical_not(has_ties))
    def _():
        for rows, t in zip(blocks, thrs):
            for sl in sls:
                bias_ref[rows, sl] = jnp.where(key_ref[rows, sl] >= t, 0.0, -jnp.inf)

    @pl.when(has_ties)
    def _():
        lane_rb = lax.broadcasted_iota(I32, (rb, LANES), 1)
        read_pos = lambda ref, rows, sl: lax.bitcast_convert_type(ref[rows, sl], I32)
        nbits = int(np.log2(width - 1)) + 1
        for rows, t in zip(blocks, thrs):
            need = kf - count(lambda kc: kc > t, rows)
            for c, sl in enumerate(sls):
                tpos = jnp.where(key_ref[rows, sl] == t, lane_rb + c * LANES, jnp.int32(2 ** 30))
                bias_ref[rows, sl] = lax.bitcast_convert_type(tpos, F32)

            def tie_body(i, j, rows=rows, need=need):
                cand = j + lax.shift_left(jnp.int32(1), jnp.int32(nbits - 1) - i)
                below = count(lambda pc: pc < cand, rows, read_pos, bias_ref)
                return jnp.where(below < need, cand, j)

            jmax = lax.fori_loop(0, nbits, tie_body, jnp.zeros((rb, LANES), I32))
            for sl in sls:
                keep = jnp.where(read_pos(bias_ref, rows, sl) <= jmax, 0.0, -jnp.inf)
                bias_ref[rows, sl] = jnp.where(key_ref[rows, sl] > t, 0.0, keep)

    bias = bias_ref[:, 0:width]
    a_idx = jnp.where(lo, lane, lane - HEAD_DIM)

    def qk(h):
        parts = _bf16_parts((2.0 ** -(h + 1)) * LOG2E)
        fill = jnp.zeros((tq, LANES), F32)
        for n in reversed(range(SLOPE_TERMS)):
            fill = jnp.where(a_idx < 2 * (n + 1), parts[n], fill)
        kcol = (2 * (h // (HEADS // KV_HEADS)) + h % 2) * LANES
        return _nt(head_lhs(q_ref, h, fill.astype(BF16)), k4_ref[0, 0:width, kcol:kcol + LANES])

    even = None
    qk_next = qk(0)
    for h in range(HEADS):
        g = h // (HEADS // KV_HEADS)
        qk_cur, qk_next = qk_next, (qk(h + 1) if h + 1 < HEADS else None)
        logit = qk_cur + bias
        m = jnp.max(logit, axis=1, keepdims=True)
        e = jnp.exp2(logit - m)
        vrows = (2 * g + h % 2) * LANES
        out = _nt(vt_ref[vrows:vrows + LANES, 0:width], e.astype(BF16)).T
        if h % 2 == 0:
            even = out
        else:
            p = h // 2
            denom = jnp.where(lo, jnp.broadcast_to(even[:, HEAD_DIM:HEAD_DIM + 1], (tq, LANES)),
                              jnp.broadcast_to(out[:, 0:1], (tq, LANES)))
            o_ref[0, :, p * LANES:(p + 1) * LANES] = (jnp.where(lo, even, out) / denom).astype(o_ref.dtype)


def _attn_kernel(q_ref, qi_ref, wi_ref, k4_ref, vt_ref, ki2_ref, o_ref,
                 key_ref, bias_ref, *, tq, seq, n_sel, n_classes):
    i = pl.program_id(0)
    tiles_per_class = (seq // tq) // n_classes
    for cls in range(n_classes):
        width = (cls + 1) * tiles_per_class * tq

        @pl.when((i >= cls * tiles_per_class) & (i < (cls + 1) * tiles_per_class))
        def _():
            _attn_body(q_ref, qi_ref, wi_ref, k4_ref, vt_ref, ki2_ref, o_ref, key_ref, bias_ref,
                       t0=i * tq, t0_max=width - tq, tq=tq, width=width, n_sel=n_sel,
                       cw=SCORE_COLS if width % SCORE_COLS == 0 else SCORE_COLS // 2)


def _attention(q, qi, wi, k4, vt, ki2, tq):
    b, seq, _ = q.shape
    n_sel = min(TOPK_MAX, seq // 4)
    n_classes = min(ATTN_CLASSES, seq // tq)
    qblk = lambda n: pl.BlockSpec((1, tq, n), lambda i, bi: (bi, i, 0))
    kblk = lambda n: pl.BlockSpec((1, seq, n), lambda i, bi: (bi, 0, 0))
    vblk = pl.BlockSpec((vt.shape[0], seq), lambda i, bi: (0, bi))
    return pl.pallas_call(
        functools.partial(_attn_kernel, tq=tq, seq=seq, n_sel=n_sel, n_classes=n_classes),
        grid=(seq // tq, b),
        in_specs=[qblk(ATTN_W), qblk(ATTN_W), qblk(LANES), kblk(K4_W), vblk, kblk(LANES)],
        out_specs=qblk(ATTN_W),
        out_shape=jax.ShapeDtypeStruct((b, seq, ATTN_W), BF16),
        scratch_shapes=[pltpu.VMEM((tq, seq), F32), pltpu.VMEM((tq, seq), F32)],
        compiler_params=pltpu.CompilerParams(
            dimension_semantics=("parallel", "parallel"), vmem_limit_bytes=VMEM_LIMIT),
        name="dsa_attention",
    )(q, qi, wi, k4, vt, ki2)


def _bf(x):
    return x.astype(BF16)


def _rwkv_kernel(pr_ref, mu_ref, w0_ref, wup_ref, a0_ref, aup_ref, gup_ref, kk_ref, ka_ref,
                 rk_ref, lnw_ref, lnb_ref, o_ref,
                 state_ref, prev_ref, tinv_ref, w_ref, atrt_ref, bhm_ref, y_ref, *, c):
    nb, tl, width = o_ref.shape
    n_pairs = width // LANES
    n_chunks = tl // c
    rows2 = 2 * c
    n_bp = nb * n_pairs

    @pl.when(pl.program_id(1) == 0)
    def _():
        state_ref[...] = jnp.zeros_like(state_ref)
        prev_ref[...] = jnp.zeros_like(prev_ref)

    ti = lax.broadcasted_iota(I32, (tl, tl), 0)
    tj = lax.broadcasted_iota(I32, (tl, tl), 1)
    same_chunk = (ti // c) == (tj // c)
    tri = jnp.where(same_chunk & (ti >= tj), 1.0, 0.0).astype(BF16)
    blk = jnp.where(same_chunk, 1.0, 0.0).astype(BF16)
    li = lax.broadcasted_iota(I32, (LANES, LANES), 0)
    lj = lax.broadcasted_iota(I32, (LANES, LANES), 1)
    ones_bd = jnp.where((li // HEAD_DIM) == (lj // HEAD_DIM), 1.0, 0.0).astype(BF16)
    upper_strict = li < lj
    upper_incl = li <= lj
    eye = jnp.where(li == lj, 1.0, 0.0)
    lo = lax.broadcasted_iota(I32, (c, LANES), 1) < HEAD_DIM
    rows = lax.broadcasted_iota(I32, (tl, pr_ref.shape[-1]), 0)

    def head_sum(x):
        return _nn(_bf(x), ones_bd)

    def stack(x):
        return _bf(jnp.concatenate([jnp.where(lo, x, 0.0), jnp.where(lo, 0.0, x)], axis=0))

    n_pc = n_chunks * n_bp
    atrt_l, btkt_l, vst_l, kh_l, bh_l = ([None] * n_pc for _ in range(5))
    kept = []
    for bi in range(nb):
        p = pr_ref[bi]
        shifted = jnp.where(rows == 0, prev_ref[bi, 7:8, :], pltpu.roll(p, 1, 0))
        prev_ref[bi] = p[tl - 8:tl, :]
        ps = p + (shifted - p) * mu_ref[...]
        r = ps[:, 0:width]
        k = ps[:, width:2 * width]
        v = ps[:, 2 * width:3 * width]
        lora = ps[:, 3 * width:3 * width + LORA_W]
        nz = -(w0_ref[...] + _nn(_bf(jnp.tanh(lora)), wup_ref[...]))
        log_w = -(jnp.maximum(nz, 0.0) + jnp.log(1.0 + jnp.exp(-jnp.abs(nz)))) - 0.5
        lw = -jnp.exp(log_w)
        a_sig = jax.nn.sigmoid(a0_ref[...] + _nn(_bf(lora), aup_ref[...]))
        gate = _nn(_bf(jax.nn.sigmoid(lora)), gup_ref[...])
        lw_hi = _bf(lw)
        lw_lo = _bf(lw - lw_hi.astype(F32))
        cum = _nn(tri, lw_hi) + _nn(tri, lw_lo)
        tot = _nn(blk, lw_hi) + _nn(blk, lw_lo)
        kept.append((r, k, v, a_sig, gate, tot))
        for pi in range(n_pairs):
            sl = slice(pi * LANES, (pi + 1) * LANES)
            r_p, k_p, v_p, a_p = r[:, sl], k[:, sl], v[:, sl], a_sig[:, sl]
            kk = k_p * kk_ref[:, sl]
            kk = kk * lax.rsqrt(jnp.maximum(head_sum(kk * kk), 1e-24))
            kmod = k_p * (1.0 + (a_p - 1.0) * ka_ref[:, sl])
            bvec = kk * a_p
            cum_p, tot_p = cum[:, sl], tot[:, sl]
            p_inv = jnp.exp(-cum_p)
            p_end = jnp.exp(tot_p - cum_p)
            at_all = -kk * jnp.exp(cum_p - lw[:, sl])
            rt_all = r_p * jnp.exp(cum_p)
            bt_all, kt_all = bvec * p_inv, kmod * p_inv
            bh_all, kh_all = bvec * p_end, kmod * p_end
            for ci in range(n_chunks):
                rs = slice(ci * c, (ci + 1) * c)
                idx = (ci * nb + bi) * n_pairs + pi
                atrt_l[idx] = jnp.concatenate([stack(at_all[rs]), stack(rt_all[rs])], axis=0)
                btkt_l[idx] = jnp.concatenate([stack(bt_all[rs]), stack(kt_all[rs])], axis=0)
                vst_l[idx] = _bf(jnp.concatenate([jnp.where(lo, v_p[rs], 0.0),
                                                  jnp.where(lo, 0.0, v_p[rs])], axis=0).T)
                kh_l[idx] = stack(kh_all[rs])
                bh_l[idx] = stack(bh_all[rs])

    atrt = jnp.stack(atrt_l)
    gt = _bnt(jnp.stack(btkt_l), atrt)
    n_t = jnp.where(upper_strict, gt[:, 0:rows2, 0:rows2], 0.0)
    m_rb_t = jnp.where(upper_incl, gt[:, 0:rows2, rows2:], 0.0)
    m_ak_t = jnp.where(upper_strict, gt[:, rows2:, 0:rows2], 0.0)
    m_rk_t = jnp.where(upper_incl, gt[:, rows2:, rows2:], 0.0)
    tinv = eye + n_t
    npow = _bf(n_t)
    for _ in range(int(np.log2(c)) - 1):
        npow = _bf(_bnn(npow, npow))
        tinv = tinv + _bnn(_bf(tinv), npow)
    rhs = jnp.concatenate([_bf(m_ak_t), jnp.stack(kh_l), _bf(m_rk_t)], axis=2)
    w_ref[...] = _bnn(jnp.stack(vst_l), rhs)
    tinv_ref[...] = _bf(tinv)
    atrt_ref[...] = atrt
    bhm_ref[...] = jnp.concatenate([jnp.stack(bh_l), _bf(m_rb_t)], axis=2)

    for ci in range(n_chunks):
        rs = slice(ci * c, (ci + 1) * c)
        ids = slice(ci * n_bp, (ci + 1) * n_bp)
        state = state_ref[...]
        x = _bnt(_bf(state), atrt_ref[ids])
        w = w_ref[ids]
        us_t = _bnn(_bf(x[:, :, 0:rows2] + w[:, :, 0:rows2]), tinv_ref[ids])
        z = _bnn(_bf(us_t), bhm_ref[ids])
        decay_c = jnp.exp(jnp.stack([kept[bi][5][ci * c:ci * c + 1, pi * LANES:(pi + 1) * LANES]
                                     for bi in range(nb) for pi in range(n_pairs)]))
        state_ref[...] = state * decay_c + z[:, :, 0:rows2] + w[:, :, rows2:2 * rows2]
        ys_t = x[:, :, rows2:] + z[:, :, rows2:] + w[:, :, 2 * rows2:]
        for bi in range(nb):
            for pi in range(n_pairs):
                ys = ys_t[bi * n_pairs + pi].T
                y_ref[bi, rs, pi * LANES:(pi + 1) * LANES] = ys[0:c, :] + ys[c:rows2, :]

    for bi in range(nb):
        r, k, v, a_sig, gate, _ = kept[bi]
        for pi in range(n_pairs):
            sl = slice(pi * LANES, (pi + 1) * LANES)
            y = y_ref[bi, :, sl]
            mean = head_sum(y) * (1.0 / HEAD_DIM)
            yc = y - mean
            var = head_sum(yc * yc) * (1.0 / HEAD_DIM)
            yn = yc * lax.rsqrt(var + GN_EPS) * lnw_ref[:, sl] + lnb_ref[:, sl]
            kmod = k[:, sl] * (1.0 + (a_sig[:, sl] - 1.0) * ka_ref[:, sl])
            bonus = head_sum(r[:, sl] * kmod * rk_ref[:, sl]) * v[:, sl]
            o_ref[bi, :, sl] = ((yn + bonus) * gate[:, sl]).astype(o_ref.dtype)


def _rwkv(pr, vecs, mats, c, tl, nb):
    b, seq, pw = pr.shape
    width = (pw - LORA_W) // 3
    n_pairs = width // LANES
    n_pc = (tl // c) * nb * n_pairs
    full = lambda a: pl.BlockSpec(a.shape, lambda bi, i: (0,) * a.ndim)
    mu, w0, a0, kk, ka, rk, lnw, lnb = vecs
    wup, aup, gup = [m.astype(BF16) for m in mats]
    return pl.pallas_call(
        functools.partial(_rwkv_kernel, c=c),
        grid=(b // nb, seq // tl),
        in_specs=[pl.BlockSpec((nb, tl, pw), lambda bi, i: (bi, i, 0)),
                  full(mu), full(w0), full(wup), full(a0), full(aup), full(gup),
                  full(kk), full(ka), full(rk), full(lnw), full(lnb)],
        out_specs=pl.BlockSpec((nb, tl, width), lambda bi, i: (bi, i, 0)),
        out_shape=jax.ShapeDtypeStruct((b, seq, width), BF16),
        scratch_shapes=[pltpu.VMEM((nb * n_pairs, LANES, LANES), F32),
                        pltpu.VMEM((nb, 8, pw), F32),
                        pltpu.VMEM((n_pc, 2 * c, 2 * c), BF16),
                        pltpu.VMEM((n_pc, 2 * c, 6 * c), F32),
                        pltpu.VMEM((n_pc, 4 * c, LANES), BF16),
                        pltpu.VMEM((n_pc, 2 * c, 4 * c), BF16),
                        pltpu.VMEM((nb, tl, width), F32)],
        compiler_params=pltpu.CompilerParams(
            dimension_semantics=("parallel", "arbitrary"), vmem_limit_bytes=VMEM_LIMIT),
        name="rwkv7",
    )(pr, mu, w0, wup, a0, aup, gup, kk, ka, rk, lnw, lnb)


def _layer_norm(z, w, b):
    mu = jnp.mean(z, axis=-1, keepdims=True)
    zc = z - mu
    var = jnp.mean(zc * zc, axis=-1, keepdims=True)
    return zc * lax.rsqrt(var + LN_EPS) * w + b


def _merge_kernel(x_ref, ya_ref, yr_ref, wg_ref, wa_ref, wb_ref, wo_ref, lnw_ref, lnb_ref,
                  o_ref, *, alpha):
    x = x_ref[...]
    d = x.shape[-1]
    xb = x.astype(BF16)
    mix = (jax.nn.sigmoid(_nn(xb, wg_ref[:, 0:d])) * _nn(ya_ref[...], wa_ref[...])
           + jax.nn.sigmoid(_nn(xb, wg_ref[:, d:2 * d])) * _nn(yr_ref[...], wb_ref[...]))
    z = alpha * x + _nn(mix.astype(BF16), wo_ref[...])
    o_ref[...] = _layer_norm(z, lnw_ref[...], lnb_ref[...])


def _merge(x2, ya, yr, wg, wa, wb, wo, lnw, lnb, tm, alpha):
    m, d = x2.shape
    full = lambda a: pl.BlockSpec(a.shape, lambda i: (0,) * a.ndim)
    row = lambda n: pl.BlockSpec((tm, n), lambda i: (i, 0))
    return pl.pallas_call(
        functools.partial(_merge_kernel, alpha=alpha),
        grid=(m // tm,),
        in_specs=[row(d), row(ya.shape[1]), row(yr.shape[1]),
                  full(wg), full(wa), full(wb), full(wo), full(lnw), full(lnb)],
        out_specs=row(d),
        out_shape=jax.ShapeDtypeStruct((m, d), F32),
        compiler_params=pltpu.CompilerParams(
            dimension_semantics=("parallel",), vmem_limit_bytes=VMEM_LIMIT),
        name="merge_out_ln",
    )(x2, ya, yr, wg, wa, wb, wo, lnw, lnb)


SUBLANES = 8
FFN_COLS = 256
FFN_VMEM_LIMIT = 56 * 1024 * 1024


def _ffn_kernel(h_ref, wu_ref, cw_ref, cb_ref, wd_ref, lnw_ref, lnb_ref, o_ref,
                prev_ref, act_ref, *, alpha, tiles_per_seq):
    tm = h_ref.shape[0]
    dff = wd_ref.shape[0]
    h = h_ref[...]
    hb = h.astype(BF16)

    @pl.when(pl.program_id(0) % tiles_per_seq == 0)
    def _():
        prev_ref[...] = jnp.zeros_like(prev_ref)

    def conv_proj(cols):
        u = _nn(hb, wu_ref[:, cols])
        ext = jnp.concatenate([prev_ref[:, cols], u], axis=0)
        prev_ref[:, cols] = u[tm - SUBLANES:, :]
        w = cw_ref[:, cols]
        back1 = pltpu.roll(ext, 1, 0)[SUBLANES:]
        back2 = pltpu.roll(ext, 2, 0)[SUBLANES:]
        return w[0:1] * back2 + w[1:2] * back1 + w[2:3] * u + cb_ref[:, cols]

    for c in range(dff // FFN_COLS):
        gate = conv_proj(slice(c * FFN_COLS, (c + 1) * FFN_COLS))
        up = conv_proj(slice(dff + c * FFN_COLS, dff + (c + 1) * FFN_COLS))
        act_ref[:, c * FFN_COLS:(c + 1) * FFN_COLS] = (gate * jax.nn.sigmoid(gate) * up).astype(BF16)
    o_ref[...] = _layer_norm(alpha * h + _nn(act_ref[...], wd_ref[...]), lnw_ref[...], lnb_ref[...])


def _ffn(h1, wu, cw, cb, wd, lnw, lnb, tm, seq, alpha):
    m, d = h1.shape
    dff = wd.shape[0]
    assert dff % FFN_COLS == 0 and seq % tm == 0
    const = lambda a: pl.BlockSpec(a.shape, lambda i: (0,) * a.ndim, pipeline_mode=pl.Buffered(1))
    return pl.pallas_call(
        functools.partial(_ffn_kernel, alpha=alpha, tiles_per_seq=seq // tm),
        grid=(m // tm,),
        in_specs=[pl.BlockSpec((tm, d), lambda i: (i, 0))]
        + [const(a) for a in (wu, cw, cb, wd, lnw, lnb)],
        out_specs=pl.BlockSpec((tm, d), lambda i: (i, 0)),
        out_shape=jax.ShapeDtypeStruct((m, d), F32),
        scratch_shapes=[pltpu.VMEM((SUBLANES, 2 * dff), F32), pltpu.VMEM((tm, dff), BF16)],
        compiler_params=pltpu.CompilerParams(
            dimension_semantics=("arbitrary",), vmem_limit_bytes=FFN_VMEM_LIMIT),
        name="conv_ffn_ln",
    )(h1, wu, cw, cb, wd, lnw, lnb)


def _tile_sizes(batch, seq):
    return dict(tm_proj=min(512, seq), tq=min(256, seq), chunk=64, tl_rwkv=min(256, seq),
                nb_rwkv=2 if batch % 2 == 0 else 1, tm_merge=min(512, seq), tm_ffn=min(512, seq))


def _layer(h, w_in, idx_w, idx_b, mu, w0, w_up, a0, a_up, g_up, k_k, k_a, r_k, gn_w, gn_b,
           w_ba, w_br, w_out, ln1_w, ln1_b, w_up_ffn, conv_w, conv_b, w_down, ln2_w, ln2_b, alpha):
    b, seq, d = h.shape
    ts = _tile_sizes(b, seq)
    aw = HEADS * HEAD_DIM
    kvw = KV_HEADS * HEAD_DIM
    o = np.cumsum([0, aw, kvw, kvw, aw, HEAD_DIM, HEADS, 3 * aw + LORA_W, d, d])
    col = lambda i: w_in[:, o[i]:o[i + 1]]
    head = lambda w, hh: w[:, hh * HEAD_DIM:(hh + 1) * HEAD_DIM]
    zero = jnp.zeros((d, HEAD_DIM), w_in.dtype)
    k_cols = [blk for hh in range(KV_HEADS)
              for blk in (head(col(1), hh), zero, zero, head(col(1), hh))]
    wi_pad = jnp.pad(col(5), ((0, 0), (0, LANES - HEADS)))
    wa = jnp.concatenate([col(0)] + k_cols + [col(2), col(3), col(4), col(4), wi_pad],
                         axis=1).astype(BF16)
    wr = col(6).astype(BF16)
    row2 = lambda v: v.reshape(1, -1)
    lnw2 = row2(jnp.concatenate([idx_w, idx_w]))
    lnb2 = row2(jnp.concatenate([idx_b, idx_b]))

    x2 = h.reshape(b * seq, d)
    q, k4, vt, qi, ki2, wi, pr = _inproj(x2, wa, wr, lnw2, lnb2, ts["tm_proj"], seq)
    r3 = lambda a: a.reshape(b, seq, a.shape[-1])

    y_attn = _attention(r3(q), r3(qi), r3(wi), r3(k4), vt, r3(ki2), ts["tq"])

    ld, la = w_up.shape[0], a_up.shape[0]
    pad_rows = lambda w, start: jnp.pad(w, ((start, LORA_W - start - w.shape[0]), (0, 0)))
    vecs = [row2(v) for v in (mu, w0, a0, k_k, k_a, r_k.reshape(-1), gn_w, gn_b)]
    mats = [pad_rows(w_up, 0), pad_rows(a_up, ld), pad_rows(g_up, ld + la)]
    y_rwkv = _rwkv(r3(pr), vecs, mats, ts["chunk"], ts["tl_rwkv"], ts["nb_rwkv"])

    h1 = _merge(x2, y_attn.reshape(b * seq, aw), y_rwkv.reshape(b * seq, aw),
                w_in[:, o[7]:o[9]].astype(BF16),
                w_ba.astype(BF16), w_br.astype(BF16), w_out.astype(BF16),
                row2(ln1_w), row2(ln1_b), ts["tm_merge"], alpha)

    out = _ffn(h1, w_up_ffn.astype(BF16), conv_w, row2(conv_b), w_down.astype(BF16),
               row2(ln2_w), row2(ln2_b), ts["tm_ffn"], seq, alpha)
    return out.reshape(b, seq, d)


def kernel(x, w_in, idx_k_norm_w, idx_k_norm_b, rwkv_mu, rwkv_w0, rwkv_w_up, rwkv_a0, rwkv_a_up,
           rwkv_g_up, rwkv_k_k, rwkv_k_a, rwkv_r_k, rwkv_ln_w, rwkv_ln_b, w_branch_attn,
           w_branch_rwkv, w_out, ln1_w, ln1_b, w_up, conv_w, conv_b, w_down, ln2_w, ln2_b):
    depth = w_in.shape[0]
    alpha = (2.0 * depth) ** 0.25
    h = x
    for l in range(depth):
        h = _layer(h, w_in[l], idx_k_norm_w[l], idx_k_norm_b[l], rwkv_mu[l], rwkv_w0[l],
                   rwkv_w_up[l], rwkv_a0[l], rwkv_a_up[l], rwkv_g_up[l], rwkv_k_k[l], rwkv_k_a[l],
                   rwkv_r_k[l], rwkv_ln_w[l], rwkv_ln_b[l], w_branch_attn[l], w_branch_rwkv[l],
                   w_out[l], ln1_w[l], ln1_b[l], w_up[l], conv_w[l], conv_b[l], w_down[l],
                   ln2_w[l], ln2_b[l], alpha)
    return h
```

```python
import functools

import jax
import jax.numpy as jnp
import ml_dtypes
import numpy as np
from jax import lax
from jax.experimental import pallas as pl
from jax.experimental.pallas import tpu as pltpu

F32 = jnp.float32
BF16 = jnp.bfloat16
I32 = jnp.int32

LANES = 128
HEAD_DIM = 64
HEADS = 8
KV_HEADS = 2
TOPK_MAX = 256
LORA_W = 128
LN_EPS = 1e-5
GN_EPS = 64e-5
ATTN_CLASSES = 4
SCORE_COLS = 512
BISECT_PASSES = 16
BISECT_ROWS = 256
BISECT_UNROLL = 4
VMEM_LIMIT = 48 * 1024 * 1024


def _nt(a, b):
    return lax.dot_general(a, b, (((1,), (1,)), ((), ())), preferred_element_type=F32)


def _nn(a, b):
    return lax.dot_general(a, b, (((1,), (0,)), ((), ())), preferred_element_type=F32)


def _bnt(a, b):
    return lax.dot_general(a, b, (((2,), (2,)), ((0,), (0,))), preferred_element_type=F32)


def _bnn(a, b):
    return lax.dot_general(a, b, (((2,), (1,)), ((0,), (0,))), preferred_element_type=F32)


LOG2E = 1.4426950408889634
POS_SPLIT = 16
SLOPE_TERMS = 3
ATTN_W = HEADS * HEAD_DIM
K4_W = 2 * KV_HEADS * LANES
_QKV_COLS = np.cumsum([0, ATTN_W, K4_W, KV_HEADS * HEAD_DIM, ATTN_W, LANES, LANES])


def _inproj_kernel(x_ref, wa_ref, wr_ref, lnw_ref, lnb_ref,
                   q_ref, k4_ref, vt_ref, qi_ref, ki2_ref, wi_ref, pr_ref, *, seq):
    tm = x_ref.shape[0]
    xb = x_ref[...].astype(BF16)
    pa = _nn(xb, wa_ref[...])
    part = lambda i: pa[:, _QKV_COLS[i]:_QKV_COLS[i + 1]]
    q_ref[...] = (part(0) * (HEAD_DIM ** -0.5 * LOG2E)).astype(BF16)
    pos = (pl.program_id(0) * tm) % seq + lax.broadcasted_iota(I32, (tm, 2 * LANES), 0)
    s_lo = pos & (POS_SPLIT - 1)
    ln = lax.broadcasted_iota(I32, (tm, 2 * LANES), 1)
    a = jnp.where(ln < LANES, ln - HEAD_DIM, ln - LANES)
    in_aug = (a >= 0) & (a < 2 * SLOPE_TERMS)
    aug = jnp.where(in_aug, jnp.where((a & 1) == 0, pos - s_lo, s_lo), 0).astype(F32)
    keys = part(1)
    for g in range(KV_HEADS):
        sl = slice(g * 2 * LANES, (g + 1) * 2 * LANES)
        k4_ref[:, sl] = (keys[:, sl] + aug).astype(BF16)
    vt = part(2).T
    vt_swapped = pltpu.roll(vt, HEAD_DIM, 0)
    vrow = lax.broadcasted_iota(I32, vt.shape, 0)
    top = vrow < HEAD_DIM
    one_at = lambda r: jnp.where(vrow == r, 1.0, 0.0)
    for g, (v_top, v_bottom) in enumerate(((vt, vt_swapped), (vt_swapped, vt))):
        vt_ref[(2 * g) * LANES:(2 * g + 1) * LANES, :] = jnp.where(top, v_top, one_at(HEAD_DIM)).astype(BF16)
        vt_ref[(2 * g + 1) * LANES:(2 * g + 2) * LANES, :] = jnp.where(top, one_at(0), v_bottom).astype(BF16)
    qi_ref[...] = part(3).astype(BF16)
    ki = part(4)
    mu = jnp.mean(ki, axis=-1, keepdims=True)
    var = jnp.mean(jnp.square(ki - mu), axis=-1, keepdims=True)
    ki2_ref[...] = ((ki - mu) * lax.rsqrt(var + LN_EPS) * lnw_ref[...] + lnb_ref[...]).astype(BF16)
    wi_ref[...] = part(5)
    pr_ref[...] = _nn(xb, wr_ref[...])


def _inproj(x2, wa, wr, lnw2, lnb2, tm, seq):
    m, d = x2.shape
    assert seq <= 2 ** 8 * POS_SPLIT and KV_HEADS == 2 and wa.shape[1] == _QKV_COLS[-1]
    full = lambda a: pl.BlockSpec(a.shape, lambda i: (0,) * a.ndim)
    row = lambda n: pl.BlockSpec((tm, n), lambda i: (i, 0))
    outs = [(ATTN_W, BF16), (K4_W, BF16), None, (ATTN_W, BF16), (LANES, BF16), (LANES, F32),
            (wr.shape[1], F32)]
    vt_rows = 2 * KV_HEADS * LANES
    return pl.pallas_call(
        functools.partial(_inproj_kernel, seq=seq),
        grid=(m // tm,),
        in_specs=[row(d), full(wa), full(wr), full(lnw2), full(lnb2)],
        out_specs=[row(o[0]) if o else pl.BlockSpec((vt_rows, tm), lambda i: (0, i)) for o in outs],
        out_shape=[jax.ShapeDtypeStruct((m, o[0]), o[1]) if o
                   else jax.ShapeDtypeStruct((vt_rows, m), BF16) for o in outs],
        compiler_params=pltpu.CompilerParams(
            dimension_semantics=("parallel",), vmem_limit_bytes=VMEM_LIMIT),
        name="inproj",
    )(x2, wa, wr, lnw2, lnb2)


def _bf16_parts(x):
    parts = []
    for _ in range(SLOPE_TERMS):
        parts.append(float(ml_dtypes.bfloat16(x)))
        x -= parts[-1]
    return parts


def _attn_body(q_ref, qi_ref, wi_ref, k4_ref, vt_ref, ki2_ref, o_ref,
               key_ref, bias_ref, *, t0, t0_max, tq, width, n_sel, cw):
    lane = lax.broadcasted_iota(I32, (tq, LANES), 1)
    lo = lane < HEAD_DIM
    n_cc = width // cw
    n_lc = width // LANES

    def head_lhs(ref, h, fill=None):
        pair = ref[0, :, (h // 2) * LANES:(h // 2 + 1) * LANES]
        fill = jnp.zeros_like(pair) if fill is None else fill
        return jnp.where(lo if h % 2 == 0 else jnp.logical_not(lo), pair, fill)

    wi = wi_ref[0] * (HEAD_DIM ** -0.5 * HEADS ** -0.5)
    row_c = t0 + lax.broadcasted_iota(I32, (tq, cw), 0)
    col_c = lax.broadcasted_iota(I32, (tq, cw), 1)
    for c in range(n_cc):
        kc = ki2_ref[0, c * cw:(c + 1) * cw, :]
        s = jnp.zeros((tq, cw), F32)
        for h in range(HEADS):
            s = s + jnp.maximum(_nt(head_lhs(qi_ref, h), kc), 0.0) * wi[:, h:h + 1]
        key_ref[:, c * cw:(c + 1) * cw] = jnp.where(col_c + c * cw <= row_c, s, -jnp.inf)

    kf = jnp.float32(n_sel)
    rb = min(BISECT_ROWS, tq)
    n_rb = tq // rb
    blocks = [slice(b * rb, (b + 1) * rb) for b in range(n_rb)]
    sls = [slice(c * LANES, (c + 1) * LANES) for c in range(n_lc)]
    n_live = {blk.start: min(n_lc, -(-(t0_max + blk.stop) // LANES)) for blk in blocks}

    def lanes_all(x):
        return jnp.broadcast_to(x, (rb, LANES))

    def count(pred_fn, rows, read=lambda ref, rows, sl: ref[rows, sl], ref=key_ref):
        acc = jnp.zeros((rb, LANES), F32)
        for sl in sls[:n_live[rows.start]]:
            acc = acc + jnp.where(pred_fn(read(ref, rows, sl)), 1.0, 0.0)
        return lanes_all(jnp.sum(acc, axis=1, keepdims=True))

    def smallest(pred_fn, rows):
        acc = jnp.full((rb, LANES), jnp.inf, F32)
        for sl in sls[:n_live[rows.start]]:
            sc = key_ref[rows, sl]
            acc = jnp.minimum(acc, jnp.where(pred_fn(sc), sc, jnp.inf))
        return lanes_all(jnp.min(acc, axis=1, keepdims=True))

    def largest(rows):
        acc = jnp.full((rb, LANES), -jnp.inf, F32)
        for sl in sls[:n_live[rows.start]]:
            acc = jnp.maximum(acc, key_ref[rows, sl])
        return lanes_all(jnp.max(acc, axis=1, keepdims=True))

    def bis_body(_, bounds):
        out = []
        for rows, (low, high) in zip(blocks, bounds):
            mid = 0.5 * low + 0.5 * high
            enough = count(lambda sc: sc >= mid, rows) >= kf
            out.append((jnp.where(enough, mid, low), jnp.where(enough, high, mid)))
        return tuple(out)

    bounds = tuple((smallest(lambda sc: sc > -jnp.inf, rows), largest(rows)) for rows in blocks)
    bounds = lax.fori_loop(0, BISECT_PASSES, bis_body, bounds, unroll=BISECT_UNROLL)
    lows = [low for low, _ in bounds]


    def settle(m, thr, done, rows):
        newly = jnp.where(count(lambda sc: sc > m, rows) < kf, 1.0 - done, 0.0)
        return jnp.where(newly > 0.0, m, thr), jnp.maximum(done, newly)

    walk = []
    for rows, low in zip(blocks, lows):
        m = smallest(lambda sc: sc >= low, rows)
        walk.append((m,) + settle(m, low, jnp.zeros((rb, LANES), F32), rows))

    def walk_pending(state):
        return functools.reduce(jnp.minimum, [jnp.min(done) for _, _, done in state]) < 1.0

    def walk_step(state):
        out = []
        for rows, (m, thr, done) in zip(blocks, state):
            m = smallest(lambda sc: sc > m, rows)
            out.append((m,) + settle(m, thr, done, rows))
        return tuple(out)

    thrs = [thr for _, thr, _ in lax.while_loop(walk_pending, walk_step, tuple(walk))]

    excess = [count(lambda sc: sc >= t, rows) - kf for rows, t in zip(blocks, thrs)]
    has_ties = functools.reduce(jnp.maximum, [jnp.max(e) for e in excess]) > 0.0

    @pl.when(jnp.logical_not(has_ties))
    def _():
        for rows, t in zip(blocks, thrs):
            for sl in sls:
                bias_ref[rows, sl] = jnp.where(key_ref[rows, sl] >= t, 0.0, -jnp.inf)

    @pl.when(has_ties)
    def _():
        lane_rb = lax.broadcasted_iota(I32, (rb, LANES), 1).astype(F32)
        far = jnp.float32(2 ** 30)
        for rows, t, surplus in zip(blocks, thrs, excess):
            for c, sl in enumerate(sls):
                bias_ref[rows, sl] = jnp.where(key_ref[rows, sl] == t, lane_rb + float(c * LANES), far)

            def drop_pending(state):
                return jnp.max(state[1]) > 0.0

            def drop_step(state, rows=rows):
                cut, left = state
                acc = jnp.full((rb, LANES), -1.0, F32)
                for sl in sls[:n_live[rows.start]]:
                    tpos = bias_ref[rows, sl]
                    acc = jnp.maximum(acc, jnp.where(tpos < cut, tpos, -1.0))
                highest = lanes_all(jnp.max(acc, axis=1, keepdims=True))
                return jnp.where(left > 0.0, highest, cut), left - 1.0

            cut, _ = lax.while_loop(drop_pending, drop_step, (jnp.full((rb, LANES), far, F32), surplus))
            for sl in sls:
                keep = jnp.where(bias_ref[rows, sl] < cut, 0.0, -jnp.inf)
                bias_ref[rows, sl] = jnp.where(key_ref[rows, sl] > t, 0.0, keep)

    bias = bias_ref[:, 0:width]
    a_idx = jnp.where(lo, lane, lane - HEAD_DIM)

    def qk(h):
        parts = _bf16_parts((2.0 ** -(h + 1)) * LOG2E)
        fill = jnp.zeros((tq, LANES), F32)
        for n in reversed(range(SLOPE_TERMS)):
            fill = jnp.where(a_idx < 2 * (n + 1), parts[n], fill)
        kcol = (2 * (h // (HEADS // KV_HEADS)) + h % 2) * LANES
        return _nt(head_lhs(q_ref, h, fill.astype(BF16)), k4_ref[0, 0:width, kcol:kcol + LANES])

    even = None
    qk_next = qk(0)
    for h in range(HEADS):
        g = h // (HEADS // KV_HEADS)
        qk_cur, qk_next = qk_next, (qk(h + 1) if h + 1 < HEADS else None)
        logit = qk_cur + bias
        m = jnp.max(logit, axis=1, keepdims=True)
        e = jnp.exp2(logit - m)
        vrows = (2 * g + h % 2) * LANES
        out = _nt(vt_ref[vrows:vrows + LANES, 0:width], e.astype(BF16)).T
        if h % 2 == 0:
            even = out
        else:
            p = h // 2
            denom = jnp.where(lo, jnp.broadcast_to(even[:, HEAD_DIM:HEAD_DIM + 1], (tq, LANES)),
                              jnp.broadcast_to(out[:, 0:1], (tq, LANES)))
            o_ref[0, :, p * LANES:(p + 1) * LANES] = (jnp.where(lo, even, out) / denom).astype(o_ref.dtype)


def _attn_kernel(q_ref, qi_ref, wi_ref, k4_ref, vt_ref, ki2_ref, o_ref,
                 key_ref, bias_ref, *, tq, seq, n_sel, n_classes):
    i = pl.program_id(0)
    tiles_per_class = (seq // tq) // n_classes
    ends = [(c + 1) * tiles_per_class for c in range(n_classes)]
    for first, last in zip([0] + ends[:-1], ends):
        width = last * tq

        @pl.when((i >= first) & (i < last))
        def _():
            _attn_body(q_ref, qi_ref, wi_ref, k4_ref, vt_ref, ki2_ref, o_ref, key_ref, bias_ref,
                       t0=i * tq, t0_max=width - tq, tq=tq, width=width, n_sel=n_sel,
                       cw=SCORE_COLS if width % SCORE_COLS == 0 else SCORE_COLS // 2)


def _attention(q, qi, wi, k4, vt, ki2, tq):
    b, seq, _ = q.shape
    n_sel = min(TOPK_MAX, seq // 4)
    n_classes = min(ATTN_CLASSES, seq // tq)
    qblk = lambda n: pl.BlockSpec((1, tq, n), lambda i, bi: (bi, i, 0))
    kblk = lambda n: pl.BlockSpec((1, seq, n), lambda i, bi: (bi, 0, 0))
    vblk = pl.BlockSpec((vt.shape[0], seq), lambda i, bi: (0, bi))
    return pl.pallas_call(
        functools.partial(_attn_kernel, tq=tq, seq=seq, n_sel=n_sel, n_classes=n_classes),
        grid=(seq // tq, b),
        in_specs=[qblk(ATTN_W), qblk(ATTN_W), qblk(LANES), kblk(K4_W), vblk, kblk(LANES)],
        out_specs=qblk(ATTN_W),
        out_shape=jax.ShapeDtypeStruct((b, seq, ATTN_W), BF16),
        scratch_shapes=[pltpu.VMEM((tq, seq), F32), pltpu.VMEM((tq, seq), F32)],
        compiler_params=pltpu.CompilerParams(
            dimension_semantics=("parallel", "parallel"), vmem_limit_bytes=VMEM_LIMIT),
        name="dsa_attention",
    )(q, qi, wi, k4, vt, ki2)


def _bf(x):
    return x.astype(BF16)


def _rwkv_kernel(pr_ref, mu_ref, w0_ref, wup_ref, a0_ref, aup_ref, gup_ref, kk_ref, ka_ref,
                 rk_ref, lnw_ref, lnb_ref, o_ref,
                 state_ref, prev_ref, tinv_ref, w_ref, atrt_ref, bhm_ref, y_ref, *, c):
    nb, tl, width = o_ref.shape
    n_pairs = width // LANES
    n_chunks = tl // c
    rows2 = 2 * c
    n_bp = nb * n_pairs

    @pl.when(pl.program_id(1) == 0)
    def _():
        state_ref[...] = jnp.zeros_like(state_ref)
        prev_ref[...] = jnp.zeros_like(prev_ref)

    ti = lax.broadcasted_iota(I32, (tl, tl), 0)
    tj = lax.broadcasted_iota(I32, (tl, tl), 1)
    same_chunk = (ti // c) == (tj // c)
    tri = jnp.where(same_chunk & (ti >= tj), 1.0, 0.0).astype(BF16)
    blk = jnp.where(same_chunk, 1.0, 0.0).astype(BF16)
    li = lax.broadcasted_iota(I32, (LANES, LANES), 0)
    lj = lax.broadcasted_iota(I32, (LANES, LANES), 1)
    ones_bd = jnp.where((li // HEAD_DIM) == (lj // HEAD_DIM), 1.0, 0.0).astype(BF16)
    upper_strict = li < lj
    upper_incl = li <= lj
    eye = jnp.where(li == lj, 1.0, 0.0)
    lo = lax.broadcasted_iota(I32, (c, LANES), 1) < HEAD_DIM
    rows = lax.broadcasted_iota(I32, (tl, pr_ref.shape[-1]), 0)

    def head_sum(x):
        return _nn(_bf(x), ones_bd)

    def stack(x):
        return _bf(jnp.concatenate([jnp.where(lo, x, 0.0), jnp.where(lo, 0.0, x)], axis=0))

    n_pc = n_chunks * n_bp
    atrt_l, btkt_l, vst_l, kh_l, bh_l = ([None] * n_pc for _ in range(5))
    kept = []
    for bi in range(nb):
        p = pr_ref[bi]
        shifted = jnp.where(rows == 0, prev_ref[bi, 7:8, :], pltpu.roll(p, 1, 0))
        prev_ref[bi] = p[tl - 8:tl, :]
        ps = p + (shifted - p) * mu_ref[...]
        r = ps[:, 0:width]
        k = ps[:, width:2 * width]
        v = ps[:, 2 * width:3 * width]
        lora = ps[:, 3 * width:3 * width + LORA_W]
        nz = -(w0_ref[...] + _nn(_bf(jnp.tanh(lora)), wup_ref[...]))
        log_w = -(jnp.maximum(nz, 0.0) + jnp.log(1.0 + jnp.exp(-jnp.abs(nz)))) - 0.5
        lw = -jnp.exp(log_w)
        a_sig = jax.nn.sigmoid(a0_ref[...] + _nn(_bf(lora), aup_ref[...]))
        gate = _nn(_bf(jax.nn.sigmoid(lora)), gup_ref[...])
        lw_hi = _bf(lw)
        lw_lo = _bf(lw - lw_hi.astype(F32))
        cum = _nn(tri, lw_hi) + _nn(tri, lw_lo)
        tot = _nn(blk, lw_hi) + _nn(blk, lw_lo)
        kept.append((r, k, v, a_sig, gate, tot))
        for pi in range(n_pairs):
            sl = slice(pi * LANES, (pi + 1) * LANES)
            r_p, k_p, v_p, a_p = r[:, sl], k[:, sl], v[:, sl], a_sig[:, sl]
            kk = k_p * kk_ref[:, sl]
            kk = kk * lax.rsqrt(jnp.maximum(head_sum(kk * kk), 1e-24))
            kmod = k_p * (1.0 + (a_p - 1.0) * ka_ref[:, sl])
            bvec = kk * a_p
            cum_p, tot_p = cum[:, sl], tot[:, sl]
            p_inv = jnp.exp(-cum_p)
            p_end = jnp.exp(tot_p - cum_p)
            at_all = -kk * jnp.exp(cum_p - lw[:, sl])
            rt_all = r_p * jnp.exp(cum_p)
            bt_all, kt_all = bvec * p_inv, kmod * p_inv
            bh_all, kh_all = bvec * p_end, kmod * p_end
            for ci in range(n_chunks):
                rs = slice(ci * c, (ci + 1) * c)
                idx = (ci * nb + bi) * n_pairs + pi
                atrt_l[idx] = jnp.concatenate([stack(at_all[rs]), stack(rt_all[rs])], axis=0)
                btkt_l[idx] = jnp.concatenate([stack(bt_all[rs]), stack(kt_all[rs])], axis=0)
                vst_l[idx] = _bf(jnp.concatenate([jnp.where(lo, v_p[rs], 0.0),
                                                  jnp.where(lo, 0.0, v_p[rs])], axis=0).T)
                kh_l[idx] = stack(kh_all[rs])
                bh_l[idx] = stack(bh_all[rs])

    atrt = jnp.stack(atrt_l)
    gt = _bnt(jnp.stack(btkt_l), atrt)
    n_t = jnp.where(upper_strict, gt[:, 0:rows2, 0:rows2], 0.0)
    m_rb_t = jnp.where(upper_incl, gt[:, 0:rows2, rows2:], 0.0)
    m_ak_t = jnp.where(upper_strict, gt[:, rows2:, 0:rows2], 0.0)
    m_rk_t = jnp.where(upper_incl, gt[:, rows2:, rows2:], 0.0)
    tinv = eye + n_t
    npow = _bf(n_t)
    for _ in range(int(np.log2(c)) - 1):
        npow = _bf(_bnn(npow, npow))
        tinv = tinv + _bnn(_bf(tinv), npow)
    rhs = jnp.concatenate([_bf(m_ak_t), jnp.stack(kh_l), _bf(m_rk_t)], axis=2)
    w_ref[...] = _bnn(jnp.stack(vst_l), rhs)
    tinv_ref[...] = _bf(tinv)
    atrt_ref[...] = atrt
    bhm_ref[...] = jnp.concatenate([jnp.stack(bh_l), _bf(m_rb_t)], axis=2)

    for ci in range(n_chunks):
        rs = slice(ci * c, (ci + 1) * c)
        ids = slice(ci * n_bp, (ci + 1) * n_bp)
        state = state_ref[...]
        x = _bnt(_bf(state), atrt_ref[ids])
        w = w_ref[ids]
        us_t = _bnn(_bf(x[:, :, 0:rows2] + w[:, :, 0:rows2]), tinv_ref[ids])
        z = _bnn(_bf(us_t), bhm_ref[ids])
        decay_c = jnp.exp(jnp.stack([kept[bi][5][ci * c:ci * c + 1, pi * LANES:(pi + 1) * LANES]
                                     for bi in range(nb) for pi in range(n_pairs)]))
        state_ref[...] = state * decay_c + z[:, :, 0:rows2] + w[:, :, rows2:2 * rows2]
        ys_t = x[:, :, rows2:] + z[:, :, rows2:] + w[:, :, 2 * rows2:]
        for bi in range(nb):
            for pi in range(n_pairs):
                ys = ys_t[bi * n_pairs + pi].T
                y_ref[bi, rs, pi * LANES:(pi + 1) * LANES] = ys[0:c, :] + ys[c:rows2, :]

    for bi in range(nb):
        r, k, v, a_sig, gate, _ = kept[bi]
        for pi in range(n_pairs):
            sl = slice(pi * LANES, (pi + 1) * LANES)
            y = y_ref[bi, :, sl]
            mean = head_sum(y) * (1.0 / HEAD_DIM)
            yc = y - mean
            var = head_sum(yc * yc) * (1.0 / HEAD_DIM)
            yn = yc * lax.rsqrt(var + GN_EPS) * lnw_ref[:, sl] + lnb_ref[:, sl]
            kmod = k[:, sl] * (1.0 + (a_sig[:, sl] - 1.0) * ka_ref[:, sl])
            bonus = head_sum(r[:, sl] * kmod * rk_ref[:, sl]) * v[:, sl]
            o_ref[bi, :, sl] = ((yn + bonus) * gate[:, sl]).astype(o_ref.dtype)


def _rwkv(pr, vecs, mats, c, tl, nb):
    b, seq, pw = pr.shape
    width = (pw - LORA_W) // 3
    n_pairs = width // LANES
    n_pc = (tl // c) * nb * n_pairs
    full = lambda a: pl.BlockSpec(a.shape, lambda bi, i: (0,) * a.ndim)
    mu, w0, a0, kk, ka, rk, lnw, lnb = vecs
    wup, aup, gup = [m.astype(BF16) for m in mats]
    return pl.pallas_call(
        functools.partial(_rwkv_kernel, c=c),
        grid=(b // nb, seq // tl),
        in_specs=[pl.BlockSpec((nb, tl, pw), lambda bi, i: (bi, i, 0)),
                  full(mu), full(w0), full(wup), full(a0), full(aup), full(gup),
                  full(kk), full(ka), full(rk), full(lnw), full(lnb)],
        out_specs=pl.BlockSpec((nb, tl, width), lambda bi, i: (bi, i, 0)),
        out_shape=jax.ShapeDtypeStruct((b, seq, width), BF16),
        scratch_shapes=[pltpu.VMEM((nb * n_pairs, LANES, LANES), F32),
                        pltpu.VMEM((nb, 8, pw), F32),
                        pltpu.VMEM((n_pc, 2 * c, 2 * c), BF16),
                        pltpu.VMEM((n_pc, 2 * c, 6 * c), F32),
                        pltpu.VMEM((n_pc, 4 * c, LANES), BF16),
                        pltpu.VMEM((n_pc, 2 * c, 4 * c), BF16),
                        pltpu.VMEM((nb, tl, width), F32)],
        compiler_params=pltpu.CompilerParams(
            dimension_semantics=("parallel", "arbitrary"), vmem_limit_bytes=VMEM_LIMIT),
        name="rwkv7",
    )(pr, mu, w0, wup, a0, aup, gup, kk, ka, rk, lnw, lnb)


def _layer_norm(z, w, b):
    mu = jnp.mean(z, axis=-1, keepdims=True)
    zc = z - mu
    var = jnp.mean(zc * zc, axis=-1, keepdims=True)
    return zc * lax.rsqrt(var + LN_EPS) * w + b


def _merge_kernel(x_ref, ya_ref, yr_ref, wg_ref, wa_ref, wb_ref, wo_ref, lnw_ref, lnb_ref,
                  o_ref, *, alpha):
    x = x_ref[...]
    d = x.shape[-1]
    xb = x.astype(BF16)
    mix = (jax.nn.sigmoid(_nn(xb, wg_ref[:, 0:d])) * _nn(ya_ref[...], wa_ref[...])
           + jax.nn.sigmoid(_nn(xb, wg_ref[:, d:2 * d])) * _nn(yr_ref[...], wb_ref[...]))
    z = alpha * x + _nn(mix.astype(BF16), wo_ref[...])
    o_ref[...] = _layer_norm(z, lnw_ref[...], lnb_ref[...])


def _merge(x2, ya, yr, wg, wa, wb, wo, lnw, lnb, tm, alpha):
    m, d = x2.shape
    full = lambda a: pl.BlockSpec(a.shape, lambda i: (0,) * a.ndim)
    row = lambda n: pl.BlockSpec((tm, n), lambda i: (i, 0))
    return pl.pallas_call(
        functools.partial(_merge_kernel, alpha=alpha),
        grid=(m // tm,),
        in_specs=[row(d), row(ya.shape[1]), row(yr.shape[1]),
                  full(wg), full(wa), full(wb), full(wo), full(lnw), full(lnb)],
        out_specs=row(d),
        out_shape=jax.ShapeDtypeStruct((m, d), F32),
        compiler_params=pltpu.CompilerParams(
            dimension_semantics=("parallel",), vmem_limit_bytes=VMEM_LIMIT),
        name="merge_out_ln",
    )(x2, ya, yr, wg, wa, wb, wo, lnw, lnb)


SUBLANES = 8
FFN_COLS = 256
FFN_VMEM_LIMIT = 56 * 1024 * 1024


def _ffn_kernel(h_ref, wu_ref, cw_ref, cb_ref, wd_ref, lnw_ref, lnb_ref, o_ref,
                prev_ref, act_ref, *, alpha, tiles_per_seq):
    tm = h_ref.shape[0]
    dff = wd_ref.shape[0]
    h = h_ref[...]
    hb = h.astype(BF16)

    @pl.when(pl.program_id(0) % tiles_per_seq == 0)
    def _():
        prev_ref[...] = jnp.zeros_like(prev_ref)

    def conv_proj(cols):
        u = _nn(hb, wu_ref[:, cols])
        ext = jnp.concatenate([prev_ref[:, cols], u], axis=0)
        prev_ref[:, cols] = u[tm - SUBLANES:, :]
        w = cw_ref[:, cols]
        back1 = pltpu.roll(ext, 1, 0)[SUBLANES:]
        back2 = pltpu.roll(ext, 2, 0)[SUBLANES:]
        return w[0:1] * back2 + w[1:2] * back1 + w[2:3] * u + cb_ref[:, cols]

    for c in range(dff // FFN_COLS):
        gate = conv_proj(slice(c * FFN_COLS, (c + 1) * FFN_COLS))
        up = conv_proj(slice(dff + c * FFN_COLS, dff + (c + 1) * FFN_COLS))
        act_ref[:, c * FFN_COLS:(c + 1) * FFN_COLS] = (gate * jax.nn.sigmoid(gate) * up).astype(BF16)
    o_ref[...] = _layer_norm(alpha * h + _nn(act_ref[...], wd_ref[...]), lnw_ref[...], lnb_ref[...])


def _ffn(h1, wu, cw, cb, wd, lnw, lnb, tm, seq, alpha):
    m, d = h1.shape
    dff = wd.shape[0]
    assert dff % FFN_COLS == 0 and seq % tm == 0
    const = lambda a: pl.BlockSpec(a.shape, lambda i: (0,) * a.ndim, pipeline_mode=pl.Buffered(1))
    return pl.pallas_call(
        functools.partial(_ffn_kernel, alpha=alpha, tiles_per_seq=seq // tm),
        grid=(m // tm,),
        in_specs=[pl.BlockSpec((tm, d), lambda i: (i, 0))]
        + [const(a) for a in (wu, cw, cb, wd, lnw, lnb)],
        out_specs=pl.BlockSpec((tm, d), lambda i: (i, 0)),
        out_shape=jax.ShapeDtypeStruct((m, d), F32),
        scratch_shapes=[pltpu.VMEM((SUBLANES, 2 * dff), F32), pltpu.VMEM((tm, dff), BF16)],
        compiler_params=pltpu.CompilerParams(
            dimension_semantics=("arbitrary",), vmem_limit_bytes=FFN_VMEM_LIMIT),
        name="conv_ffn_ln",
    )(h1, wu, cw, cb, wd, lnw, lnb)


def _tile_sizes(batch, seq):
    return dict(tm_proj=min(512, seq), tq=min(256, seq), chunk=64, tl_rwkv=min(256, seq),
                nb_rwkv=2 if batch % 2 == 0 else 1, tm_merge=min(512, seq), tm_ffn=min(512, seq))


def _layer(h, w_in, idx_w, idx_b, mu, w0, w_up, a0, a_up, g_up, k_k, k_a, r_k, gn_w, gn_b,
           w_ba, w_br, w_out, ln1_w, ln1_b, w_up_ffn, conv_w, conv_b, w_down, ln2_w, ln2_b, alpha):
    b, seq, d = h.shape
    ts = _tile_sizes(b, seq)
    aw = HEADS * HEAD_DIM
    kvw = KV_HEADS * HEAD_DIM
    o = np.cumsum([0, aw, kvw, kvw, aw, HEAD_DIM, HEADS, 3 * aw + LORA_W, d, d])
    col = lambda i: w_in[:, o[i]:o[i + 1]]
    head = lambda w, hh: w[:, hh * HEAD_DIM:(hh + 1) * HEAD_DIM]
    zero = jnp.zeros((d, HEAD_DIM), w_in.dtype)
    k_cols = [blk for hh in range(KV_HEADS)
              for blk in (head(col(1), hh), zero, zero, head(col(1), hh))]
    wi_pad = jnp.pad(col(5), ((0, 0), (0, LANES - HEADS)))
    wa = jnp.concatenate([col(0)] + k_cols + [col(2), col(3), col(4), col(4), wi_pad],
                         axis=1).astype(BF16)
    wr = col(6).astype(BF16)
    row2 = lambda v: v.reshape(1, -1)
    lnw2 = row2(jnp.concatenate([idx_w, idx_w]))
    lnb2 = row2(jnp.concatenate([idx_b, idx_b]))

    x2 = h.reshape(b * seq, d)
    q, k4, vt, qi, ki2, wi, pr = _inproj(x2, wa, wr, lnw2, lnb2, ts["tm_proj"], seq)
    r3 = lambda a: a.reshape(b, seq, a.shape[-1])

    y_attn = _attention(r3(q), r3(qi), r3(wi), r3(k4), vt, r3(ki2), ts["tq"])

    ld, la = w_up.shape[0], a_up.shape[0]
    pad_rows = lambda w, start: jnp.pad(w, ((start, LORA_W - start - w.shape[0]), (0, 0)))
    vecs = [row2(v) for v in (mu, w0, a0, k_k, k_a, r_k.reshape(-1), gn_w, gn_b)]
    mats = [pad_rows(w_up, 0), pad_rows(a_up, ld), pad_rows(g_up, ld + la)]
    y_rwkv = _rwkv(r3(pr), vecs, mats, ts["chunk"], ts["tl_rwkv"], ts["nb_rwkv"])

    h1 = _merge(x2, y_attn.reshape(b * seq, aw), y_rwkv.reshape(b * seq, aw),
                w_in[:, o[7]:o[9]].astype(BF16),
                w_ba.astype(BF16), w_br.astype(BF16), w_out.astype(BF16),
                row2(ln1_w), row2(ln1_b), ts["tm_merge"], alpha)

    out = _ffn(h1, w_up_ffn.astype(BF16), conv_w, row2(conv_b), w_down.astype(BF16),
               row2(ln2_w), row2(ln2_b), ts["tm_ffn"], seq, alpha)
    return out.reshape(b, seq, d)


def kernel(x, w_in, idx_k_norm_w, idx_k_norm_b, rwkv_mu, rwkv_w0, rwkv_w_up, rwkv_a0, rwkv_a_up,
           rwkv_g_up, rwkv_k_k, rwkv_k_a, rwkv_r_k, rwkv_ln_w, rwkv_ln_b, w_branch_attn,
           w_branch_rwkv, w_out, ln1_w, ln1_b, w_up, conv_w, conv_b, w_down, ln2_w, ln2_b):
    depth = w_in.shape[0]
    alpha = (2.0 * depth) ** 0.25
    h = x
    for l in range(depth):
        h = _layer(h, w_in[l], idx_k_norm_w[l], idx_k_norm_b[l], rwkv_mu[l], rwkv_w0[l],
                   rwkv_w_up[l], rwkv_a0[l], rwkv_a_up[l], rwkv_g_up[l], rwkv_k_k[l], rwkv_k_a[l],
                   rwkv_r_k[l], rwkv_ln_w[l], rwkv_ln_b[l], w_branch_attn[l], w_branch_rwkv[l],
                   w_out[l], ln1_w[l], ln1_b[l], w_up[l], conv_w[l], conv_b[l], w_down[l],
                   ln2_w[l], ln2_b[l], alpha)
    return h
```

```python
import functools

import jax
import jax.numpy as jnp
import ml_dtypes
import numpy as np
from jax import lax
from jax.experimental import pallas as pl
from jax.experimental.pallas import tpu as pltpu

F32 = jnp.float32
BF16 = jnp.bfloat16
I32 = jnp.int32

LANES = 128
HEAD_DIM = 64
HEADS = 8
KV_HEADS = 2
TOPK_MAX = 256
LORA_W = 128
LN_EPS = 1e-5
GN_EPS = 64e-5
ATTN_CLASSES = 4
SCORE_COLS = 512
BISECT_PASSES = 16
BISECT_ROWS = 256
BISECT_UNROLL = 8
VMEM_LIMIT = 48 * 1024 * 1024


def _nt(a, b):
    return lax.dot_general(a, b, (((1,), (1,)), ((), ())), preferred_element_type=F32)


def _nn(a, b):
    return lax.dot_general(a, b, (((1,), (0,)), ((), ())), preferred_element_type=F32)


def _bnt(a, b):
    return lax.dot_general(a, b, (((2,), (2,)), ((0,), (0,))), preferred_element_type=F32)


def _bnn(a, b):
    return lax.dot_general(a, b, (((2,), (1,)), ((0,), (0,))), preferred_element_type=F32)


LOG2E = 1.4426950408889634
POS_SPLIT = 16
SLOPE_TERMS = 3
ATTN_W = HEADS * HEAD_DIM
K4_W = 2 * KV_HEADS * LANES
_QKV_COLS = np.cumsum([0, ATTN_W, K4_W, KV_HEADS * HEAD_DIM, ATTN_W, LANES, LANES])


def _inproj_kernel(x_ref, wa_ref, wr_ref, lnw_ref, lnb_ref,
                   q_ref, k4_ref, vt_ref, qi_ref, ki2_ref, wi_ref, pr_ref, *, seq):
    tm = x_ref.shape[0]
    xb = x_ref[...].astype(BF16)
    pa = _nn(xb, wa_ref[...])
    part = lambda i: pa[:, _QKV_COLS[i]:_QKV_COLS[i + 1]]
    q_ref[...] = (part(0) * (HEAD_DIM ** -0.5 * LOG2E)).astype(BF16)
    pos = (pl.program_id(0) * tm) % seq + lax.broadcasted_iota(I32, (tm, 2 * LANES), 0)
    s_lo = pos & (POS_SPLIT - 1)
    ln = lax.broadcasted_iota(I32, (tm, 2 * LANES), 1)
    a = jnp.where(ln < LANES, ln - HEAD_DIM, ln - LANES)
    in_aug = (a >= 0) & (a < 2 * SLOPE_TERMS)
    aug = jnp.where(in_aug, jnp.where((a & 1) == 0, pos - s_lo, s_lo), 0).astype(F32)
    keys = part(1)
    for g in range(KV_HEADS):
        sl = slice(g * 2 * LANES, (g + 1) * 2 * LANES)
        k4_ref[:, sl] = (keys[:, sl] + aug).astype(BF16)
    vt = part(2).T
    vt_swapped = pltpu.roll(vt, HEAD_DIM, 0)
    vrow = lax.broadcasted_iota(I32, vt.shape, 0)
    top = vrow < HEAD_DIM
    one_at = lambda r: jnp.where(vrow == r, 1.0, 0.0)
    for g, (v_top, v_bottom) in enumerate(((vt, vt_swapped), (vt_swapped, vt))):
        vt_ref[(2 * g) * LANES:(2 * g + 1) * LANES, :] = jnp.where(top, v_top, one_at(HEAD_DIM)).astype(BF16)
        vt_ref[(2 * g + 1) * LANES:(2 * g + 2) * LANES, :] = jnp.where(top, one_at(0), v_bottom).astype(BF16)
    qi_ref[...] = part(3).astype(BF16)
    ki = part(4)
    mu = jnp.mean(ki, axis=-1, keepdims=True)
    var = jnp.mean(jnp.square(ki - mu), axis=-1, keepdims=True)
    ki2_ref[...] = ((ki - mu) * lax.rsqrt(var + LN_EPS) * lnw_ref[...] + lnb_ref[...]).astype(BF16)
    wi_ref[...] = part(5)
    pr_ref[...] = _nn(xb, wr_ref[...])


def _inproj(x2, wa, wr, lnw2, lnb2, tm, seq):
    m, d = x2.shape
    assert seq <= 2 ** 8 * POS_SPLIT and KV_HEADS == 2 and wa.shape[1] == _QKV_COLS[-1]
    full = lambda a: pl.BlockSpec(a.shape, lambda i: (0,) * a.ndim)
    row = lambda n: pl.BlockSpec((tm, n), lambda i: (i, 0))
    outs = [(ATTN_W, BF16), (K4_W, BF16), None, (ATTN_W, BF16), (LANES, BF16), (LANES, F32),
            (wr.shape[1], F32)]
    vt_rows = 2 * KV_HEADS * LANES
    return pl.pallas_call(
        functools.partial(_inproj_kernel, seq=seq),
        grid=(m // tm,),
        in_specs=[row(d), full(wa), full(wr), full(lnw2), full(lnb2)],
        out_specs=[row(o[0]) if o else pl.BlockSpec((vt_rows, tm), lambda i: (0, i)) for o in outs],
        out_shape=[jax.ShapeDtypeStruct((m, o[0]), o[1]) if o
                   else jax.ShapeDtypeStruct((vt_rows, m), BF16) for o in outs],
        compiler_params=pltpu.CompilerParams(
            dimension_semantics=("parallel",), vmem_limit_bytes=VMEM_LIMIT),
        name="inproj",
    )(x2, wa, wr, lnw2, lnb2)


def _bf16_parts(x):
    parts = []
    for _ in range(SLOPE_TERMS):
        parts.append(float(ml_dtypes.bfloat16(x)))
        x -= parts[-1]
    return parts


def _attn_body(q_ref, qi_ref, wi_ref, k4_ref, vt_ref, ki2_ref, o_ref,
               key_ref, bias_ref, *, t0, t0_max, tq, width, n_sel, cw):
    lane = lax.broadcasted_iota(I32, (tq, LANES), 1)
    lo = lane < HEAD_DIM
    n_cc = width // cw
    n_lc = width // LANES

    def head_lhs(ref, h, fill=None):
        pair = ref[0, :, (h // 2) * LANES:(h // 2 + 1) * LANES]
        fill = jnp.zeros_like(pair) if fill is None else fill
        return jnp.where(lo if h % 2 == 0 else jnp.logical_not(lo), pair, fill)

    wi = wi_ref[0] * (HEAD_DIM ** -0.5 * HEADS ** -0.5)
    row_c = t0 + lax.broadcasted_iota(I32, (tq, cw), 0)
    col_c = lax.broadcasted_iota(I32, (tq, cw), 1)
    for c in range(n_cc):
        kc = ki2_ref[0, c * cw:(c + 1) * cw, :]
        s = jnp.zeros((tq, cw), F32)
        for h in range(HEADS):
            s = s + jnp.maximum(_nt(head_lhs(qi_ref, h), kc), 0.0) * wi[:, h:h + 1]
        key_ref[:, c * cw:(c + 1) * cw] = jnp.where(col_c + c * cw <= row_c, s, -jnp.inf)

    kf = jnp.float32(n_sel)
    rb = min(BISECT_ROWS, tq)
    n_rb = tq // rb
    blocks = [slice(b * rb, (b + 1) * rb) for b in range(n_rb)]
    sls = [slice(c * LANES, (c + 1) * LANES) for c in range(n_lc)]
    n_live = {blk.start: min(n_lc, -(-(t0_max + blk.stop) // LANES)) for blk in blocks}

    def lanes_all(x):
        return jnp.broadcast_to(x, (rb, LANES))

    def count(pred_fn, rows, read=lambda ref, rows, sl: ref[rows, sl], ref=key_ref):
        acc = jnp.zeros((rb, LANES), F32)
        for sl in sls[:n_live[rows.start]]:
            acc = acc + jnp.where(pred_fn(read(ref, rows, sl)), 1.0, 0.0)
        return lanes_all(jnp.sum(acc, axis=1, keepdims=True))

    def smallest(pred_fn, rows):
        acc = jnp.full((rb, LANES), jnp.inf, F32)
        for sl in sls[:n_live[rows.start]]:
            sc = key_ref[rows, sl]
            acc = jnp.minimum(acc, jnp.where(pred_fn(sc), sc, jnp.inf))
        return lanes_all(jnp.min(acc, axis=1, keepdims=True))

    def largest(rows):
        acc = jnp.full((rb, LANES), -jnp.inf, F32)
        for sl in sls[:n_live[rows.start]]:
            acc = jnp.maximum(acc, key_ref[rows, sl])
        return lanes_all(jnp.max(acc, axis=1, keepdims=True))

    def bis_body(_, bounds):
        out = []
        for rows, (low, high) in zip(blocks, bounds):
            mid = 0.5 * low + 0.5 * high
            enough = count(lambda sc: sc >= mid, rows) >= kf
            out.append((jnp.where(enough, mid, low), jnp.where(enough, high, mid)))
        return tuple(out)

    bounds = tuple((smallest(lambda sc: sc > -jnp.inf, rows), largest(rows)) for rows in blocks)
    bounds = lax.fori_loop(0, BISECT_PASSES, bis_body, bounds, unroll=BISECT_UNROLL)
    lows = [low for low, _ in bounds]


    def settle(m, thr, done, rows):
        newly = jnp.where(count(lambda sc: sc > m, rows) < kf, 1.0 - done, 0.0)
        return jnp.where(newly > 0.0, m, thr), jnp.maximum(done, newly)

    walk = []
    for rows, low in zip(blocks, lows):
        m = smallest(lambda sc: sc >= low, rows)
        walk.append((m,) + settle(m, low, jnp.zeros((rb, LANES), F32), rows))

    def walk_pending(state):
        return functools.reduce(jnp.minimum, [jnp.min(done) for _, _, done in state]) < 1.0

    def walk_step(state):
        out = []
        for rows, (m, thr, done) in zip(blocks, state):
            m = smallest(lambda sc: sc > m, rows)
            out.append((m,) + settle(m, thr, done, rows))
        return tuple(out)

    thrs = [thr for _, thr, _ in lax.while_loop(walk_pending, walk_step, tuple(walk))]

    excess = []
    for rows, t in zip(blocks, thrs):
        acc = jnp.zeros((rb, LANES), F32)
        for sl in sls:
            hit = key_ref[rows, sl] >= t
            bias_ref[rows, sl] = jnp.where(hit, 0.0, -jnp.inf)
            acc = acc + jnp.where(hit, 1.0, 0.0)
        excess.append(lanes_all(jnp.sum(acc, axis=1, keepdims=True)) - kf)
    has_ties = functools.reduce(jnp.maximum, [jnp.max(e) for e in excess]) > 0.0

    @pl.when(has_ties)
    def _():
        lane_rb = lax.broadcasted_iota(I32, (rb, LANES), 1).astype(F32)
        far = jnp.float32(2 ** 30)
        for rows, t, surplus in zip(blocks, thrs, excess):
            for c, sl in enumerate(sls):
                bias_ref[rows, sl] = jnp.where(key_ref[rows, sl] == t, lane_rb + float(c * LANES), far)

            def drop_pending(state):
                return jnp.max(state[1]) > 0.0

            def drop_step(state, rows=rows):
                cut, left = state
                acc = jnp.full((rb, LANES), -1.0, F32)
                for sl in sls[:n_live[rows.start]]:
                    tpos = bias_ref[rows, sl]
                    acc = jnp.maximum(acc, jnp.where(tpos < cut, tpos, -1.0))
                highest = lanes_all(jnp.max(acc, axis=1, keepdims=True))
                return jnp.where(left > 0.0, highest, cut), left - 1.0

            cut, _ = lax.while_loop(drop_pending, drop_step, (jnp.full((rb, LANES), far, F32), surplus))
            for sl in sls:
                keep = jnp.where(bias_ref[rows, sl] < cut, 0.0, -jnp.inf)
                bias_ref[rows, sl] = jnp.where(key_ref[rows, sl] > t, 0.0, keep)

    bias = bias_ref[:, 0:width]
    a_idx = jnp.where(lo, lane, lane - HEAD_DIM)

    def qk(h):
        parts = _bf16_parts((2.0 ** -(h + 1)) * LOG2E)
        fill = jnp.zeros((tq, LANES), F32)
        for n in reversed(range(SLOPE_TERMS)):
            fill = jnp.where(a_idx < 2 * (n + 1), parts[n], fill)
        kcol = (2 * (h // (HEADS // KV_HEADS)) + h % 2) * LANES
        return _nt(head_lhs(q_ref, h, fill.astype(BF16)), k4_ref[0, 0:width, kcol:kcol + LANES])

    even = None
    qk_next = qk(0)
    for h in range(HEADS):
        g = h // (HEADS // KV_HEADS)
        qk_cur, qk_next = qk_next, (qk(h + 1) if h + 1 < HEADS else None)
        logit = qk_cur + bias
        m = jnp.max(logit, axis=1, keepdims=True)
        e = jnp.exp2(logit - m)
        vrows = (2 * g + h % 2) * LANES
        out = _nt(vt_ref[vrows:vrows + LANES, 0:width], e.astype(BF16)).T
        if h % 2 == 0:
            even = out
        else:
            p = h // 2
            denom = jnp.where(lo, jnp.broadcast_to(even[:, HEAD_DIM:HEAD_DIM + 1], (tq, LANES)),
                              jnp.broadcast_to(out[:, 0:1], (tq, LANES)))
            o_ref[0, :, p * LANES:(p + 1) * LANES] = (jnp.where(lo, even, out) / denom).astype(o_ref.dtype)


def _attn_kernel(q_ref, qi_ref, wi_ref, k4_ref, vt_ref, ki2_ref, o_ref,
                 key_ref, bias_ref, *, tq, seq, n_sel, n_classes):
    i = pl.program_id(0)
    tiles_per_class = (seq // tq) // n_classes
    ends = [(c + 1) * tiles_per_class for c in range(n_classes)]
    for first, last in zip([0] + ends[:-1], ends):
        width = last * tq

        @pl.when((i >= first) & (i < last))
        def _():
            _attn_body(q_ref, qi_ref, wi_ref, k4_ref, vt_ref, ki2_ref, o_ref, key_ref, bias_ref,
                       t0=i * tq, t0_max=width - tq, tq=tq, width=width, n_sel=n_sel,
                       cw=SCORE_COLS if width % SCORE_COLS == 0 else SCORE_COLS // 2)


def _attention(q, qi, wi, k4, vt, ki2, tq):
    b, seq, _ = q.shape
    n_sel = min(TOPK_MAX, seq // 4)
    n_classes = min(ATTN_CLASSES, seq // tq)
    qblk = lambda n: pl.BlockSpec((1, tq, n), lambda i, bi: (bi, i, 0))
    kblk = lambda n: pl.BlockSpec((1, seq, n), lambda i, bi: (bi, 0, 0))
    vblk = pl.BlockSpec((vt.shape[0], seq), lambda i, bi: (0, bi))
    return pl.pallas_call(
        functools.partial(_attn_kernel, tq=tq, seq=seq, n_sel=n_sel, n_classes=n_classes),
        grid=(seq // tq, b),
        in_specs=[qblk(ATTN_W), qblk(ATTN_W), qblk(LANES), kblk(K4_W), vblk, kblk(LANES)],
        out_specs=qblk(ATTN_W),
        out_shape=jax.ShapeDtypeStruct((b, seq, ATTN_W), BF16),
        scratch_shapes=[pltpu.VMEM((tq, seq), F32), pltpu.VMEM((tq, seq), F32)],
        compiler_params=pltpu.CompilerParams(
            dimension_semantics=("parallel", "parallel"), vmem_limit_bytes=VMEM_LIMIT),
        name="dsa_attention",
    )(q, qi, wi, k4, vt, ki2)


def _bf(x):
    return x.astype(BF16)


def _rwkv_kernel(pr_ref, mu_ref, w0_ref, wup_ref, a0_ref, aup_ref, gup_ref, kk_ref, ka_ref,
                 rk_ref, lnw_ref, lnb_ref, o_ref,
                 state_ref, prev_ref, tinv_ref, w_ref, atrt_ref, bhm_ref, y_ref, *, c):
    nb, tl, width = o_ref.shape
    n_pairs = width // LANES
    n_chunks = tl // c
    rows2 = 2 * c
    n_bp = nb * n_pairs

    @pl.when(pl.program_id(1) == 0)
    def _():
        state_ref[...] = jnp.zeros_like(state_ref)
        prev_ref[...] = jnp.zeros_like(prev_ref)

    ti = lax.broadcasted_iota(I32, (tl, tl), 0)
    tj = lax.broadcasted_iota(I32, (tl, tl), 1)
    same_chunk = (ti // c) == (tj // c)
    tri = jnp.where(same_chunk & (ti >= tj), 1.0, 0.0).astype(BF16)
    blk = jnp.where(same_chunk, 1.0, 0.0).astype(BF16)
    li = lax.broadcasted_iota(I32, (LANES, LANES), 0)
    lj = lax.broadcasted_iota(I32, (LANES, LANES), 1)
    ones_bd = jnp.where((li // HEAD_DIM) == (lj // HEAD_DIM), 1.0, 0.0).astype(BF16)
    upper_strict = li < lj
    upper_incl = li <= lj
    eye = jnp.where(li == lj, 1.0, 0.0)
    lo = lax.broadcasted_iota(I32, (c, LANES), 1) < HEAD_DIM
    rows = lax.broadcasted_iota(I32, (tl, pr_ref.shape[-1]), 0)

    def head_sum(x):
        return _nn(_bf(x), ones_bd)

    def stack(x):
        return _bf(jnp.concatenate([jnp.where(lo, x, 0.0), jnp.where(lo, 0.0, x)], axis=0))

    n_pc = n_chunks * n_bp
    atrt_l, btkt_l, vst_l, kh_l, bh_l = ([None] * n_pc for _ in range(5))
    kept = []
    for bi in range(nb):
        p = pr_ref[bi]
        shifted = jnp.where(rows == 0, prev_ref[bi, 7:8, :], pltpu.roll(p, 1, 0))
        prev_ref[bi] = p[tl - 8:tl, :]
        ps = p + (shifted - p) * mu_ref[...]
        r = ps[:, 0:width]
        k = ps[:, width:2 * width]
        v = ps[:, 2 * width:3 * width]
        lora = ps[:, 3 * width:3 * width + LORA_W]
        nz = -(w0_ref[...] + _nn(_bf(jnp.tanh(lora)), wup_ref[...]))
        log_w = -(jnp.maximum(nz, 0.0) + jnp.log(1.0 + jnp.exp(-jnp.abs(nz)))) - 0.5
        lw = -jnp.exp(log_w)
        a_sig = jax.nn.sigmoid(a0_ref[...] + _nn(_bf(lora), aup_ref[...]))
        gate = _nn(_bf(jax.nn.sigmoid(lora)), gup_ref[...])
        lw_hi = _bf(lw)
        lw_lo = _bf(lw - lw_hi.astype(F32))
        cum = _nn(tri, lw_hi) + _nn(tri, lw_lo)
        tot = _nn(blk, lw_hi) + _nn(blk, lw_lo)
        kept.append((r, k, v, a_sig, gate, tot))
        for pi in range(n_pairs):
            sl = slice(pi * LANES, (pi + 1) * LANES)
            r_p, k_p, v_p, a_p = r[:, sl], k[:, sl], v[:, sl], a_sig[:, sl]
            kk = k_p * kk_ref[:, sl]
            kk = kk * lax.rsqrt(jnp.maximum(head_sum(kk * kk), 1e-24))
            kmod = k_p * (1.0 + (a_p - 1.0) * ka_ref[:, sl])
            bvec = kk * a_p
            cum_p, tot_p = cum[:, sl], tot[:, sl]
            p_inv = jnp.exp(-cum_p)
            p_end = jnp.exp(tot_p - cum_p)
            at_all = -kk * jnp.exp(cum_p - lw[:, sl])
            rt_all = r_p * jnp.exp(cum_p)
            bt_all, kt_all = bvec * p_inv, kmod * p_inv
            bh_all, kh_all = bvec * p_end, kmod * p_end
            for ci in range(n_chunks):
                rs = slice(ci * c, (ci + 1) * c)
                idx = (ci * nb + bi) * n_pairs + pi
                atrt_l[idx] = jnp.concatenate([stack(at_all[rs]), stack(rt_all[rs])], axis=0)
                btkt_l[idx] = jnp.concatenate([stack(bt_all[rs]), stack(kt_all[rs])], axis=0)
                vst_l[idx] = _bf(jnp.concatenate([jnp.where(lo, v_p[rs], 0.0),
                                                  jnp.where(lo, 0.0, v_p[rs])], axis=0).T)
                kh_l[idx] = stack(kh_all[rs])
                bh_l[idx] = stack(bh_all[rs])

    atrt = jnp.stack(atrt_l)
    gt = _bnt(jnp.stack(btkt_l), atrt)
    n_t = jnp.where(upper_strict, gt[:, 0:rows2, 0:rows2], 0.0)
    m_rb_t = jnp.where(upper_incl, gt[:, 0:rows2, rows2:], 0.0)
    m_ak_t = jnp.where(upper_strict, gt[:, rows2:, 0:rows2], 0.0)
    m_rk_t = jnp.where(upper_incl, gt[:, rows2:, rows2:], 0.0)
    tinv = eye + n_t
    npow = _bf(n_t)
    for _ in range(int(np.log2(c)) - 1):
        npow = _bf(_bnn(npow, npow))
        tinv = tinv + _bnn(_bf(tinv), npow)
    rhs = jnp.concatenate([_bf(m_ak_t), jnp.stack(kh_l), _bf(m_rk_t)], axis=2)
    w_ref[...] = _bnn(jnp.stack(vst_l), rhs)
    tinv_ref[...] = _bf(tinv)
    atrt_ref[...] = atrt
    bhm_ref[...] = jnp.concatenate([jnp.stack(bh_l), _bf(m_rb_t)], axis=2)

    for ci in range(n_chunks):
        rs = slice(ci * c, (ci + 1) * c)
        ids = slice(ci * n_bp, (ci + 1) * n_bp)
        state = state_ref[...]
        x = _bnt(_bf(state), atrt_ref[ids])
        w = w_ref[ids]
        us_t = _bnn(_bf(x[:, :, 0:rows2] + w[:, :, 0:rows2]), tinv_ref[ids])
        z = _bnn(_bf(us_t), bhm_ref[ids])
        decay_c = jnp.exp(jnp.stack([kept[bi][5][ci * c:ci * c + 1, pi * LANES:(pi + 1) * LANES]
                                     for bi in range(nb) for pi in range(n_pairs)]))
        state_ref[...] = state * decay_c + z[:, :, 0:rows2] + w[:, :, rows2:2 * rows2]
        ys_t = x[:, :, rows2:] + z[:, :, rows2:] + w[:, :, 2 * rows2:]
        for bi in range(nb):
            for pi in range(n_pairs):
                ys = ys_t[bi * n_pairs + pi].T
                y_ref[bi, rs, pi * LANES:(pi + 1) * LANES] = ys[0:c, :] + ys[c:rows2, :]

    for bi in range(nb):
        r, k, v, a_sig, gate, _ = kept[bi]
        for pi in range(n_pairs):
            sl = slice(pi * LANES, (pi + 1) * LANES)
            y = y_ref[bi, :, sl]
            mean = head_sum(y) * (1.0 / HEAD_DIM)
            yc = y - mean
            var = head_sum(yc * yc) * (1.0 / HEAD_DIM)
            yn = yc * lax.rsqrt(var + GN_EPS) * lnw_ref[:, sl] + lnb_ref[:, sl]
            kmod = k[:, sl] * (1.0 + (a_sig[:, sl] - 1.0) * ka_ref[:, sl])
            bonus = head_sum(r[:, sl] * kmod * rk_ref[:, sl]) * v[:, sl]
            o_ref[bi, :, sl] = ((yn + bonus) * gate[:, sl]).astype(o_ref.dtype)


def _rwkv(pr, vecs, mats, c, tl, nb):
    b, seq, pw = pr.shape
    width = (pw - LORA_W) // 3
    n_pairs = width // LANES
    n_pc = (tl // c) * nb * n_pairs
    full = lambda a: pl.BlockSpec(a.shape, lambda bi, i: (0,) * a.ndim)
    mu, w0, a0, kk, ka, rk, lnw, lnb = vecs
    wup, aup, gup = [m.astype(BF16) for m in mats]
    return pl.pallas_call(
        functools.partial(_rwkv_kernel, c=c),
        grid=(b // nb, seq // tl),
        in_specs=[pl.BlockSpec((nb, tl, pw), lambda bi, i: (bi, i, 0)),
                  full(mu), full(w0), full(wup), full(a0), full(aup), full(gup),
                  full(kk), full(ka), full(rk), full(lnw), full(lnb)],
        out_specs=pl.BlockSpec((nb, tl, width), lambda bi, i: (bi, i, 0)),
        out_shape=jax.ShapeDtypeStruct((b, seq, width), BF16),
        scratch_shapes=[pltpu.VMEM((nb * n_pairs, LANES, LANES), F32),
                        pltpu.VMEM((nb, 8, pw), F32),
                        pltpu.VMEM((n_pc, 2 * c, 2 * c), BF16),
                        pltpu.VMEM((n_pc, 2 * c, 6 * c), F32),
                        pltpu.VMEM((n_pc, 4 * c, LANES), BF16),
                        pltpu.VMEM((n_pc, 2 * c, 4 * c), BF16),
                        pltpu.VMEM((nb, tl, width), F32)],
        compiler_params=pltpu.CompilerParams(
            dimension_semantics=("parallel", "arbitrary"), vmem_limit_bytes=VMEM_LIMIT),
        name="rwkv7",
    )(pr, mu, w0, wup, a0, aup, gup, kk, ka, rk, lnw, lnb)


def _layer_norm(z, w, b):
    mu = jnp.mean(z, axis=-1, keepdims=True)
    zc = z - mu
    var = jnp.mean(zc * zc, axis=-1, keepdims=True)
    return zc * lax.rsqrt(var + LN_EPS) * w + b


def _merge_kernel(x_ref, ya_ref, yr_ref, wg_ref, wa_ref, wb_ref, wo_ref, lnw_ref, lnb_ref,
                  o_ref, *, alpha):
    x = x_ref[...]
    d = x.shape[-1]
    xb = x.astype(BF16)
    mix = (jax.nn.sigmoid(_nn(xb, wg_ref[:, 0:d])) * _nn(ya_ref[...], wa_ref[...])
           + jax.nn.sigmoid(_nn(xb, wg_ref[:, d:2 * d])) * _nn(yr_ref[...], wb_ref[...]))
    z = alpha * x + _nn(mix.astype(BF16), wo_ref[...])
    o_ref[...] = _layer_norm(z, lnw_ref[...], lnb_ref[...])


def _merge(x2, ya, yr, wg, wa, wb, wo, lnw, lnb, tm, alpha):
    m, d = x2.shape
    full = lambda a: pl.BlockSpec(a.shape, lambda i: (0,) * a.ndim)
    row = lambda n: pl.BlockSpec((tm, n), lambda i: (i, 0))
    return pl.pallas_call(
        functools.partial(_merge_kernel, alpha=alpha),
        grid=(m // tm,),
        in_specs=[row(d), row(ya.shape[1]), row(yr.shape[1]),
                  full(wg), full(wa), full(wb), full(wo), full(lnw), full(lnb)],
        out_specs=row(d),
        out_shape=jax.ShapeDtypeStruct((m, d), F32),
        compiler_params=pltpu.CompilerParams(
            dimension_semantics=("parallel",), vmem_limit_bytes=VMEM_LIMIT),
        name="merge_out_ln",
    )(x2, ya, yr, wg, wa, wb, wo, lnw, lnb)


SUBLANES = 8
FFN_COLS = 256
FFN_VMEM_LIMIT = 56 * 1024 * 1024


def _ffn_kernel(h_ref, wu_ref, cw_ref, cb_ref, wd_ref, lnw_ref, lnb_ref, o_ref,
                prev_ref, act_ref, *, alpha, tiles_per_seq):
    tm = h_ref.shape[0]
    dff = wd_ref.shape[0]
    h = h_ref[...]
    hb = h.astype(BF16)

    @pl.when(pl.program_id(0) % tiles_per_seq == 0)
    def _():
        prev_ref[...] = jnp.zeros_like(prev_ref)

    def conv_proj(cols):
        u = _nn(hb, wu_ref[:, cols])
        ext = jnp.concatenate([prev_ref[:, cols], u], axis=0)
        prev_ref[:, cols] = u[tm - SUBLANES:, :]
        w = cw_ref[:, cols]
        back1 = pltpu.roll(ext, 1, 0)[SUBLANES:]
        back2 = pltpu.roll(ext, 2, 0)[SUBLANES:]
        return w[0:1] * back2 + w[1:2] * back1 + w[2:3] * u + cb_ref[:, cols]

    for c in range(dff // FFN_COLS):
        gate = conv_proj(slice(c * FFN_COLS, (c + 1) * FFN_COLS))
        up = conv_proj(slice(dff + c * FFN_COLS, dff + (c + 1) * FFN_COLS))
        act_ref[:, c * FFN_COLS:(c + 1) * FFN_COLS] = (gate * jax.nn.sigmoid(gate) * up).astype(BF16)
    o_ref[...] = _layer_norm(alpha * h + _nn(act_ref[...], wd_ref[...]), lnw_ref[...], lnb_ref[...])


def _ffn(h1, wu, cw, cb, wd, lnw, lnb, tm, seq, alpha):
    m, d = h1.shape
    dff = wd.shape[0]
    assert dff % FFN_COLS == 0 and seq % tm == 0
    const = lambda a: pl.BlockSpec(a.shape, lambda i: (0,) * a.ndim, pipeline_mode=pl.Buffered(1))
    return pl.pallas_call(
        functools.partial(_ffn_kernel, alpha=alpha, tiles_per_seq=seq // tm),
        grid=(m // tm,),
        in_specs=[pl.BlockSpec((tm, d), lambda i: (i, 0))]
        + [const(a) for a in (wu, cw, cb, wd, lnw, lnb)],
        out_specs=pl.BlockSpec((tm, d), lambda i: (i, 0)),
        out_shape=jax.ShapeDtypeStruct((m, d), F32),
        scratch_shapes=[pltpu.VMEM((SUBLANES, 2 * dff), F32), pltpu.VMEM((tm, dff), BF16)],
        compiler_params=pltpu.CompilerParams(
            dimension_semantics=("arbitrary",), vmem_limit_bytes=FFN_VMEM_LIMIT),
        name="conv_ffn_ln",
    )(h1, wu, cw, cb, wd, lnw, lnb)


def _tile_sizes(batch, seq):
    return dict(tm_proj=min(512, seq), tq=min(256, seq), chunk=64, tl_rwkv=min(256, seq),
                nb_rwkv=2 if batch % 2 == 0 else 1, tm_merge=min(512, seq), tm_ffn=min(512, seq))


def _layer(h, w_in, idx_w, idx_b, mu, w0, w_up, a0, a_up, g_up, k_k, k_a, r_k, gn_w, gn_b,
           w_ba, w_br, w_out, ln1_w, ln1_b, w_up_ffn, conv_w, conv_b, w_down, ln2_w, ln2_b, alpha):
    b, seq, d = h.shape
    ts = _tile_sizes(b, seq)
    aw = HEADS * HEAD_DIM
    kvw = KV_HEADS * HEAD_DIM
    o = np.cumsum([0, aw, kvw, kvw, aw, HEAD_DIM, HEADS, 3 * aw + LORA_W, d, d])
    col = lambda i: w_in[:, o[i]:o[i + 1]]
    head = lambda w, hh: w[:, hh * HEAD_DIM:(hh + 1) * HEAD_DIM]
    zero = jnp.zeros((d, HEAD_DIM), w_in.dtype)
    k_cols = [blk for hh in range(KV_HEADS)
              for blk in (head(col(1), hh), zero, zero, head(col(1), hh))]
    wi_pad = jnp.pad(col(5), ((0, 0), (0, LANES - HEADS)))
    wa = jnp.concatenate([col(0)] + k_cols + [col(2), col(3), col(4), col(4), wi_pad],
                         axis=1).astype(BF16)
    wr = col(6).astype(BF16)
    row2 = lambda v: v.reshape(1, -1)
    lnw2 = row2(jnp.concatenate([idx_w, idx_w]))
    lnb2 = row2(jnp.concatenate([idx_b, idx_b]))

    x2 = h.reshape(b * seq, d)
    q, k4, vt, qi, ki2, wi, pr = _inproj(x2, wa, wr, lnw2, lnb2, ts["tm_proj"], seq)
    r3 = lambda a: a.reshape(b, seq, a.shape[-1])

    y_attn = _attention(r3(q), r3(qi), r3(wi), r3(k4), vt, r3(ki2), ts["tq"])

    ld, la = w_up.shape[0], a_up.shape[0]
    pad_rows = lambda w, start: jnp.pad(w, ((start, LORA_W - start - w.shape[0]), (0, 0)))
    vecs = [row2(v) for v in (mu, w0, a0, k_k, k_a, r_k.reshape(-1), gn_w, gn_b)]
    mats = [pad_rows(w_up, 0), pad_rows(a_up, ld), pad_rows(g_up, ld + la)]
    y_rwkv = _rwkv(r3(pr), vecs, mats, ts["chunk"], ts["tl_rwkv"], ts["nb_rwkv"])

    h1 = _merge(x2, y_attn.reshape(b * seq, aw), y_rwkv.reshape(b * seq, aw),
                w_in[:, o[7]:o[9]].astype(BF16),
                w_ba.astype(BF16), w_br.astype(BF16), w_out.astype(BF16),
                row2(ln1_w), row2(ln1_b), ts["tm_merge"], alpha)

    out = _ffn(h1, w_up_ffn.astype(BF16), conv_w, row2(conv_b), w_down.astype(BF16),
               row2(ln2_w), row2(ln2_b), ts["tm_ffn"], seq, alpha)
    return out.reshape(b, seq, d)


def kernel(x, w_in, idx_k_norm_w, idx_k_norm_b, rwkv_mu, rwkv_w0, rwkv_w_up, rwkv_a0, rwkv_a_up,
           rwkv_g_up, rwkv_k_k, rwkv_k_a, rwkv_r_k, rwkv_ln_w, rwkv_ln_b, w_branch_attn,
           w_branch_rwkv, w_out, ln1_w, ln1_b, w_up, conv_w, conv_b, w_down, ln2_w, ln2_b):
    depth = w_in.shape[0]
    alpha = (2.0 * depth) ** 0.25
    h = x
    for l in range(depth):
        h = _layer(h, w_in[l], idx_k_norm_w[l], idx_k_norm_b[l], rwkv_mu[l], rwkv_w0[l],
                   rwkv_w_up[l], rwkv_a0[l], rwkv_a_up[l], rwkv_g_up[l], rwkv_k_k[l], rwkv_k_a[l],
                   rwkv_r_k[l], rwkv_ln_w[l], rwkv_ln_b[l], w_branch_attn[l], w_branch_rwkv[l],
                   w_out[l], ln1_w[l], ln1_b[l], w_up[l], conv_w[l], conv_b[l], w_down[l],
                   ln2_w[l], ln2_b[l], alpha)
    return h
```

```python
import functools

import jax
import jax.numpy as jnp
import ml_dtypes
import numpy as np
from jax import lax
from jax.experimental import pallas as pl
from jax.experimental.pallas import tpu as pltpu

F32 = jnp.float32
BF16 = jnp.bfloat16
I32 = jnp.int32

LANES = 128
HEAD_DIM = 64
HEADS = 8
KV_HEADS = 2
TOPK_MAX = 256
LORA_W = 128
LN_EPS = 1e-5
GN_EPS = 64e-5
ATTN_CLASSES = 4
SCORE_COLS = 512
BISECT_PASSES = 16
BISECT_ROWS = 256
BISECT_UNROLL = 8
VMEM_LIMIT = 48 * 1024 * 1024


def _nt(a, b):
    return lax.dot_general(a, b, (((1,), (1,)), ((), ())), preferred_element_type=F32)


def _nn(a, b):
    return lax.dot_general(a, b, (((1,), (0,)), ((), ())), preferred_element_type=F32)


def _bnt(a, b):
    return lax.dot_general(a, b, (((2,), (2,)), ((0,), (0,))), preferred_element_type=F32)


def _bnn(a, b):
    return lax.dot_general(a, b, (((2,), (1,)), ((0,), (0,))), preferred_element_type=F32)


LOG2E = 1.4426950408889634
POS_SPLIT = 16
SLOPE_TERMS = 3
ATTN_W = HEADS * HEAD_DIM
K4_W = 2 * KV_HEADS * LANES
_QKV_COLS = np.cumsum([0, ATTN_W, K4_W, KV_HEADS * HEAD_DIM, ATTN_W, LANES, LANES])


def _inproj_kernel(x_ref, wa_ref, wr_ref, lnw_ref, lnb_ref,
                   q_ref, k4_ref, vt_ref, qi_ref, ki2_ref, wi_ref, pr_ref, *, seq):
    tm = x_ref.shape[0]
    xb = x_ref[...].astype(BF16)
    pa = _nn(xb, wa_ref[...])
    part = lambda i: pa[:, _QKV_COLS[i]:_QKV_COLS[i + 1]]
    q_ref[...] = (part(0) * (HEAD_DIM ** -0.5 * LOG2E)).astype(BF16)
    pos = (pl.program_id(0) * tm) % seq + lax.broadcasted_iota(I32, (tm, 2 * LANES), 0)
    s_lo = pos & (POS_SPLIT - 1)
    ln = lax.broadcasted_iota(I32, (tm, 2 * LANES), 1)
    a = jnp.where(ln < LANES, ln - HEAD_DIM, ln - LANES)
    in_aug = (a >= 0) & (a < 2 * SLOPE_TERMS)
    aug = jnp.where(in_aug, jnp.where((a & 1) == 0, pos - s_lo, s_lo), 0).astype(F32)
    keys = part(1)
    for g in range(KV_HEADS):
        sl = slice(g * 2 * LANES, (g + 1) * 2 * LANES)
        k4_ref[:, sl] = (keys[:, sl] + aug).astype(BF16)
    vt = part(2).T
    vt_swapped = pltpu.roll(vt, HEAD_DIM, 0)
    vrow = lax.broadcasted_iota(I32, vt.shape, 0)
    top = vrow < HEAD_DIM
    one_at = lambda r: jnp.where(vrow == r, 1.0, 0.0)
    for g, (v_top, v_bottom) in enumerate(((vt, vt_swapped), (vt_swapped, vt))):
        vt_ref[(2 * g) * LANES:(2 * g + 1) * LANES, :] = jnp.where(top, v_top, one_at(HEAD_DIM)).astype(BF16)
        vt_ref[(2 * g + 1) * LANES:(2 * g + 2) * LANES, :] = jnp.where(top, one_at(0), v_bottom).astype(BF16)
    qi_ref[...] = part(3).astype(BF16)
    ki = part(4)
    mu = jnp.mean(ki, axis=-1, keepdims=True)
    var = jnp.mean(jnp.square(ki - mu), axis=-1, keepdims=True)
    ki2_ref[...] = ((ki - mu) * lax.rsqrt(var + LN_EPS) * lnw_ref[...] + lnb_ref[...]).astype(BF16)
    wi_ref[...] = part(5)
    pr_ref[...] = _nn(xb, wr_ref[...])


def _inproj(x2, wa, wr, lnw2, lnb2, tm, seq):
    m, d = x2.shape
    assert seq <= 2 ** 8 * POS_SPLIT and KV_HEADS == 2 and wa.shape[1] == _QKV_COLS[-1]
    full = lambda a: pl.BlockSpec(a.shape, lambda i: (0,) * a.ndim)
    row = lambda n: pl.BlockSpec((tm, n), lambda i: (i, 0))
    outs = [(ATTN_W, BF16), (K4_W, BF16), None, (ATTN_W, BF16), (LANES, BF16), (LANES, F32),
            (wr.shape[1], F32)]
    vt_rows = 2 * KV_HEADS * LANES
    return pl.pallas_call(
        functools.partial(_inproj_kernel, seq=seq),
        grid=(m // tm,),
        in_specs=[row(d), full(wa), full(wr), full(lnw2), full(lnb2)],
        out_specs=[row(o[0]) if o else pl.BlockSpec((vt_rows, tm), lambda i: (0, i)) for o in outs],
        out_shape=[jax.ShapeDtypeStruct((m, o[0]), o[1]) if o
                   else jax.ShapeDtypeStruct((vt_rows, m), BF16) for o in outs],
        compiler_params=pltpu.CompilerParams(
            dimension_semantics=("parallel",), vmem_limit_bytes=VMEM_LIMIT),
        name="inproj",
    )(x2, wa, wr, lnw2, lnb2)


def _bf16_parts(x):
    parts = []
    for _ in range(SLOPE_TERMS):
        parts.append(float(ml_dtypes.bfloat16(x)))
        x -= parts[-1]
    return parts


def _attn_body(q_ref, qi_ref, wi_ref, k4_ref, vt_ref, ki2_ref, o_ref,
               key_ref, bias_ref, *, t0, t0_max, tq, width, n_sel, cw):
    lane = lax.broadcasted_iota(I32, (tq, LANES), 1)
    lo = lane < HEAD_DIM
    n_cc = width // cw
    n_lc = width // LANES

    def head_lhs(ref, h, fill=None):
        pair = ref[0, :, (h // 2) * LANES:(h // 2 + 1) * LANES]
        fill = jnp.zeros_like(pair) if fill is None else fill
        return jnp.where(lo if h % 2 == 0 else jnp.logical_not(lo), pair, fill)

    wi = wi_ref[0] * (HEAD_DIM ** -0.5 * HEADS ** -0.5)
    row_c = t0 + lax.broadcasted_iota(I32, (tq, cw), 0)
    col_c = lax.broadcasted_iota(I32, (tq, cw), 1)
    for c in range(n_cc):
        kc = ki2_ref[0, c * cw:(c + 1) * cw, :]
        s = jnp.zeros((tq, cw), F32)
        for h in range(HEADS):
            s = s + jnp.maximum(_nt(head_lhs(qi_ref, h), kc), 0.0) * wi[:, h:h + 1]
        key_ref[:, c * cw:(c + 1) * cw] = jnp.where(col_c + c * cw <= row_c, s, -jnp.inf)

    kf = jnp.float32(n_sel)
    rb = min(BISECT_ROWS, tq)
    n_rb = tq // rb
    blocks = [slice(b * rb, (b + 1) * rb) for b in range(n_rb)]
    sls = [slice(c * LANES, (c + 1) * LANES) for c in range(n_lc)]
    n_live = {blk.start: min(n_lc, -(-(t0_max + blk.stop) // LANES)) for blk in blocks}

    def lanes_all(x):
        return jnp.broadcast_to(x, (rb, LANES))

    def count(pred_fn, rows, read=lambda ref, rows, sl: ref[rows, sl], ref=key_ref):
        acc = jnp.zeros((rb, LANES), F32)
        for sl in sls[:n_live[rows.start]]:
            acc = acc + jnp.where(pred_fn(read(ref, rows, sl)), 1.0, 0.0)
        return lanes_all(jnp.sum(acc, axis=1, keepdims=True))

    def smallest(pred_fn, rows):
        acc = jnp.full((rb, LANES), jnp.inf, F32)
        for sl in sls[:n_live[rows.start]]:
            sc = key_ref[rows, sl]
            acc = jnp.minimum(acc, jnp.where(pred_fn(sc), sc, jnp.inf))
        return lanes_all(jnp.min(acc, axis=1, keepdims=True))

    def largest(rows):
        acc = jnp.full((rb, LANES), -jnp.inf, F32)
        for sl in sls[:n_live[rows.start]]:
            acc = jnp.maximum(acc, key_ref[rows, sl])
        return lanes_all(jnp.max(acc, axis=1, keepdims=True))

    def bis_body(_, bounds):
        out = []
        for rows, (low, high) in zip(blocks, bounds):
            mid = 0.5 * low + 0.5 * high
            enough = count(lambda sc: sc >= mid, rows) >= kf
            out.append((jnp.where(enough, mid, low), jnp.where(enough, high, mid)))
        return tuple(out)

    bounds = tuple((smallest(lambda sc: sc > -jnp.inf, rows), largest(rows)) for rows in blocks)
    bounds = lax.fori_loop(0, BISECT_PASSES, bis_body, bounds, unroll=BISECT_UNROLL)
    lows = [low for low, _ in bounds]


    def settle(m, thr, done, rows):
        newly = jnp.where(count(lambda sc: sc > m, rows) < kf, 1.0 - done, 0.0)
        return jnp.where(newly > 0.0, m, thr), jnp.maximum(done, newly)

    walk = []
    for rows, low in zip(blocks, lows):
        m = smallest(lambda sc: sc >= low, rows)
        walk.append((m,) + settle(m, low, jnp.zeros((rb, LANES), F32), rows))

    def walk_pending(state):
        return functools.reduce(jnp.minimum, [jnp.min(done) for _, _, done in state]) < 1.0

    def walk_step(state):
        out = []
        for rows, (m, thr, done) in zip(blocks, state):
            m = smallest(lambda sc: sc > m, rows)
            out.append((m,) + settle(m, thr, done, rows))
        return tuple(out)

    thrs = [thr for _, thr, _ in lax.while_loop(walk_pending, walk_step, tuple(walk))]

    excess = []
    for rows, t in zip(blocks, thrs):
        acc = jnp.zeros((rb, LANES), F32)
        for sl in sls:
            hit = key_ref[rows, sl] >= t
            bias_ref[rows, sl] = jnp.where(hit, 0.0, -jnp.inf)
            acc = acc + jnp.where(hit, 1.0, 0.0)
        excess.append(lanes_all(jnp.sum(acc, axis=1, keepdims=True)) - kf)
    has_ties = functools.reduce(jnp.maximum, [jnp.max(e) for e in excess]) > 0.0

    @pl.when(has_ties)
    def _():
        lane_rb = lax.broadcasted_iota(I32, (rb, LANES), 1).astype(F32)
        far = jnp.float32(2 ** 30)
        for rows, t, surplus in zip(blocks, thrs, excess):
            for c, sl in enumerate(sls):
                bias_ref[rows, sl] = jnp.where(key_ref[rows, sl] == t, lane_rb + float(c * LANES), far)

            def drop_pending(state):
                return jnp.max(state[1]) > 0.0

            def drop_step(state, rows=rows):
                cut, left = state
                acc = jnp.full((rb, LANES), -1.0, F32)
                for sl in sls[:n_live[rows.start]]:
                    tpos = bias_ref[rows, sl]
                    acc = jnp.maximum(acc, jnp.where(tpos < cut, tpos, -1.0))
                highest = lanes_all(jnp.max(acc, axis=1, keepdims=True))
                return jnp.where(left > 0.0, highest, cut), left - 1.0

            cut, _ = lax.while_loop(drop_pending, drop_step, (jnp.full((rb, LANES), far, F32), surplus))
            for sl in sls:
                keep = jnp.where(bias_ref[rows, sl] < cut, 0.0, -jnp.inf)
                bias_ref[rows, sl] = jnp.where(key_ref[rows, sl] > t, 0.0, keep)

    bias = bias_ref[:, 0:width]
    a_idx = jnp.where(lo, lane, lane - HEAD_DIM)

    def qk(h):
        parts = _bf16_parts((2.0 ** -(h + 1)) * LOG2E)
        fill = jnp.zeros((tq, LANES), F32)
        for n in reversed(range(SLOPE_TERMS)):
            fill = jnp.where(a_idx < 2 * (n + 1), parts[n], fill)
        kcol = (2 * (h // (HEADS // KV_HEADS)) + h % 2) * LANES
        return _nt(head_lhs(q_ref, h, fill.astype(BF16)), k4_ref[0, 0:width, kcol:kcol + LANES])

    even = None
    qk_next = qk(0)
    for h in range(HEADS):
        g = h // (HEADS // KV_HEADS)
        qk_cur, qk_next = qk_next, (qk(h + 1) if h + 1 < HEADS else None)
        halves = []
        for r0 in range(0, tq, tq // 2):
            logit = qk_cur[r0:r0 + tq // 2] + bias[r0:r0 + tq // 2]
            m = jnp.max(logit, axis=1, keepdims=True)
            halves.append(jnp.exp2(logit - m))
        e = jnp.concatenate(halves, axis=0)
        vrows = (2 * g + h % 2) * LANES
        out = _nt(vt_ref[vrows:vrows + LANES, 0:width], e.astype(BF16)).T
        if h % 2 == 0:
            even = out
        else:
            p = h // 2
            denom = jnp.where(lo, jnp.broadcast_to(even[:, HEAD_DIM:HEAD_DIM + 1], (tq, LANES)),
                              jnp.broadcast_to(out[:, 0:1], (tq, LANES)))
            o_ref[0, :, p * LANES:(p + 1) * LANES] = (jnp.where(lo, even, out) / denom).astype(o_ref.dtype)


def _attn_kernel(q_ref, qi_ref, wi_ref, k4_ref, vt_ref, ki2_ref, o_ref,
                 key_ref, bias_ref, *, tq, seq, n_sel, n_classes):
    i = pl.program_id(0)
    tiles_per_class = (seq // tq) // n_classes
    ends = [(c + 1) * tiles_per_class for c in range(n_classes)]
    for first, last in zip([0] + ends[:-1], ends):
        width = last * tq

        @pl.when((i >= first) & (i < last))
        def _():
            _attn_body(q_ref, qi_ref, wi_ref, k4_ref, vt_ref, ki2_ref, o_ref, key_ref, bias_ref,
                       t0=i * tq, t0_max=width - tq, tq=tq, width=width, n_sel=n_sel,
                       cw=SCORE_COLS if width % SCORE_COLS == 0 else SCORE_COLS // 2)


def _attention(q, qi, wi, k4, vt, ki2, tq):
    b, seq, _ = q.shape
    n_sel = min(TOPK_MAX, seq // 4)
    n_classes = min(ATTN_CLASSES, seq // tq)
    qblk = lambda n: pl.BlockSpec((1, tq, n), lambda i, bi: (bi, i, 0))
    kblk = lambda n: pl.BlockSpec((1, seq, n), lambda i, bi: (bi, 0, 0))
    vblk = pl.BlockSpec((vt.shape[0], seq), lambda i, bi: (0, bi))
    return pl.pallas_call(
        functools.partial(_attn_kernel, tq=tq, seq=seq, n_sel=n_sel, n_classes=n_classes),
        grid=(seq // tq, b),
        in_specs=[qblk(ATTN_W), qblk(ATTN_W), qblk(LANES), kblk(K4_W), vblk, kblk(LANES)],
        out_specs=qblk(ATTN_W),
        out_shape=jax.ShapeDtypeStruct((b, seq, ATTN_W), BF16),
        scratch_shapes=[pltpu.VMEM((tq, seq), F32), pltpu.VMEM((tq, seq), F32)],
        compiler_params=pltpu.CompilerParams(
            dimension_semantics=("parallel", "parallel"), vmem_limit_bytes=VMEM_LIMIT),
        name="dsa_attention",
    )(q, qi, wi, k4, vt, ki2)


def _bf(x):
    return x.astype(BF16)


def _rwkv_kernel(pr_ref, mu_ref, w0_ref, wup_ref, a0_ref, aup_ref, gup_ref, kk_ref, ka_ref,
                 rk_ref, lnw_ref, lnb_ref, o_ref,
                 state_ref, prev_ref, tinv_ref, w_ref, atrt_ref, bhm_ref, y_ref, *, c):
    nb, tl, width = o_ref.shape
    n_pairs = width // LANES
    n_chunks = tl // c
    rows2 = 2 * c
    n_bp = nb * n_pairs

    @pl.when(pl.program_id(1) == 0)
    def _():
        state_ref[...] = jnp.zeros_like(state_ref)
        prev_ref[...] = jnp.zeros_like(prev_ref)

    ti = lax.broadcasted_iota(I32, (tl, tl), 0)
    tj = lax.broadcasted_iota(I32, (tl, tl), 1)
    same_chunk = (ti // c) == (tj // c)
    tri = jnp.where(same_chunk & (ti >= tj), 1.0, 0.0).astype(BF16)
    blk = jnp.where(same_chunk, 1.0, 0.0).astype(BF16)
    li = lax.broadcasted_iota(I32, (LANES, LANES), 0)
    lj = lax.broadcasted_iota(I32, (LANES, LANES), 1)
    ones_bd = jnp.where((li // HEAD_DIM) == (lj // HEAD_DIM), 1.0, 0.0).astype(BF16)
    upper_strict = li < lj
    upper_incl = li <= lj
    eye = jnp.where(li == lj, 1.0, 0.0)
    lo = lax.broadcasted_iota(I32, (c, LANES), 1) < HEAD_DIM
    rows = lax.broadcasted_iota(I32, (tl, pr_ref.shape[-1]), 0)

    def head_sum(x):
        return _nn(_bf(x), ones_bd)

    def stack(x):
        return _bf(jnp.concatenate([jnp.where(lo, x, 0.0), jnp.where(lo, 0.0, x)], axis=0))

    n_pc = n_chunks * n_bp
    atrt_l, btkt_l, vst_l, kh_l, bh_l = ([None] * n_pc for _ in range(5))
    kept = []
    for bi in range(nb):
        p = pr_ref[bi]
        shifted = jnp.where(rows == 0, prev_ref[bi, 7:8, :], pltpu.roll(p, 1, 0))
        prev_ref[bi] = p[tl - 8:tl, :]
        ps = p + (shifted - p) * mu_ref[...]
        r = ps[:, 0:width]
        k = ps[:, width:2 * width]
        v = ps[:, 2 * width:3 * width]
        lora = ps[:, 3 * width:3 * width + LORA_W]
        nz = -(w0_ref[...] + _nn(_bf(jnp.tanh(lora)), wup_ref[...]))
        log_w = -(jnp.maximum(nz, 0.0) + jnp.log(1.0 + jnp.exp(-jnp.abs(nz)))) - 0.5
        lw = -jnp.exp(log_w)
        a_sig = jax.nn.sigmoid(a0_ref[...] + _nn(_bf(lora), aup_ref[...]))
        gate = _nn(_bf(jax.nn.sigmoid(lora)), gup_ref[...])
        lw_hi = _bf(lw)
        lw_lo = _bf(lw - lw_hi.astype(F32))
        cum = _nn(tri, lw_hi) + _nn(tri, lw_lo)
        tot = _nn(blk, lw_hi) + _nn(blk, lw_lo)
        kept.append((r, k, v, a_sig, gate, tot))
        for pi in range(n_pairs):
            sl = slice(pi * LANES, (pi + 1) * LANES)
            r_p, k_p, v_p, a_p = r[:, sl], k[:, sl], v[:, sl], a_sig[:, sl]
            kk = k_p * kk_ref[:, sl]
            kk = kk * lax.rsqrt(jnp.maximum(head_sum(kk * kk), 1e-24))
            kmod = k_p * (1.0 + (a_p - 1.0) * ka_ref[:, sl])
            bvec = kk * a_p
            cum_p, tot_p = cum[:, sl], tot[:, sl]
            p_inv = jnp.exp(-cum_p)
            p_end = jnp.exp(tot_p - cum_p)
            at_all = -kk * jnp.exp(cum_p - lw[:, sl])
            rt_all = r_p * jnp.exp(cum_p)
            bt_all, kt_all = bvec * p_inv, kmod * p_inv
            bh_all, kh_all = bvec * p_end, kmod * p_end
            for ci in range(n_chunks):
                rs = slice(ci * c, (ci + 1) * c)
                idx = (ci * nb + bi) * n_pairs + pi
                atrt_l[idx] = jnp.concatenate([stack(at_all[rs]), stack(rt_all[rs])], axis=0)
                btkt_l[idx] = jnp.concatenate([stack(bt_all[rs]), stack(kt_all[rs])], axis=0)
                vst_l[idx] = _bf(jnp.concatenate([jnp.where(lo, v_p[rs], 0.0),
                                                  jnp.where(lo, 0.0, v_p[rs])], axis=0).T)
                kh_l[idx] = stack(kh_all[rs])
                bh_l[idx] = stack(bh_all[rs])

    atrt = jnp.stack(atrt_l)
    gt = _bnt(jnp.stack(btkt_l), atrt)
    n_t = jnp.where(upper_strict, gt[:, 0:rows2, 0:rows2], 0.0)
    m_rb_t = jnp.where(upper_incl, gt[:, 0:rows2, rows2:], 0.0)
    m_ak_t = jnp.where(upper_strict, gt[:, rows2:, 0:rows2], 0.0)
    m_rk_t = jnp.where(upper_incl, gt[:, rows2:, rows2:], 0.0)
    tinv = eye + n_t
    npow = _bf(n_t)
    for _ in range(int(np.log2(c)) - 1):
        npow = _bf(_bnn(npow, npow))
        tinv = tinv + _bnn(_bf(tinv), npow)
    rhs = jnp.concatenate([_bf(m_ak_t), jnp.stack(kh_l), _bf(m_rk_t)], axis=2)
    w_ref[...] = _bnn(jnp.stack(vst_l), rhs)
    tinv_ref[...] = _bf(tinv)
    atrt_ref[...] = atrt
    bhm_ref[...] = jnp.concatenate([jnp.stack(bh_l), _bf(m_rb_t)], axis=2)

    for ci in range(n_chunks):
        rs = slice(ci * c, (ci + 1) * c)
        ids = slice(ci * n_bp, (ci + 1) * n_bp)
        state = state_ref[...]
        x = _bnt(_bf(state), atrt_ref[ids])
        w = w_ref[ids]
        us_t = _bnn(_bf(x[:, :, 0:rows2] + w[:, :, 0:rows2]), tinv_ref[ids])
        z = _bnn(_bf(us_t), bhm_ref[ids])
        decay_c = jnp.exp(jnp.stack([kept[bi][5][ci * c:ci * c + 1, pi * LANES:(pi + 1) * LANES]
                                     for bi in range(nb) for pi in range(n_pairs)]))
        state_ref[...] = state * decay_c + z[:, :, 0:rows2] + w[:, :, rows2:2 * rows2]
        ys_t = x[:, :, rows2:] + z[:, :, rows2:] + w[:, :, 2 * rows2:]
        for bi in range(nb):
            for pi in range(n_pairs):
                ys = ys_t[bi * n_pairs + pi].T
                y_ref[bi, rs, pi * LANES:(pi + 1) * LANES] = ys[0:c, :] + ys[c:rows2, :]

    for bi in range(nb):
        r, k, v, a_sig, gate, _ = kept[bi]
        for pi in range(n_pairs):
            sl = slice(pi * LANES, (pi + 1) * LANES)
            y = y_ref[bi, :, sl]
            mean = head_sum(y) * (1.0 / HEAD_DIM)
            yc = y - mean
            var = head_sum(yc * yc) * (1.0 / HEAD_DIM)
            yn = yc * lax.rsqrt(var + GN_EPS) * lnw_ref[:, sl] + lnb_ref[:, sl]
            kmod = k[:, sl] * (1.0 + (a_sig[:, sl] - 1.0) * ka_ref[:, sl])
            bonus = head_sum(r[:, sl] * kmod * rk_ref[:, sl]) * v[:, sl]
            o_ref[bi, :, sl] = ((yn + bonus) * gate[:, sl]).astype(o_ref.dtype)


def _rwkv(pr, vecs, mats, c, tl, nb):
    b, seq, pw = pr.shape
    width = (pw - LORA_W) // 3
    n_pairs = width // LANES
    n_pc = (tl // c) * nb * n_pairs
    full = lambda a: pl.BlockSpec(a.shape, lambda bi, i: (0,) * a.ndim)
    mu, w0, a0, kk, ka, rk, lnw, lnb = vecs
    wup, aup, gup = [m.astype(BF16) for m in mats]
    return pl.pallas_call(
        functools.partial(_rwkv_kernel, c=c),
        grid=(b // nb, seq // tl),
        in_specs=[pl.BlockSpec((nb, tl, pw), lambda bi, i: (bi, i, 0)),
                  full(mu), full(w0), full(wup), full(a0), full(aup), full(gup),
                  full(kk), full(ka), full(rk), full(lnw), full(lnb)],
        out_specs=pl.BlockSpec((nb, tl, width), lambda bi, i: (bi, i, 0)),
        out_shape=jax.ShapeDtypeStruct((b, seq, width), BF16),
        scratch_shapes=[pltpu.VMEM((nb * n_pairs, LANES, LANES), F32),
                        pltpu.VMEM((nb, 8, pw), F32),
                        pltpu.VMEM((n_pc, 2 * c, 2 * c), BF16),
                        pltpu.VMEM((n_pc, 2 * c, 6 * c), F32),
                        pltpu.VMEM((n_pc, 4 * c, LANES), BF16),
                        pltpu.VMEM((n_pc, 2 * c, 4 * c), BF16),
                        pltpu.VMEM((nb, tl, width), F32)],
        compiler_params=pltpu.CompilerParams(
            dimension_semantics=("parallel", "arbitrary"), vmem_limit_bytes=VMEM_LIMIT),
        name="rwkv7",
    )(pr, mu, w0, wup, a0, aup, gup, kk, ka, rk, lnw, lnb)


def _layer_norm(z, w, b):
    mu = jnp.mean(z, axis=-1, keepdims=True)
    zc = z - mu
    var = jnp.mean(zc * zc, axis=-1, keepdims=True)
    return zc * lax.rsqrt(var + LN_EPS) * w + b


def _merge_kernel(x_ref, ya_ref, yr_ref, wg_ref, wa_ref, wb_ref, wo_ref, lnw_ref, lnb_ref,
                  o_ref, *, alpha):
    x = x_ref[...]
    d = x.shape[-1]
    xb = x.astype(BF16)
    mix = (jax.nn.sigmoid(_nn(xb, wg_ref[:, 0:d])) * _nn(ya_ref[...], wa_ref[...])
           + jax.nn.sigmoid(_nn(xb, wg_ref[:, d:2 * d])) * _nn(yr_ref[...], wb_ref[...]))
    z = alpha * x + _nn(mix.astype(BF16), wo_ref[...])
    o_ref[...] = _layer_norm(z, lnw_ref[...], lnb_ref[...])


def _merge(x2, ya, yr, wg, wa, wb, wo, lnw, lnb, tm, alpha):
    m, d = x2.shape
    full = lambda a: pl.BlockSpec(a.shape, lambda i: (0,) * a.ndim)
    row = lambda n: pl.BlockSpec((tm, n), lambda i: (i, 0))
    return pl.pallas_call(
        functools.partial(_merge_kernel, alpha=alpha),
        grid=(m // tm,),
        in_specs=[row(d), row(ya.shape[1]), row(yr.shape[1]),
                  full(wg), full(wa), full(wb), full(wo), full(lnw), full(lnb)],
        out_specs=row(d),
        out_shape=jax.ShapeDtypeStruct((m, d), F32),
        compiler_params=pltpu.CompilerParams(
            dimension_semantics=("parallel",), vmem_limit_bytes=VMEM_LIMIT),
        name="merge_out_ln",
    )(x2, ya, yr, wg, wa, wb, wo, lnw, lnb)


SUBLANES = 8
FFN_COLS = 256
FFN_VMEM_LIMIT = 56 * 1024 * 1024


def _ffn_kernel(h_ref, wu_ref, cw_ref, cb_ref, wd_ref, lnw_ref, lnb_ref, o_ref,
                prev_ref, act_ref, *, alpha, tiles_per_seq):
    tm = h_ref.shape[0]
    dff = wd_ref.shape[0]
    h = h_ref[...]
    hb = h.astype(BF16)

    @pl.when(pl.program_id(0) % tiles_per_seq == 0)
    def _():
        prev_ref[...] = jnp.zeros_like(prev_ref)

    def conv_proj(cols):
        u = _nn(hb, wu_ref[:, cols])
        ext = jnp.concatenate([prev_ref[:, cols], u], axis=0)
        prev_ref[:, cols] = u[tm - SUBLANES:, :]
        w = cw_ref[:, cols]
        back1 = pltpu.roll(ext, 1, 0)[SUBLANES:]
        back2 = pltpu.roll(ext, 2, 0)[SUBLANES:]
        return w[0:1] * back2 + w[1:2] * back1 + w[2:3] * u + cb_ref[:, cols]

    for c in range(dff // FFN_COLS):
        gate = conv_proj(slice(c * FFN_COLS, (c + 1) * FFN_COLS))
        up = conv_proj(slice(dff + c * FFN_COLS, dff + (c + 1) * FFN_COLS))
        act_ref[:, c * FFN_COLS:(c + 1) * FFN_COLS] = (gate * jax.nn.sigmoid(gate) * up).astype(BF16)
    o_ref[...] = _layer_norm(alpha * h + _nn(act_ref[...], wd_ref[...]), lnw_ref[...], lnb_ref[...])


def _ffn(h1, wu, cw, cb, wd, lnw, lnb, tm, seq, alpha):
    m, d = h1.shape
    dff = wd.shape[0]
    assert dff % FFN_COLS == 0 and seq % tm == 0
    const = lambda a: pl.BlockSpec(a.shape, lambda i: (0,) * a.ndim, pipeline_mode=pl.Buffered(1))
    return pl.pallas_call(
        functools.partial(_ffn_kernel, alpha=alpha, tiles_per_seq=seq // tm),
        grid=(m // tm,),
        in_specs=[pl.BlockSpec((tm, d), lambda i: (i, 0))]
        + [const(a) for a in (wu, cw, cb, wd, lnw, lnb)],
        out_specs=pl.BlockSpec((tm, d), lambda i: (i, 0)),
        out_shape=jax.ShapeDtypeStruct((m, d), F32),
        scratch_shapes=[pltpu.VMEM((SUBLANES, 2 * dff), F32), pltpu.VMEM((tm, dff), BF16)],
        compiler_params=pltpu.CompilerParams(
            dimension_semantics=("arbitrary",), vmem_limit_bytes=FFN_VMEM_LIMIT),
        name="conv_ffn_ln",
    )(h1, wu, cw, cb, wd, lnw, lnb)


def _tile_sizes(batch, seq):
    return dict(tm_proj=min(512, seq), tq=min(256, seq), chunk=64, tl_rwkv=min(256, seq),
                nb_rwkv=2 if batch % 2 == 0 else 1, tm_merge=min(512, seq), tm_ffn=min(512, seq))


def _layer(h, w_in, idx_w, idx_b, mu, w0, w_up, a0, a_up, g_up, k_k, k_a, r_k, gn_w, gn_b,
           w_ba, w_br, w_out, ln1_w, ln1_b, w_up_ffn, conv_w, conv_b, w_down, ln2_w, ln2_b, alpha):
    b, seq, d = h.shape
    ts = _tile_sizes(b, seq)
    aw = HEADS * HEAD_DIM
    kvw = KV_HEADS * HEAD_DIM
    o = np.cumsum([0, aw, kvw, kvw, aw, HEAD_DIM, HEADS, 3 * aw + LORA_W, d, d])
    col = lambda i: w_in[:, o[i]:o[i + 1]]
    head = lambda w, hh: w[:, hh * HEAD_DIM:(hh + 1) * HEAD_DIM]
    zero = jnp.zeros((d, HEAD_DIM), w_in.dtype)
    k_cols = [blk for hh in range(KV_HEADS)
              for blk in (head(col(1), hh), zero, zero, head(col(1), hh))]
    wi_pad = jnp.pad(col(5), ((0, 0), (0, LANES - HEADS)))
    wa = jnp.concatenate([col(0)] + k_cols + [col(2), col(3), col(4), col(4), wi_pad],
                         axis=1).astype(BF16)
    wr = col(6).astype(BF16)
    row2 = lambda v: v.reshape(1, -1)
    lnw2 = row2(jnp.concatenate([idx_w, idx_w]))
    lnb2 = row2(jnp.concatenate([idx_b, idx_b]))

    x2 = h.reshape(b * seq, d)
    q, k4, vt, qi, ki2, wi, pr = _inproj(x2, wa, wr, lnw2, lnb2, ts["tm_proj"], seq)
    r3 = lambda a: a.reshape(b, seq, a.shape[-1])

    y_attn = _attention(r3(q), r3(qi), r3(wi), r3(k4), vt, r3(ki2), ts["tq"])

    ld, la = w_up.shape[0], a_up.shape[0]
    pad_rows = lambda w, start: jnp.pad(w, ((start, LORA_W - start - w.shape[0]), (0, 0)))
    vecs = [row2(v) for v in (mu, w0, a0, k_k, k_a, r_k.reshape(-1), gn_w, gn_b)]
    mats = [pad_rows(w_up, 0), pad_rows(a_up, ld), pad_rows(g_up, ld + la)]
    y_rwkv = _rwkv(r3(pr), vecs, mats, ts["chunk"], ts["tl_rwkv"], ts["nb_rwkv"])

    h1 = _merge(x2, y_attn.reshape(b * seq, aw), y_rwkv.reshape(b * seq, aw),
                w_in[:, o[7]:o[9]].astype(BF16),
                w_ba.astype(BF16), w_br.astype(BF16), w_out.astype(BF16),
                row2(ln1_w), row2(ln1_b), ts["tm_merge"], alpha)

    out = _ffn(h1, w_up_ffn.astype(BF16), conv_w, row2(conv_b), w_down.astype(BF16),
               row2(ln2_w), row2(ln2_b), ts["tm_ffn"], seq, alpha)
    return out.reshape(b, seq, d)


def kernel(x, w_in, idx_k_norm_w, idx_k_norm_b, rwkv_mu, rwkv_w0, rwkv_w_up, rwkv_a0, rwkv_a_up,
           rwkv_g_up, rwkv_k_k, rwkv_k_a, rwkv_r_k, rwkv_ln_w, rwkv_ln_b, w_branch_attn,
           w_branch_rwkv, w_out, ln1_w, ln1_b, w_up, conv_w, conv_b, w_down, ln2_w, ln2_b):
    depth = w_in.shape[0]
    alpha = (2.0 * depth) ** 0.25
    h = x
    for l in range(depth):
        h = _layer(h, w_in[l], idx_k_norm_w[l], idx_k_norm_b[l], rwkv_mu[l], rwkv_w0[l],
                   rwkv_w_up[l], rwkv_a0[l], rwkv_a_up[l], rwkv_g_up[l], rwkv_k_k[l], rwkv_k_a[l],
                   rwkv_r_k[l], rwkv_ln_w[l], rwkv_ln_b[l], w_branch_attn[l], w_branch_rwkv[l],
                   w_out[l], ln1_w[l], ln1_b[l], w_up[l], conv_w[l], conv_b[l], w_down[l],
                   ln2_w[l], ln2_b[l], alpha)
    return h
```
